```python
import jax
import jax.numpy as jnp
from jax import lax
import numpy as np

D_MODEL = 1024
BATCH = 8
SEQ = 4096
DEPTH = 2

HEAD_DIM = 64
ROPE_DIM = HEAD_DIM // 4
ROPE_THETA = 500000.0
Q_BLOCK = 128
NEG_INF = -1e30
RMS_EPS = 1e-6

NSA_HEADS = 4
NSA_CMP_LEN = 32
NSA_CMP_STRIDE = 16
NSA_CMP_HIDDEN = 128
NSA_SEL_LEN = 64
NSA_SEL_TOPK = 16
NSA_WINDOW = 512
NSA_FORCE_SCORE = 1e4

DIL_PATTERNS = ((128, 1), (512, 4), (2048, 16))
DIL_HEADS_PER_GROUP = 2
DIL_HEADS = DIL_HEADS_PER_GROUP * len(DIL_PATTERNS)

FOX_HEADS = 6

D_FF = 2816

NSA_Q_W = NSA_HEADS * HEAD_DIM
NSA_KV_W = 6 * HEAD_DIM
NSA_GATE_W = 3 * NSA_HEADS
DIL_W = DIL_HEADS * HEAD_DIM
DIL_OUT_W = DIL_HEADS_PER_GROUP * HEAD_DIM
FOX_W = FOX_HEADS * HEAD_DIM
IN_WIDTHS = (NSA_Q_W, NSA_KV_W, NSA_GATE_W, 3 * DIL_W, 3 * FOX_W, FOX_HEADS, 3 * D_MODEL)
IN_COLS = sum(IN_WIDTHS)

kernel_name = 'hybrid_nsa_dilated_fox_macaron_block'


def _split_cols(x, widths):
    offsets = [int(o) for o in np.cumsum(widths)[:-1]]
    return jnp.split(x, offsets, axis=-1)


def _rms_norm(x, g):
    xf = x.astype(jnp.float32)
    y = xf * lax.rsqrt(jnp.mean(xf * xf, axis=-1, keepdims=True) + RMS_EPS)
    return (y * g.astype(jnp.float32)).astype(x.dtype)


def _modulate(x, shift, scale):
    return x * (1.0 + scale) + shift


def _swiglu(x, w_in, w_out):
    gate, up = jnp.split(x @ w_in, 2, axis=-1)
    return (jax.nn.silu(gate) * up) @ w_out


def _rope_tables(seq_len):
    inv_freq = ROPE_THETA ** (-jnp.arange(0, ROPE_DIM, 2, dtype=jnp.float32) / ROPE_DIM)
    ang = jnp.arange(seq_len, dtype=jnp.float32)[:, None] * inv_freq[None, :]
    return jnp.cos(ang), jnp.sin(ang)


def _partial_rope(x, cos, sin):
    half = ROPE_DIM // 2
    c = cos[None, :, None, :]
    s = sin[None, :, None, :]
    x1 = x[..., :half].astype(jnp.float32)
    x2 = x[..., half:ROPE_DIM].astype(jnp.float32)
    rot = jnp.concatenate([x1 * c - x2 * s, x2 * c + x1 * s], axis=-1).astype(x.dtype)
    return jnp.concatenate([rot, x[..., ROPE_DIM:]], axis=-1)


def _masked_softmax(s, mask):
    s = jnp.where(mask, s, NEG_INF)
    m = jnp.max(s, axis=-1, keepdims=True)
    e = jnp.where(mask, jnp.exp(s - m), 0.0)
    den = jnp.sum(e, axis=-1, keepdims=True)
    den = jnp.where(den > 0, den, 1.0)
    return e / den, m + jnp.log(den)


def _nsa_attention(q, k_cmp, v_cmp, k_sel, v_sel, k_win, v_win, gate_logits,
                   cmp_pe, cmp_w1, cmp_w2):
    B, S = q.shape[0], q.shape[1]
    scale = HEAD_DIM ** -0.5
    n_cmp = (S - NSA_CMP_LEN) // NSA_CMP_STRIDE + 1
    cmp_start = np.arange(n_cmp) * NSA_CMP_STRIDE
    cmp_idx = cmp_start[:, None] + np.arange(NSA_CMP_LEN)[None, :]

    def compress(t, j):
        blk = t[:, cmp_idx] + cmp_pe[j]
        hid = jax.nn.silu(blk.reshape(B, n_cmp, NSA_CMP_LEN * HEAD_DIM) @ cmp_w1[j])
        return hid @ cmp_w2[j]

    kc = compress(k_cmp, 0)
    vc = compress(v_cmp, 1)
    cmp_end = jnp.asarray(cmp_start + NSA_CMP_LEN - 1)
    n_sel = S // NSA_SEL_LEN
    sel_start = np.arange(n_sel) * NSA_SEL_LEN
    ov = np.minimum(cmp_start[:, None] + NSA_CMP_LEN, sel_start[None, :] + NSA_SEL_LEN) \
        - np.maximum(cmp_start[:, None], sel_start[None, :])
    overlap = jnp.asarray(np.clip(ov, 0, None) / NSA_CMP_LEN, dtype=jnp.float32)
    top_k = min(NSA_SEL_TOPK, n_sel)
    sel_ids = jnp.arange(n_sel)
    kw_pad = jnp.pad(k_win, ((0, 0), (NSA_WINDOW, 0), (0, 0)))
    vw_pad = jnp.pad(v_win, ((0, 0), (NSA_WINDOW, 0), (0, 0)))
    win_len = NSA_WINDOW + Q_BLOCK
    gather = jax.vmap(lambda a, i: a[i])

    def block(qb):
        q0 = qb * Q_BLOCK
        t = q0 + jnp.arange(Q_BLOCK)
        qblk = lax.dynamic_slice_in_dim(q, q0, Q_BLOCK, axis=1)
        s = jnp.einsum('bqhd,bnd->bhqn', qblk, kc).astype(jnp.float32) * scale
        p_cmp, _ = _masked_softmax(s, cmp_end[None, :] <= t[:, None])
        o_cmp = jnp.einsum('bhqn,bnd->bqhd', p_cmp.astype(vc.dtype), vc)
        imp = jnp.einsum('bhqn,nj->bqj', p_cmp, overlap)
        valid = (sel_ids[None, :] * NSA_SEL_LEN) <= t[:, None]
        forced = (sel_ids[None, :] == (t // NSA_SEL_LEN)[:, None]) | (sel_ids[None, :] == 0)
        imp = jnp.where(forced, NSA_FORCE_SCORE, jnp.where(valid, imp, -1.0))
        _, blk_idx = lax.top_k(imp, top_k)
        tok = (blk_idx[..., None] * NSA_SEL_LEN + jnp.arange(NSA_SEL_LEN)).reshape(B, Q_BLOCK * top_k * NSA_SEL_LEN)
        ks = gather(k_sel, tok).reshape(B, Q_BLOCK, top_k * NSA_SEL_LEN, HEAD_DIM)
        vs = gather(v_sel, tok).reshape(B, Q_BLOCK, top_k * NSA_SEL_LEN, HEAD_DIM)
        s = jnp.einsum('bqhd,bqkd->bhqk', qblk, ks).astype(jnp.float32) * scale
        smask = (tok.reshape(B, Q_BLOCK, top_k * NSA_SEL_LEN) <= t[None, :, None])[:, None]
        p, _ = _masked_softmax(s, smask)
        o_sel = jnp.einsum('bhqk,bqkd->bqhd', p.astype(vs.dtype), vs)
        kw = lax.dynamic_slice_in_dim(kw_pad, q0, win_len, axis=1)
        vw = lax.dynamic_slice_in_dim(vw_pad, q0, win_len, axis=1)
        kpos = q0 - NSA_WINDOW + jnp.arange(win_len)
        wmask = (kpos[None, :] <= t[:, None]) & (kpos[None, :] > t[:, None] - NSA_WINDOW) & (kpos[None, :] >= 0)
        s = jnp.einsum('bqhd,bkd->bhqk', qblk, kw).astype(jnp.float32) * scale
        p, _ = _masked_softmax(s, wmask)
        o_win = jnp.einsum('bhqk,bkd->bqhd', p.astype(vw.dtype), vw)
        g = jax.nn.sigmoid(lax.dynamic_slice_in_dim(gate_logits, q0, Q_BLOCK, axis=1).astype(jnp.float32)).astype(q.dtype)
        return g[..., 0:1] * o_cmp + g[..., 1:2] * o_sel + g[..., 2:3] * o_win

    out = lax.map(block, jnp.arange(S // Q_BLOCK))
    return out.transpose(1, 0, 2, 3, 4).reshape(B, S, NSA_Q_W)


def _dilated_attention(q, k, v):
    B, S = q.shape[0], q.shape[1]
    scale = HEAD_DIM ** -0.5
    hpg = DIL_HEADS_PER_GROUP
    qg_all = [q[:, :, g * hpg:(g + 1) * hpg] for g in range(len(DIL_PATTERNS))]
    kg_all = [k[:, :, g * hpg:(g + 1) * hpg] for g in range(len(DIL_PATTERNS))]
    vg_all = [v[:, :, g * hpg:(g + 1) * hpg] for g in range(len(DIL_PATTERNS))]

    def block(qb):
        q0 = qb * Q_BLOCK
        t = q0 + jnp.arange(Q_BLOCK)
        outs, lses = [], []
        for g, (window, dil) in enumerate(DIL_PATTERNS):
            qg = lax.dynamic_slice_in_dim(qg_all[g], q0, Q_BLOCK, axis=1)
            pos = t[:, None] - dil * jnp.arange(window // dil + 1)[None, :]
            valid = pos >= 0
            idx = jnp.maximum(pos, 0)
            kk = kg_all[g][:, idx]
            vv = vg_all[g][:, idx]
            s = jnp.einsum('bqhd,bqkhd->bhqk', qg, kk).astype(jnp.float32) * scale
            p, lse = _masked_softmax(s, valid)
            outs.append(jnp.einsum('bhqk,bqkhd->bqhd', p.astype(vv.dtype), vv))
            lses.append(lse[..., 0])
        wts = jax.nn.softmax(jnp.stack(lses, axis=0), axis=0)
        wts = wts.transpose(0, 1, 3, 2)[..., None].astype(q.dtype)
        return jnp.sum(wts * jnp.stack(outs, axis=0), axis=0)

    out = lax.map(block, jnp.arange(S // Q_BLOCK))
    return out.transpose(1, 0, 2, 3, 4).reshape(B, S, DIL_OUT_W)


def _forgetting_attention(q, k, v, f_logit, f_bias):
    B, S = q.shape[0], q.shape[1]
    scale = HEAD_DIM ** -0.5
    log_f = jax.nn.log_sigmoid(f_logit.astype(jnp.float32) + f_bias.astype(jnp.float32))
    cum = jnp.cumsum(log_f, axis=1)
    cum_k = cum.transpose(0, 2, 1)[:, :, None, :]
    kpos = jnp.arange(S)

    def block(qb):
        q0 = qb * Q_BLOCK
        t = q0 + jnp.arange(Q_BLOCK)
        qblk = lax.dynamic_slice_in_dim(q, q0, Q_BLOCK, axis=1)
        cq = lax.dynamic_slice_in_dim(cum, q0, Q_BLOCK, axis=1).transpose(0, 2, 1)[..., None]
        s = jnp.einsum('bqhd,bkhd->bhqk', qblk, k).astype(jnp.float32) * scale + (cq - cum_k)
        p, _ = _masked_softmax(s, kpos[None, :] <= t[:, None])
        return jnp.einsum('bhqk,bkhd->bqhd', p.astype(v.dtype), v)

    out = lax.map(block, jnp.arange(S // Q_BLOCK))
    return out.transpose(1, 0, 2, 3, 4).reshape(B, S, FOX_W)


def _token_mixing(n, w_in, cmp_pe, cmp_w1, cmp_w2, fox_bias, br_nsa, br_dil, br_fox, w_out, cos, sin):
    B, S = n.shape[0], n.shape[1]
    proj = n @ w_in
    nsa_q, nsa_kv, nsa_g, dil_qkv, fox_qkv, fox_f, merge_g = _split_cols(proj, IN_WIDTHS)

    def heads(t, h):
        return t.reshape(B, S, h, HEAD_DIM)

    def rope_single(t):
        return _partial_rope(t[:, :, None, :], cos, sin)[:, :, 0, :]

    q_a = _partial_rope(heads(nsa_q, NSA_HEADS), cos, sin)
    kc, vc, ks, vs, kw, vw = jnp.split(nsa_kv, 6, axis=-1)
    y_a = _nsa_attention(q_a, rope_single(kc), vc, rope_single(ks), vs, rope_single(kw), vw,
                         nsa_g.reshape(B, S, NSA_HEADS, 3), cmp_pe, cmp_w1, cmp_w2)
    dq, dk, dv = jnp.split(dil_qkv, 3, axis=-1)
    y_b = _dilated_attention(_partial_rope(heads(dq, DIL_HEADS), cos, sin),
                             _partial_rope(heads(dk, DIL_HEADS), cos, sin),
                             heads(dv, DIL_HEADS))
    fq, fk, fv = jnp.split(fox_qkv, 3, axis=-1)
    y_c = _forgetting_attention(heads(fq, FOX_HEADS), heads(fk, FOX_HEADS), heads(fv, FOX_HEADS),
                                fox_f, fox_bias)
    g_a, g_b, g_c = jnp.split(jax.nn.sigmoid(merge_g), 3, axis=-1)
    merged = g_a * (y_a @ br_nsa) + g_b * (y_b @ br_dil) + g_c * (y_c @ br_fox)
    return merged @ w_out


def setup_inputs(seed: int = 0) -> dict:
    key = jax.random.key(seed)
    ks = jax.random.split(key, 17)
    L, D = DEPTH, D_MODEL

    def nrm(k, shape, scale):
        return jax.random.normal(k, shape, jnp.float32) * scale

    return {
        'x': nrm(ks[0], (BATCH, SEQ, D), 1.0),
        'c': nrm(ks[1], (BATCH, D), 1.0),
        'ada_w': nrm(ks[2], (L, D, 9 * D), 0.5 * D ** -0.5),
        'ada_b': nrm(ks[3], (L, 9 * D), 0.02),
        'norm_g': 1.0 + nrm(ks[4], (L, 3, D), 0.05),
        'final_norm_g': 1.0 + nrm(ks[5], (D,), 0.05),
        'ffn_w_in': nrm(ks[6], (L, 2, D, 2 * D_FF), D ** -0.5),
        'ffn_w_out': nrm(ks[7], (L, 2, D_FF, D), D_FF ** -0.5),
        'mix_w_in': nrm(ks[8], (L, D, IN_COLS), D ** -0.5),
        'nsa_cmp_pe': nrm(ks[9], (L, 2, NSA_CMP_LEN, HEAD_DIM), 0.1),
        'nsa_cmp_w1': nrm(ks[10], (L, 2, NSA_CMP_LEN * HEAD_DIM, NSA_CMP_HIDDEN), (NSA_CMP_LEN * HEAD_DIM) ** -0.5),
        'nsa_cmp_w2': nrm(ks[11], (L, 2, NSA_CMP_HIDDEN, HEAD_DIM), NSA_CMP_HIDDEN ** -0.5),
        'fox_f_bias': jax.random.uniform(ks[12], (L, FOX_HEADS), jnp.float32, 1.0, 6.0),
        'br_w_nsa': nrm(ks[13], (L, NSA_Q_W, D), NSA_Q_W ** -0.5),
        'br_w_dil': nrm(ks[14], (L, DIL_OUT_W, D), DIL_OUT_W ** -0.5),
        'br_w_fox': nrm(ks[15], (L, FOX_W, D), FOX_W ** -0.5),
        'mix_w_out': nrm(ks[16], (L, D, D), D ** -0.5),
    }


def reference(x, c, ada_w, ada_b, norm_g, final_norm_g, ffn_w_in, ffn_w_out, mix_w_in,
              nsa_cmp_pe, nsa_cmp_w1, nsa_cmp_w2, fox_f_bias, br_w_nsa, br_w_dil, br_w_fox,
              mix_w_out):
    S = x.shape[1]
    cos, sin = _rope_tables(S)
    cond = jax.nn.silu(c)
    h = x
    for l in range(DEPTH):
        mod = cond @ ada_w[l] + ada_b[l]
        sh1, sc1, ga1, sh2, sc2, ga2, sh3, sc3, ga3 = [m[:, None, :] for m in jnp.split(mod, 9, axis=-1)]
        n = _modulate(_rms_norm(h, norm_g[l, 0]), sh1, sc1)
        h = h + 0.5 * ga1 * _swiglu(n, ffn_w_in[l, 0], ffn_w_out[l, 0])
        n = _modulate(_rms_norm(h, norm_g[l, 1]), sh2, sc2)
        h = h + ga2 * _token_mixing(n, mix_w_in[l], nsa_cmp_pe[l], nsa_cmp_w1[l], nsa_cmp_w2[l],
                                    fox_f_bias[l], br_w_nsa[l], br_w_dil[l], br_w_fox[l],
                                    mix_w_out[l], cos, sin)
        n = _modulate(_rms_norm(h, norm_g[l, 2]), sh3, sc3)
        h = h + 0.5 * ga3 * _swiglu(n, ffn_w_in[l, 1], ffn_w_out[l, 1])
    return _rms_norm(h, final_norm_g)
```

```python
import functools

import numpy as np
import jax
import jax.numpy as jnp
from jax import lax
from jax.experimental import pallas as pl
from jax.experimental.pallas import tpu as pltpu

F32 = jnp.float32
BF16 = jnp.bfloat16

HEAD_DIM = 64
ROPE_DIM = 16
ROPE_HALF = ROPE_DIM // 2
ROPE_THETA = 500000.0
Q_BLOCK = 128
NEG_INF = -1e30
RMS_EPS = 1e-6
QK_SCALE = HEAD_DIM ** -0.5

NSA_HEADS = 4
NSA_CMP_LEN = 32
NSA_CMP_STRIDE = 16
NSA_CMP_HIDDEN = 128
NSA_SEL_LEN = 64
NSA_SEL_TOPK = 16
NSA_WINDOW = 512
NSA_FORCE_SCORE = 1e4

DIL_PATTERNS = ((128, 1), (512, 4), (2048, 16))
DIL_HEADS_PER_GROUP = 2
DIL_HEADS = DIL_HEADS_PER_GROUP * len(DIL_PATTERNS)
FOX_HEADS = 6

NSA_Q_W = NSA_HEADS * HEAD_DIM
DIL_W = DIL_HEADS * HEAD_DIM
DIL_OUT_W = DIL_HEADS_PER_GROUP * HEAD_DIM
FOX_W = FOX_HEADS * HEAD_DIM

LANES = 128
MISC_FOX_LANE = 0
MISC_GATE_LANE = 8

PROJ_GROUPS = (
    ("nsa_q", NSA_Q_W, "A", QK_SCALE, BF16),
    ("cmp_kv", LANES, "B", 1.0, F32),
    ("sel_kv", LANES, "B", 1.0, BF16),
    ("win_kv", LANES, "B", 1.0, BF16),
    ("dil_q", DIL_W, "A", QK_SCALE, BF16),
    ("dil_k", DIL_W, "A", 1.0, BF16),
    ("dil_v", DIL_W, None, 1.0, BF16),
    ("fox_q", FOX_W, None, QK_SCALE, BF16),
    ("fox_k", FOX_W, None, 1.0, BF16),
    ("fox_v", FOX_W, None, 1.0, BF16),
    ("misc", LANES, None, 1.0, F32),
)
PROJ_COLS = sum(g[1] for g in PROJ_GROUPS)

VMEM_LIMIT = 56 * 1024 * 1024


def _dot(a, b):
    return jnp.dot(a, b, preferred_element_type=F32)


def _dot_nt(a, b):
    return lax.dot_general(a, b, (((1,), (1,)), ((), ())), preferred_element_type=F32)


def _norm_modulate(x, g, shift, scale):
    ms = jnp.mean(x * x, axis=-1, keepdims=True)
    y = x * lax.rsqrt(ms + RMS_EPS) * g
    return y * (1.0 + scale) + shift


def _masked_softmax(s, mask):
    s = jnp.where(mask, s, NEG_INF)
    m = jnp.max(s, axis=-1, keepdims=True)
    e = jnp.where(mask, jnp.exp(s - m), 0.0)
    den = jnp.sum(e, axis=-1, keepdims=True)
    den = jnp.where(den > 0, den, 1.0)
    return e / den, m + jnp.log(den)


def _ada_kernel(c_ref, w_ref, b_ref, o_ref):
    c = c_ref[...]
    cond = c * jax.nn.sigmoid(c)
    o_ref[...] = jnp.dot(cond, w_ref[...], preferred_element_type=F32,
                         precision=lax.Precision.HIGHEST) + b_ref[...]


def _ada_modulation(c, ada_w, ada_b):
    L, D, N = ada_w.shape
    B = c.shape[0]
    tn = 1152
    assert N % tn == 0
    return pl.pallas_call(
        _ada_kernel,
        grid=(L, N // tn),
        in_specs=[
            pl.BlockSpec((B, D), lambda l, j: (0, 0)),
            pl.BlockSpec((None, D, tn), lambda l, j: (l, 0, j)),
            pl.BlockSpec((None, 1, tn), lambda l, j: (l, 0, j)),
        ],
        out_specs=pl.BlockSpec((None, B, tn), lambda l, j: (l, 0, j)),
        out_shape=jax.ShapeDtypeStruct((L, B, N), F32),
        compiler_params=pltpu.CompilerParams(
            dimension_semantics=("arbitrary", "arbitrary"), vmem_limit_bytes=VMEM_LIMIT),
        name="ada_modulation",
    )(c, ada_w, ada_b.reshape(L, 1, N))


def _ffn_kernel(x_ref, mod_ref, g_ref, win_ref, wout_ref, fg_ref, o_ref, a_ref, *, mod_base, d_ff, chunk, final):
    x = x_ref[...]
    n = _norm_modulate(x, g_ref[...], mod_ref[mod_base:mod_base + 1, :],
                       mod_ref[mod_base + 1:mod_base + 2, :]).astype(BF16)
    for j in range(d_ff // chunk):
        gate = _dot(n, win_ref[:, j * chunk:(j + 1) * chunk])
        up = _dot(n, win_ref[:, d_ff + j * chunk:d_ff + (j + 1) * chunk])
        a_ref[:, j * chunk:(j + 1) * chunk] = (gate * jax.nn.sigmoid(gate) * up).astype(BF16)
    f = _dot(a_ref[...], wout_ref[...])
    out = x + (0.5 * mod_ref[mod_base + 2:mod_base + 3, :]) * f
    if final:
        ms = jnp.mean(out * out, axis=-1, keepdims=True)
        out = out * lax.rsqrt(ms + RMS_EPS) * fg_ref[...]
    o_ref[...] = out


def _ffn(h, mod, g, w_in, w_out, final_g, *, mod_base, final, seq, tm=512):
    T, D = h.shape
    d_ff = w_out.shape[0]
    chunk = 256
    assert T % tm == 0 and seq % tm == 0 and d_ff % chunk == 0
    tpb = seq // tm
    kern = functools.partial(_ffn_kernel, mod_base=mod_base, d_ff=d_ff, chunk=chunk, final=final)
    return pl.pallas_call(
        kern,
        grid=(T // tm,),
        in_specs=[
            pl.BlockSpec((tm, D), lambda i: (i, 0)),
            pl.BlockSpec((None, 9, D), lambda i: (i // tpb, 0, 0)),
            pl.BlockSpec((1, D), lambda i: (0, 0)),
            pl.BlockSpec((D, 2 * d_ff), lambda i: (0, 0), pipeline_mode=pl.Buffered(1)),
            pl.BlockSpec((d_ff, D), lambda i: (0, 0), pipeline_mode=pl.Buffered(1)),
            pl.BlockSpec((1, D), lambda i: (0, 0)),
        ],
        out_specs=pl.BlockSpec((tm, D), lambda i: (i, 0)),
        out_shape=jax.ShapeDtypeStruct((T, D), F32),
        scratch_shapes=[pltpu.VMEM((tm, d_ff), BF16)],
        compiler_params=pltpu.CompilerParams(
            dimension_semantics=("arbitrary",), vmem_limit_bytes=VMEM_LIMIT),
        name="ffn",
    )(h, mod, g, w_in, w_out, final_g)


def _rope_group(v, c, s1, s2):
    return v * c + pltpu.roll(v, LANES - ROPE_HALF, 1) * s1 + pltpu.roll(v, ROPE_HALF, 1) * s2


def _proj_kernel(x_ref, mod_ref, g_ref, w_ref, tc_ref, ts1_ref, ts2_ref, *out_refs):
    x = x_ref[...]
    n = _norm_modulate(x, g_ref[...], mod_ref[3:4, :], mod_ref[4:5, :]).astype(BF16)
    off = 0
    for (name, width, rope, scale, dtype), o_ref in zip(PROJ_GROUPS, out_refs):
        for sub in range(width // LANES):
            lo = off + sub * LANES
            v = _dot(n, w_ref[:, lo:lo + LANES])
            if rope is not None:
                t0 = 0 if rope == "A" else LANES
                v = _rope_group(v, tc_ref[:, t0:t0 + LANES], ts1_ref[:, t0:t0 + LANES],
                                ts2_ref[:, t0:t0 + LANES])
            if scale != 1.0:
                v = v * scale
            o_ref[:, sub * LANES:(sub + 1) * LANES] = v.astype(dtype)
        off += width


def _mixer_proj(h, mod, g, w, tabs, *, seq, tm=512):
    T, D = h.shape
    assert T % tm == 0 and seq % tm == 0
    tpb = seq // tm
    tab_spec = pl.BlockSpec((tm, 2 * LANES), lambda i: (i % tpb, 0))
    return pl.pallas_call(
        _proj_kernel,
        grid=(T // tm,),
        in_specs=[
            pl.BlockSpec((tm, D), lambda i: (i, 0)),
            pl.BlockSpec((None, 9, D), lambda i: (i // tpb, 0, 0)),
            pl.BlockSpec((1, D), lambda i: (0, 0)),
            pl.BlockSpec((D, PROJ_COLS), lambda i: (0, 0), pipeline_mode=pl.Buffered(1)),
            tab_spec, tab_spec, tab_spec,
        ],
        out_specs=[pl.BlockSpec((tm, gw), lambda i: (i, 0)) for (_, gw, _, _, _) in PROJ_GROUPS],
        out_shape=[jax.ShapeDtypeStruct((T, gw), dt) for (_, gw, _, _, dt) in PROJ_GROUPS],
        compiler_params=pltpu.CompilerParams(
            dimension_semantics=("arbitrary",), vmem_limit_bytes=VMEM_LIMIT),
        name="mixer_proj",
    )(h, mod, g, w, *tabs)


def _compress_kernel(x_ref, pet_ref, peb_ref, w1t_ref, w1b_ref, w2_ref, o_ref):
    x = x_ref[...]
    a = _dot((x + pet_ref[...]).astype(BF16), w1t_ref[...])
    b = _dot((x + peb_ref[...]).astype(BF16), w1b_ref[...])
    nrow = x.shape[0]
    hid = a + pltpu.roll(b, nrow - 1, 0)
    hid = hid * jax.nn.sigmoid(hid)
    o_ref[...] = _dot(hid.astype(BF16), w2_ref[...]).astype(o_ref.dtype)


def _compress(cmp_kv, pe_top, pe_bot, w1_top, w1_bot, w2):
    B, S, _ = cmp_kv.shape
    nchunk = S // NSA_CMP_STRIDE
    x = cmp_kv.reshape(B, nchunk, NSA_CMP_STRIDE * LANES)
    kw = NSA_CMP_STRIDE * LANES
    hw = 2 * NSA_CMP_HIDDEN
    return pl.pallas_call(
        _compress_kernel,
        grid=(B,),
        in_specs=[
            pl.BlockSpec((None, nchunk, kw), lambda b: (b, 0, 0)),
            pl.BlockSpec((1, kw), lambda b: (0, 0)),
            pl.BlockSpec((1, kw), lambda b: (0, 0)),
            pl.BlockSpec((kw, hw), lambda b: (0, 0)),
            pl.BlockSpec((kw, hw), lambda b: (0, 0)),
            pl.BlockSpec((hw, LANES), lambda b: (0, 0)),
        ],
        out_specs=pl.BlockSpec((None, nchunk, LANES), lambda b: (b, 0, 0)),
        out_shape=jax.ShapeDtypeStruct((B, nchunk, LANES), BF16),
        compiler_params=pltpu.CompilerParams(
            dimension_semantics=("arbitrary",), vmem_limit_bytes=VMEM_LIMIT),
        name="nsa_compress",
    )(x, pe_top, pe_bot, w1_top, w1_bot, w2)


def _nsa_kernel(q_ref, cmp_ref, sel_ref, win_ref, misc_ref, ovl_ref, o_ref, *, seq, tk):
    H, Q = NSA_HEADS, Q_BLOCK
    q0 = pl.program_id(1) * Q
    q = q_ref[...]
    q4 = jnp.concatenate([q[:, HEAD_DIM * h:HEAD_DIM * (h + 1)] for h in range(H)], axis=0)
    ncmp = cmp_ref.shape[0]

    kc = cmp_ref[:, 0:HEAD_DIM]
    vc = cmp_ref[:, HEAD_DIM:2 * HEAD_DIM]
    s = _dot_nt(q4, kc).reshape(H, Q, ncmp)
    tt_c = q0 + lax.broadcasted_iota(jnp.int32, (Q, ncmp), 0)
    nn = lax.broadcasted_iota(jnp.int32, (Q, ncmp), 1)
    cmask = (nn * NSA_CMP_STRIDE + (NSA_CMP_LEN - 1) <= tt_c) & (nn < ncmp - 1)
    p_cmp, _ = _masked_softmax(s, cmask[None])
    o_cmp = _dot(p_cmp.reshape(H * Q, ncmp).astype(BF16), vc)

    psum = p_cmp[0] + p_cmp[1] + p_cmp[2] + p_cmp[3]
    hi = psum.astype(BF16)
    lo = (psum - hi.astype(F32)).astype(BF16)
    ovl = ovl_ref[...]
    imp = _dot(hi, ovl) + _dot(lo, ovl)
    n_sel = seq // NSA_SEL_LEN
    jj = lax.broadcasted_iota(jnp.int32, (Q, LANES), 1)
    tt = q0 + lax.broadcasted_iota(jnp.int32, (Q, LANES), 0)
    valid = jj * NSA_SEL_LEN <= tt
    forced = (jj == (tt >> 6)) | (jj == 0)
    imp = jnp.where(forced, NSA_FORCE_SCORE, jnp.where(valid, imp, -1.0))
    cnt = jnp.zeros((Q, LANES), F32)
    for i in range(n_sel):
        ci = jnp.broadcast_to(imp[:, i:i + 1], (Q, LANES))
        tie = jnp.where(jj > i, 1.0, 0.0)
        cnt = cnt + jnp.where(ci > imp, 1.0, jnp.where(ci == imp, tie, 0.0))
    top_k = min(NSA_SEL_TOPK, n_sel)
    selected = jnp.where((cnt < top_k) & (jj < n_sel), 1.0, 0.0).astype(BF16)

    n_tiles = (q0 + Q + tk - 1) // tk
    bpt = tk // NSA_SEL_LEN

    def sel_body(k, carry):
        m, l, acc = carry
        base = pl.multiple_of(k * tk, tk)
        kv = sel_ref[pl.ds(base, tk), :]
        ks = kv[:, 0:HEAD_DIM]
        vs = kv[:, HEAD_DIM:2 * HEAD_DIM]
        s = _dot_nt(q4, ks).reshape(H, Q, tk)
        blk_row = lax.broadcasted_iota(jnp.int32, (LANES, tk), 0)
        blk_col = k * bpt + (lax.broadcasted_iota(jnp.int32, (LANES, tk), 1) >> 6)
        expand = jnp.where(blk_row == blk_col, 1.0, 0.0).astype(BF16)
        member = _dot(selected, expand)
        kpos = base + lax.broadcasted_iota(jnp.int32, (Q, tk), 1)
        tq = q0 + lax.broadcasted_iota(jnp.int32, (Q, tk), 0)
        mask = ((member > 0.5) & (kpos <= tq))[None]
        s = jnp.where(mask, s, NEG_INF)
        m_new = jnp.maximum(m, jnp.max(s, axis=-1, keepdims=True))
        e = jnp.where(mask, jnp.exp(s - m_new), 0.0)
        alpha = jnp.exp(m - m_new)
        l_new = alpha * l + jnp.sum(e, axis=-1, keepdims=True)
        pv = _dot(e.reshape(H * Q, tk).astype(BF16), vs).reshape(H, Q, HEAD_DIM)
        return m_new, l_new, alpha * acc + pv

    m0 = jnp.full((H, Q, 1), NEG_INF, F32)
    l0 = jnp.zeros((H, Q, 1), F32)
    a0 = jnp.zeros((H, Q, HEAD_DIM), F32)
    _, l_sel, acc_sel = lax.fori_loop(0, n_tiles, sel_body, (m0, l0, a0))
    o_sel = (acc_sel / jnp.where(l_sel > 0, l_sel, 1.0)).reshape(H * Q, HEAD_DIM)

    wlen = NSA_WINDOW + Q
    start = pl.multiple_of(jnp.maximum(q0 - NSA_WINDOW, 0), Q)
    kvw = win_ref[pl.ds(start, wlen), :]
    s = _dot_nt(q4, kvw[:, 0:HEAD_DIM]).reshape(H, Q, wlen)
    kpos = start + lax.broadcasted_iota(jnp.int32, (Q, wlen), 1)
    tw = q0 + lax.broadcasted_iota(jnp.int32, (Q, wlen), 0)
    wmask = (kpos <= tw) & (kpos > tw - NSA_WINDOW)
    p_win, _ = _masked_softmax(s, wmask[None])
    o_win = _dot(p_win.reshape(H * Q, wlen).astype(BF16), kvw[:, HEAD_DIM:2 * HEAD_DIM])

    g = jax.nn.sigmoid(misc_ref[...])
    outs = []
    for h in range(H):
        c0 = MISC_GATE_LANE + 3 * h
        rows = slice(h * Q, (h + 1) * Q)
        outs.append(g[:, c0:c0 + 1] * o_cmp[rows] + g[:, c0 + 1:c0 + 2] * o_sel[rows]
                    + g[:, c0 + 2:c0 + 3] * o_win[rows])
    o_ref[...] = jnp.concatenate(outs, axis=1).astype(o_ref.dtype)


def _nsa_attention(q, cmp_out, sel_kv, win_kv, misc, ovl, *, tk=512):
    B, S, _ = q.shape
    ncmp = cmp_out.shape[1]
    assert S % tk == 0 and S >= NSA_WINDOW + Q_BLOCK and S // NSA_SEL_LEN <= LANES
    kern = functools.partial(_nsa_kernel, seq=S, tk=tk)
    return pl.pallas_call(
        kern,
        grid=(B, S // Q_BLOCK),
        in_specs=[
            pl.BlockSpec((None, Q_BLOCK, NSA_Q_W), lambda b, i: (b, i, 0)),
            pl.BlockSpec((None, ncmp, LANES), lambda b, i: (b, 0, 0)),
            pl.BlockSpec((None, S, LANES), lambda b, i: (b, 0, 0)),
            pl.BlockSpec((None, S, LANES), lambda b, i: (b, 0, 0)),
            pl.BlockSpec((None, Q_BLOCK, LANES), lambda b, i: (b, i, 0)),
            pl.BlockSpec((ncmp, LANES), lambda b, i: (0, 0)),
        ],
        out_specs=pl.BlockSpec((None, Q_BLOCK, NSA_Q_W), lambda b, i: (b, i, 0)),
        out_shape=jax.ShapeDtypeStruct((B, S, NSA_Q_W), BF16),
        compiler_params=pltpu.CompilerParams(
            dimension_semantics=("arbitrary", "arbitrary"), vmem_limit_bytes=VMEM_LIMIT),
        name="nsa_attention",
    )(q, cmp_out, sel_kv, win_kv, misc, ovl)


def _dil_kernel(q_ref, k_ref, v_ref, o_ref):
    Q = Q_BLOCK
    q0 = pl.program_id(1) * Q
    q = q_ref[...]
    outs = [[], []]
    lses = [[], []]
    for g, (window, dil) in enumerate(DIL_PATTERNS):
        span = window + Q
        start = pl.multiple_of(jnp.maximum(q0 - window, 0), Q)
        kg = k_ref[pl.ds(start, span), g * LANES:(g + 1) * LANES]
        vg = v_ref[pl.ds(start, span), g * LANES:(g + 1) * LANES]
        dist = (q0 - start) + lax.broadcasted_iota(jnp.int32, (Q, span), 0) \
            - lax.broadcasted_iota(jnp.int32, (Q, span), 1)
        valid = (dist >= 0) & (dist <= window) & ((dist & (dil - 1)) == 0)
        for hh in range(DIL_HEADS_PER_GROUP):
            c0 = g * LANES + hh * HEAD_DIM
            s = _dot_nt(q[:, c0:c0 + HEAD_DIM], kg[:, hh * HEAD_DIM:(hh + 1) * HEAD_DIM])
            p, lse = _masked_softmax(s, valid)
            outs[hh].append(_dot(p.astype(BF16), vg[:, hh * HEAD_DIM:(hh + 1) * HEAD_DIM]))
            lses[hh].append(lse)
    res = []
    for hh in range(DIL_HEADS_PER_GROUP):
        mx = jnp.maximum(jnp.maximum(lses[hh][0], lses[hh][1]), lses[hh][2])
        w = [jnp.exp(l - mx) for l in lses[hh]]
        wsum = w[0] + w[1] + w[2]
        res.append((w[0] / wsum) * outs[hh][0] + (w[1] / wsum) * outs[hh][1] + (w[2] / wsum) * outs[hh][2])
    o_ref[...] = jnp.concatenate(res, axis=1).astype(o_ref.dtype)


def _dilated_attention(q, k, v):
    B, S, _ = q.shape
    assert all(d & (d - 1) == 0 for _, d in DIL_PATTERNS)
    assert S >= max(w for w, _ in DIL_PATTERNS) + Q_BLOCK
    return pl.pallas_call(
        _dil_kernel,
        grid=(B, S // Q_BLOCK),
        in_specs=[
            pl.BlockSpec((None, Q_BLOCK, DIL_W), lambda b, i: (b, i, 0)),
            pl.BlockSpec((None, S, DIL_W), lambda b, i: (b, 0, 0)),
            pl.BlockSpec((None, S, DIL_W), lambda b, i: (b, 0, 0)),
        ],
        out_specs=pl.BlockSpec((None, Q_BLOCK, DIL_OUT_W), lambda b, i: (b, i, 0)),
        out_shape=jax.ShapeDtypeStruct((B, S, DIL_OUT_W), BF16),
        compiler_params=pltpu.CompilerParams(
            dimension_semantics=("arbitrary", "arbitrary"), vmem_limit_bytes=VMEM_LIMIT),
        name="dilated_attention",
    )(q, k, v)


def _fox_cum_kernel(misc_ref, bias_ref, cum_ref, cumt_ref):
    nblk = misc_ref.shape[0] // LANES
    r = lax.broadcasted_iota(jnp.int32, (LANES, LANES), 0)
    c = lax.broadcasted_iota(jnp.int32, (LANES, LANES), 1)
    tri = jnp.where(r >= c, 1.0, 0.0).astype(F32)
    carry = jnp.zeros((1, LANES), F32)
    for blk in range(nblk):
        x = misc_ref[blk * LANES:(blk + 1) * LANES, :] + bias_ref[...]
        log_f = -(jnp.maximum(-x, 0.0) + jnp.log1p(jnp.exp(-jnp.abs(x))))
        cs = jnp.dot(tri, log_f, preferred_element_type=F32, precision=lax.Precision.HIGHEST) + carry
        cum_ref[blk * LANES:(blk + 1) * LANES, :] = cs
        cumt_ref[blk] = cs.T[0:8, :]
        carry = cs[LANES - 1:LANES, :]


def _fox_cumsum(misc, bias_row):
    B, S, _ = misc.shape
    nblk = S // LANES
    return pl.pallas_call(
        _fox_cum_kernel,
        grid=(B,),
        in_specs=[
            pl.BlockSpec((None, S, LANES), lambda b: (b, 0, 0)),
            pl.BlockSpec((1, LANES), lambda b: (0, 0)),
        ],
        out_specs=[
            pl.BlockSpec((None, S, LANES), lambda b: (b, 0, 0)),
            pl.BlockSpec((None, nblk, 8, LANES), lambda b: (b, 0, 0, 0)),
        ],
        out_shape=[
            jax.ShapeDtypeStruct((B, S, LANES), F32),
            jax.ShapeDtypeStruct((B, nblk, 8, LANES), F32),
        ],
        compiler_params=pltpu.CompilerParams(
            dimension_semantics=("arbitrary",), vmem_limit_bytes=VMEM_LIMIT),
        name="fox_cumsum",
    )(misc, bias_row)


def _fox_kernel(q_ref, k_ref, v_ref, cum_ref, cumt_ref, o_ref, *, tq, tk):
    q0 = pl.program_id(1) * tq
    n_tiles = (q0 + tq + tk - 1) // tk
    sub = tk // LANES
    cum_q = cum_ref[...]
    outs = []
    for h in range(FOX_HEADS):
        qh = q_ref[:, h * HEAD_DIM:(h + 1) * HEAD_DIM]
        cq = cum_q[:, h:h + 1]

        def body(k, carry, h=h, qh=qh, cq=cq):
            m, l, acc = carry
            base = pl.multiple_of(k * tk, tk)
            kh = k_ref[pl.ds(base, tk), h * HEAD_DIM:(h + 1) * HEAD_DIM]
            vh = v_ref[pl.ds(base, tk), h * HEAD_DIM:(h + 1) * HEAD_DIM]
            ck = jnp.concatenate([cumt_ref[k * sub + u, h:h + 1, :] for u in range(sub)], axis=1)
            s = _dot_nt(qh, kh) + (cq - ck)
            kpos = base + lax.broadcasted_iota(jnp.int32, (tq, tk), 1)
            tt = q0 + lax.broadcasted_iota(jnp.int32, (tq, tk), 0)
            mask = kpos <= tt
            s = jnp.where(mask, s, NEG_INF)
            m_new = jnp.maximum(m, jnp.max(s, axis=-1, keepdims=True))
            e = jnp.where(mask, jnp.exp(s - m_new), 0.0)
            alpha = jnp.exp(m - m_new)
            l_new = alpha * l + jnp.sum(e, axis=-1, keepdims=True)
            return m_new, l_new, alpha * acc + _dot(e.astype(BF16), vh)

        m0 = jnp.full((tq, 1), NEG_INF, F32)
        l0 = jnp.zeros((tq, 1), F32)
        a0 = jnp.zeros((tq, HEAD_DIM), F32)
        _, l, acc = lax.fori_loop(0, n_tiles, body, (m0, l0, a0))
        outs.append(acc / jnp.where(l > 0, l, 1.0))
    o_ref[...] = jnp.concatenate(outs, axis=1).astype(o_ref.dtype)


def _fox_attention(q, k, v, cum, cumt, *, tq=256, tk=512):
    B, S, _ = q.shape
    assert S % tq == 0 and S % tk == 0 and tk % LANES == 0
    nblk = S // LANES
    kern = functools.partial(_fox_kernel, tq=tq, tk=tk)
    return pl.pallas_call(
        kern,
        grid=(B, S // tq),
        in_specs=[
            pl.BlockSpec((None, tq, FOX_W), lambda b, i: (b, i, 0)),
            pl.BlockSpec((None, S, FOX_W), lambda b, i: (b, 0, 0)),
            pl.BlockSpec((None, S, FOX_W), lambda b, i: (b, 0, 0)),
            pl.BlockSpec((None, tq, LANES), lambda b, i: (b, i, 0)),
            pl.BlockSpec((None, nblk, 8, LANES), lambda b, i: (b, 0, 0, 0)),
        ],
        out_specs=pl.BlockSpec((None, tq, FOX_W), lambda b, i: (b, i, 0)),
        out_shape=jax.ShapeDtypeStruct((B, S, FOX_W), BF16),
        compiler_params=pltpu.CompilerParams(
            dimension_semantics=("arbitrary", "arbitrary"), vmem_limit_bytes=VMEM_LIMIT),
        name="fox_attention",
    )(q, k, v, cum, cumt)


def _merge_kernel(x_ref, mod_ref, g_ref, ya_ref, yb_ref, yc_ref, wg_ref, bra_ref, brb_ref, brc_ref, wo_ref, o_ref):
    x = x_ref[...]
    D = x.shape[1]
    n = _norm_modulate(x, g_ref[...], mod_ref[3:4, :], mod_ref[4:5, :]).astype(BF16)
    merged = jax.nn.sigmoid(_dot(n, wg_ref[:, 0:D])) * _dot(ya_ref[...], bra_ref[...])
    merged = merged + jax.nn.sigmoid(_dot(n, wg_ref[:, D:2 * D])) * _dot(yb_ref[...], brb_ref[...])
    merged = merged + jax.nn.sigmoid(_dot(n, wg_ref[:, 2 * D:3 * D])) * _dot(yc_ref[...], brc_ref[...])
    o_ref[...] = x + mod_ref[5:6, :] * _dot(merged.astype(BF16), wo_ref[...])


def _merge(h, mod, g, ya, yb, yc, w_gate, br_a, br_b, br_c, w_out, *, seq, tm=512):
    T, D = h.shape
    assert T % tm == 0 and seq % tm == 0
    tpb = seq // tm

    def resident(shape):
        return pl.BlockSpec(shape, lambda i: (0, 0), pipeline_mode=pl.Buffered(1))

    return pl.pallas_call(
        _merge_kernel,
        grid=(T // tm,),
        in_specs=[
            pl.BlockSpec((tm, D), lambda i: (i, 0)),
            pl.BlockSpec((None, 9, D), lambda i: (i // tpb, 0, 0)),
            pl.BlockSpec((1, D), lambda i: (0, 0)),
            pl.BlockSpec((tm, NSA_Q_W), lambda i: (i, 0)),
            pl.BlockSpec((tm, DIL_OUT_W), lambda i: (i, 0)),
            pl.BlockSpec((tm, FOX_W), lambda i: (i, 0)),
            resident((D, 3 * D)),
            resident((NSA_Q_W, D)),
            resident((DIL_OUT_W, D)),
            resident((FOX_W, D)),
            resident((D, D)),
        ],
        out_specs=pl.BlockSpec((tm, D), lambda i: (i, 0)),
        out_shape=jax.ShapeDtypeStruct((T, D), F32),
        compiler_params=pltpu.CompilerParams(
            dimension_semantics=("arbitrary",), vmem_limit_bytes=VMEM_LIMIT),
        name="merge_out",
    )(h, mod, g, ya, yb, yc, w_gate, br_a, br_b, br_c, w_out)


def _rope_tables(seq):
    inv_freq = ROPE_THETA ** (-jnp.arange(0, ROPE_DIM, 2, dtype=F32) / ROPE_DIM)
    ang = jnp.arange(seq, dtype=F32)[:, None] * inv_freq[None, :]
    cos, sin = jnp.cos(ang), jnp.sin(ang)
    d = np.arange(LANES) % HEAD_DIM
    idx = d % ROPE_HALF
    first = jnp.asarray(d < ROPE_HALF)
    second = jnp.asarray((d >= ROPE_HALF) & (d < ROPE_DIM))
    c_a = jnp.where(first | second, cos[:, idx], 1.0)
    s1_a = jnp.where(first, -sin[:, idx], 0.0)
    s2_a = jnp.where(second, sin[:, idx], 0.0)
    head0 = jnp.asarray(np.arange(LANES) < HEAD_DIM)
    c_b = jnp.where(head0, c_a, 1.0)
    s1_b = jnp.where(head0, s1_a, 0.0)
    s2_b = jnp.where(head0, s2_a, 0.0)
    return (jnp.concatenate([c_a, c_b], axis=1), jnp.concatenate([s1_a, s1_b], axis=1),
            jnp.concatenate([s2_a, s2_b], axis=1))


def _pack_mix_w_in(w):
    D = w.shape[0]
    o = 0
    nsa_q = w[:, o:o + NSA_Q_W]; o += NSA_Q_W
    kv = [w[:, o + j * HEAD_DIM:o + (j + 1) * HEAD_DIM] for j in range(6)]; o += 6 * HEAD_DIM
    nsa_g = w[:, o:o + 3 * NSA_HEADS]; o += 3 * NSA_HEADS
    dil = w[:, o:o + 3 * DIL_W]; o += 3 * DIL_W
    fox = w[:, o:o + 3 * FOX_W]; o += 3 * FOX_W
    fox_f = w[:, o:o + FOX_HEADS]; o += FOX_HEADS
    merge_g = w[:, o:]
    misc = jnp.zeros((D, LANES), w.dtype)
    misc = misc.at[:, MISC_FOX_LANE:MISC_FOX_LANE + FOX_HEADS].set(fox_f)
    misc = misc.at[:, MISC_GATE_LANE:MISC_GATE_LANE + 3 * NSA_HEADS].set(nsa_g)
    packed = jnp.concatenate([nsa_q, kv[0], kv[1], kv[2], kv[3], kv[4], kv[5], dil, fox, misc], axis=1)
    assert packed.shape[1] == PROJ_COLS
    return packed.astype(BF16), merge_g.astype(BF16)


def _pack_compress(pe, w1, w2):
    half = NSA_CMP_STRIDE
    hid = NSA_CMP_HIDDEN
    w1k = w1[0].reshape(NSA_CMP_LEN, HEAD_DIM, hid)
    w1v = w1[1].reshape(NSA_CMP_LEN, HEAD_DIM, hid)
    z = jnp.zeros((half, HEAD_DIM, hid), w1.dtype)

    def halfpack(lo):
        kk = jnp.concatenate([w1k[lo:lo + half], z], axis=-1)
        vv = jnp.concatenate([z, w1v[lo:lo + half]], axis=-1)
        return jnp.concatenate([kk, vv], axis=1).reshape(half * LANES, 2 * hid).astype(BF16)

    def pepack(lo):
        return jnp.concatenate([pe[0, lo:lo + half], pe[1, lo:lo + half]], axis=-1).reshape(1, half * LANES)

    zz = jnp.zeros((hid, HEAD_DIM), w2.dtype)
    w2p = jnp.concatenate([jnp.concatenate([w2[0], zz], axis=1),
                           jnp.concatenate([zz, w2[1]], axis=1)], axis=0).astype(BF16)
    return pepack(0), pepack(half), halfpack(0), halfpack(half), w2p


def _overlap_matrix(seq):
    n_chunk = seq // NSA_CMP_STRIDE
    n_cmp = (seq - NSA_CMP_LEN) // NSA_CMP_STRIDE + 1
    n_sel = seq // NSA_SEL_LEN
    cmp_start = np.arange(n_cmp) * NSA_CMP_STRIDE
    sel_start = np.arange(n_sel) * NSA_SEL_LEN
    ov = np.minimum(cmp_start[:, None] + NSA_CMP_LEN, sel_start[None, :] + NSA_SEL_LEN) \
        - np.maximum(cmp_start[:, None], sel_start[None, :])
    full = np.zeros((n_chunk, LANES), np.float32)
    full[:n_cmp, :n_sel] = np.clip(ov, 0, None) / NSA_CMP_LEN
    return jnp.asarray(full, dtype=BF16)


def kernel(x, c, ada_w, ada_b, norm_g, final_norm_g, ffn_w_in, ffn_w_out, mix_w_in, nsa_cmp_pe, nsa_cmp_w1,
           nsa_cmp_w2, fox_f_bias, br_w_nsa, br_w_dil, br_w_fox, mix_w_out):
    B, S, D = x.shape
    L = ada_w.shape[0]
    T = B * S
    mod_all = _ada_modulation(c, ada_w, ada_b).reshape(L, B, 9, D)
    tabs = _rope_tables(S)
    ovl = _overlap_matrix(S)
    fg = final_norm_g.reshape(1, D)
    h = x.reshape(T, D)
    for l in range(L):
        mod = mod_all[l]
        h = _ffn(h, mod, norm_g[l, 0].reshape(1, D), ffn_w_in[l, 0].astype(BF16), ffn_w_out[l, 0].astype(BF16),
                 fg, mod_base=0, final=False, seq=S)

        w_proj, w_gate = _pack_mix_w_in(mix_w_in[l])
        g1 = norm_g[l, 1].reshape(1, D)
        proj = _mixer_proj(h, mod, g1, w_proj, tabs, seq=S)
        (nsa_q, cmp_kv, sel_kv, win_kv, dil_q, dil_k, dil_v, fox_q, fox_k, fox_v, misc) = [
            p.reshape(B, S, p.shape[-1]) for p in proj]

        cmp_out = _compress(cmp_kv, *_pack_compress(nsa_cmp_pe[l], nsa_cmp_w1[l], nsa_cmp_w2[l]))
        y_a = _nsa_attention(nsa_q, cmp_out, sel_kv, win_kv, misc, ovl)
        y_b = _dilated_attention(dil_q, dil_k, dil_v)
        bias_row = jnp.zeros((1, LANES), F32).at[0, MISC_FOX_LANE:MISC_FOX_LANE + FOX_HEADS].set(fox_f_bias[l])
        cum, cumt = _fox_cumsum(misc, bias_row)
        y_c = _fox_attention(fox_q, fox_k, fox_v, cum, cumt)

        h = _merge(h, mod, g1, y_a.reshape(T, NSA_Q_W), y_b.reshape(T, DIL_OUT_W), y_c.reshape(T, FOX_W),
                   w_gate, br_w_nsa[l].astype(BF16), br_w_dil[l].astype(BF16), br_w_fox[l].astype(BF16),
                   mix_w_out[l].astype(BF16), seq=S)

        h = _ffn(h, mod, norm_g[l, 2].reshape(1, D), ffn_w_in[l, 1].astype(BF16), ffn_w_out[l, 1].astype(BF16),
                 fg, mod_base=6, final=(l == L - 1), seq=S)
    return h.reshape(B, S, D)
```

```python
import functools
import math

import numpy as np
import jax
import jax.numpy as jnp
from jax import lax
from jax.experimental import pallas as pl
from jax.experimental.pallas import tpu as pltpu

F32 = jnp.float32
BF16 = jnp.bfloat16

HEAD_DIM = 64
ROPE_DIM = 16
ROPE_HALF = ROPE_DIM // 2
ROPE_THETA = 500000.0
Q_BLOCK = 128
NEG_INF = -1e30
RMS_EPS = 1e-6
QK_SCALE = HEAD_DIM ** -0.5
LOG2E = math.log2(math.e)

NSA_HEADS = 4
NSA_CMP_LEN = 32
NSA_CMP_STRIDE = 16
NSA_CMP_HIDDEN = 128
NSA_SEL_LEN = 64
NSA_SEL_TOPK = 16
NSA_WINDOW = 512
NSA_FORCE_SCORE = 1e4

DIL_PATTERNS = ((128, 1), (512, 4), (2048, 16))
DIL_HEADS_PER_GROUP = 2
DIL_HEADS = DIL_HEADS_PER_GROUP * len(DIL_PATTERNS)
FOX_HEADS = 6

NSA_Q_W = NSA_HEADS * HEAD_DIM
DIL_W = DIL_HEADS * HEAD_DIM
DIL_OUT_W = DIL_HEADS_PER_GROUP * HEAD_DIM
FOX_W = FOX_HEADS * HEAD_DIM

LANES = 128
SUBLANES = 8
MISC_FOX_LANE = 0
MISC_GATE_LANE = 8
PROJ_TM = 512
FOX_AUG = 3

PROJ_GROUPS = (
    ("nsa_q", NSA_Q_W, "A", QK_SCALE * LOG2E),
    ("cmp_kv", LANES, "B", 1.0),
    ("sel_kv", LANES, "B", 1.0),
    ("win_kv", LANES, "B", 1.0),
    ("dil_q", DIL_W, "A", QK_SCALE),
    ("dil_k", DIL_W, "A", 1.0),
    ("dil_v", DIL_W, None, 1.0),
    ("fox_q", FOX_W, None, QK_SCALE * LOG2E),
    ("fox_k", FOX_W, None, 1.0),
    ("fox_v", FOX_W, None, 1.0),
    ("misc", LANES, None, 1.0),
)
PROJ_COLS = sum(g[1] for g in PROJ_GROUPS)

VMEM_LIMIT = 56 * 1024 * 1024


def _dot(a, b):
    return jnp.dot(a, b, preferred_element_type=F32)


def _dot_nt(a, b):
    return lax.dot_general(a, b, (((1,), (1,)), ((), ())), preferred_element_type=F32)


def _norm_modulate(x, g, shift, scale):
    ms = jnp.mean(x * x, axis=-1, keepdims=True)
    y = x * lax.rsqrt(ms + RMS_EPS) * g
    return y * (1.0 + scale) + shift


def _masked_softmax(s, mask):
    s = jnp.where(mask, s, NEG_INF)
    m = jnp.max(s, axis=-1, keepdims=True)
    e = jnp.where(mask, jnp.exp(s - m), 0.0)
    den = jnp.sum(e, axis=-1, keepdims=True)
    den = jnp.where(den > 0, den, 1.0)
    return e / den, m + jnp.log(den)


def _params(*sem):
    return pltpu.CompilerParams(dimension_semantics=sem, vmem_limit_bytes=VMEM_LIMIT)


def _ada_kernel(c_ref, w_ref, b_ref, o_ref):
    c = c_ref[...]
    cond = c * jax.nn.sigmoid(c)
    o_ref[...] = jnp.dot(cond, w_ref[...], preferred_element_type=F32,
                         precision=lax.Precision.HIGHEST) + b_ref[...]


def _ada_modulation(c, ada_w, ada_b):
    L, D, N = ada_w.shape
    B = c.shape[0]
    tn = 1152
    assert N % tn == 0
    return pl.pallas_call(
        _ada_kernel,
        grid=(L, N // tn),
        in_specs=[
            pl.BlockSpec((B, D), lambda l, j: (0, 0)),
            pl.BlockSpec((None, D, tn), lambda l, j: (l, 0, j)),
            pl.BlockSpec((None, 1, tn), lambda l, j: (l, 0, j)),
        ],
        out_specs=pl.BlockSpec((None, B, tn), lambda l, j: (l, 0, j)),
        out_shape=jax.ShapeDtypeStruct((L, B, N), F32),
        compiler_params=_params("arbitrary", "arbitrary"),
        name="ada_modulation",
    )(c, ada_w, ada_b.reshape(L, 1, N))


def _ffn_kernel(x_ref, mod_ref, g_ref, win_ref, wout_ref, fg_ref, o_ref, a_ref, *, mod_base, d_ff, chunk, final):
    x = x_ref[...]
    n = _norm_modulate(x, g_ref[...], mod_ref[mod_base:mod_base + 1, :],
                       mod_ref[mod_base + 1:mod_base + 2, :]).astype(BF16)
    for j in range(d_ff // chunk):
        gate = _dot(n, win_ref[:, j * chunk:(j + 1) * chunk])
        up = _dot(n, win_ref[:, d_ff + j * chunk:d_ff + (j + 1) * chunk])
        a_ref[:, j * chunk:(j + 1) * chunk] = (gate * jax.nn.sigmoid(gate) * up).astype(BF16)
    f = _dot(a_ref[...], wout_ref[...])
    out = x + (0.5 * mod_ref[mod_base + 2:mod_base + 3, :]) * f
    if final:
        ms = jnp.mean(out * out, axis=-1, keepdims=True)
        out = out * lax.rsqrt(ms + RMS_EPS) * fg_ref[...]
    o_ref[...] = out


def _ffn(h, mod, g, w_in, w_out, final_g, *, mod_base, final, seq, tm=PROJ_TM):
    T, D = h.shape
    d_ff = w_out.shape[0]
    chunk = 256
    assert T % tm == 0 and seq % tm == 0 and d_ff % chunk == 0
    tpb = seq // tm
    kern = functools.partial(_ffn_kernel, mod_base=mod_base, d_ff=d_ff, chunk=chunk, final=final)
    return pl.pallas_call(
        kern,
        grid=(T // tm,),
        in_specs=[
            pl.BlockSpec((tm, D), lambda i: (i, 0)),
            pl.BlockSpec((None, 9, D), lambda i: (i // tpb, 0, 0)),
            pl.BlockSpec((1, D), lambda i: (0, 0)),
            pl.BlockSpec((D, 2 * d_ff), lambda i: (0, 0), pipeline_mode=pl.Buffered(1)),
            pl.BlockSpec((d_ff, D), lambda i: (0, 0), pipeline_mode=pl.Buffered(1)),
            pl.BlockSpec((1, D), lambda i: (0, 0)),
        ],
        out_specs=pl.BlockSpec((tm, D), lambda i: (i, 0)),
        out_shape=jax.ShapeDtypeStruct((T, D), F32),
        scratch_shapes=[pltpu.VMEM((tm, d_ff), BF16)],
        compiler_params=_params("arbitrary"),
        name="ffn",
    )(h, mod, g, w_in, w_out, final_g)


def _rope_group(v, c, s1, s2):
    return v * c + pltpu.roll(v, LANES - ROPE_HALF, 1) * s1 + pltpu.roll(v, ROPE_HALF, 1) * s2


def _proj_kernel(x_ref, mod_ref, g_ref, w_ref, tc_ref, ts1_ref, ts2_ref,
                 nsa_qt_ref, cmp_ref, sel_ref, selvt_ref, win_ref, winvt_ref,
                 dq_ref, dk_ref, dv_ref, fq_ref, fk_ref, fvt_ref, misc_ref):
    x = x_ref[...]
    tm = x.shape[0]
    nblk = tm // Q_BLOCK
    n = _norm_modulate(x, g_ref[...], mod_ref[3:4, :], mod_ref[4:5, :]).astype(BF16)

    def group(name, sub):
        off = 0
        for gname, width, rope, scale in PROJ_GROUPS:
            if gname == name:
                break
            off += width
        lo = off + sub * LANES
        v = _dot(n, w_ref[:, lo:lo + LANES])
        if rope is not None:
            t0 = 0 if rope == "A" else LANES
            v = _rope_group(v, tc_ref[:, t0:t0 + LANES], ts1_ref[:, t0:t0 + LANES], ts2_ref[:, t0:t0 + LANES])
        if scale != 1.0:
            v = v * scale
        return v

    zero_half = jnp.zeros((HEAD_DIM, NSA_HEADS * Q_BLOCK), BF16)
    for u in range(nblk):
        nsa_qt_ref[u, HEAD_DIM:2 * HEAD_DIM, :] = zero_half
    for sub in range(NSA_Q_W // LANES):
        vt = group("nsa_q", sub).T.astype(BF16)
        for hh in range(2):
            h = 2 * sub + hh
            for u in range(nblk):
                nsa_qt_ref[u, 0:HEAD_DIM, h * Q_BLOCK:(h + 1) * Q_BLOCK] = \
                    vt[hh * HEAD_DIM:(hh + 1) * HEAD_DIM, u * Q_BLOCK:(u + 1) * Q_BLOCK]

    cmp_ref[...] = group("cmp_kv", 0)
    for name, kv_ref, vt_ref in (("sel_kv", sel_ref, selvt_ref), ("win_kv", win_ref, winvt_ref)):
        v = group(name, 0)
        kv_ref[...] = v.astype(BF16)
        vt = v.T.astype(BF16)
        for u in range(nblk):
            vt_ref[u] = vt[HEAD_DIM:2 * HEAD_DIM, u * Q_BLOCK:(u + 1) * Q_BLOCK]

    for name, o_ref in (("dil_q", dq_ref), ("dil_k", dk_ref), ("dil_v", dv_ref), ("fox_q", fq_ref),
                        ("fox_k", fk_ref)):
        for sub in range(o_ref.shape[1] // LANES):
            o_ref[:, sub * LANES:(sub + 1) * LANES] = group(name, sub).astype(BF16)

    for sub in range(FOX_W // LANES):
        vt = group("fox_v", sub).T.astype(BF16)
        for hh in range(2):
            for u in range(nblk):
                fvt_ref[2 * sub + hh, u] = vt[hh * HEAD_DIM:(hh + 1) * HEAD_DIM, u * Q_BLOCK:(u + 1) * Q_BLOCK]

    misc_ref[...] = group("misc", 0)


def _mixer_proj(h, mod, g, w, tabs, *, batch, seq, tm=PROJ_TM):
    T, D = h.shape
    assert T % tm == 0 and seq % tm == 0 and tm % Q_BLOCK == 0
    tpb = seq // tm
    nblk = tm // Q_BLOCK
    nqb = seq // Q_BLOCK
    tab_spec = pl.BlockSpec((tm, 2 * LANES), lambda i: (i % tpb, 0))

    def flat(width):
        return pl.BlockSpec((tm, width), lambda i: (i, 0))

    vt_spec = pl.BlockSpec((None, nblk, HEAD_DIM, Q_BLOCK), lambda i: (i // tpb, i % tpb, 0, 0))
    out_specs = [
        pl.BlockSpec((None, nblk, 2 * HEAD_DIM, NSA_HEADS * Q_BLOCK), lambda i: (i // tpb, i % tpb, 0, 0)),
        flat(LANES), flat(LANES), vt_spec, flat(LANES), vt_spec,
        flat(DIL_W), flat(DIL_W), flat(DIL_W), flat(FOX_W), flat(FOX_W),
        pl.BlockSpec((None, FOX_HEADS, nblk, HEAD_DIM, Q_BLOCK), lambda i: (i // tpb, 0, i % tpb, 0, 0)),
        flat(LANES),
    ]
    vt_shape = jax.ShapeDtypeStruct((batch, nqb, HEAD_DIM, Q_BLOCK), BF16)
    out_shape = [
        jax.ShapeDtypeStruct((batch, nqb, 2 * HEAD_DIM, NSA_HEADS * Q_BLOCK), BF16),
        jax.ShapeDtypeStruct((T, LANES), F32),
        jax.ShapeDtypeStruct((T, LANES), BF16), vt_shape,
        jax.ShapeDtypeStruct((T, LANES), BF16), vt_shape,
        jax.ShapeDtypeStruct((T, DIL_W), BF16), jax.ShapeDtypeStruct((T, DIL_W), BF16),
        jax.ShapeDtypeStruct((T, DIL_W), BF16),
        jax.ShapeDtypeStruct((T, FOX_W), BF16), jax.ShapeDtypeStruct((T, FOX_W), BF16),
        jax.ShapeDtypeStruct((batch, FOX_HEADS, nqb, HEAD_DIM, Q_BLOCK), BF16),
        jax.ShapeDtypeStruct((T, LANES), F32),
    ]
    return pl.pallas_call(
        _proj_kernel,
        grid=(T // tm,),
        in_specs=[
            pl.BlockSpec((tm, D), lambda i: (i, 0)),
            pl.BlockSpec((None, 9, D), lambda i: (i // tpb, 0, 0)),
            pl.BlockSpec((1, D), lambda i: (0, 0)),
            pl.BlockSpec((D, PROJ_COLS), lambda i: (0, 0), pipeline_mode=pl.Buffered(1)),
            tab_spec, tab_spec, tab_spec,
        ],
        out_specs=out_specs,
        out_shape=out_shape,
        compiler_params=_params("arbitrary"),
        name="mixer_proj",
    )(h, mod, g, w, *tabs)


def _compress_kernel(x_ref, pet_ref, peb_ref, w1t_ref, w1b_ref, w2_ref, o_ref, vt_ref):
    x = x_ref[...]
    a = _dot((x + pet_ref[...]).astype(BF16), w1t_ref[...])
    b = _dot((x + peb_ref[...]).astype(BF16), w1b_ref[...])
    nrow = x.shape[0]
    hid = a + pltpu.roll(b, nrow - 1, 0)
    hid = hid * jax.nn.sigmoid(hid)
    out = _dot(hid.astype(BF16), w2_ref[...])
    o_ref[...] = out.astype(o_ref.dtype)
    vt_ref[...] = out.T[HEAD_DIM:2 * HEAD_DIM, :].astype(vt_ref.dtype)


def _compress(cmp_kv, pe_top, pe_bot, w1_top, w1_bot, w2):
    B, S, _ = cmp_kv.shape
    nchunk = S // NSA_CMP_STRIDE
    x = cmp_kv.reshape(B, nchunk, NSA_CMP_STRIDE * LANES)
    kw = NSA_CMP_STRIDE * LANES
    hw = 2 * NSA_CMP_HIDDEN
    return pl.pallas_call(
        _compress_kernel,
        grid=(B,),
        in_specs=[
            pl.BlockSpec((None, nchunk, kw), lambda b: (b, 0, 0)),
            pl.BlockSpec((1, kw), lambda b: (0, 0)),
            pl.BlockSpec((1, kw), lambda b: (0, 0)),
            pl.BlockSpec((kw, hw), lambda b: (0, 0)),
            pl.BlockSpec((kw, hw), lambda b: (0, 0)),
            pl.BlockSpec((hw, LANES), lambda b: (0, 0)),
        ],
        out_specs=[
            pl.BlockSpec((None, nchunk, LANES), lambda b: (b, 0, 0)),
            pl.BlockSpec((None, HEAD_DIM, nchunk), lambda b: (b, 0, 0)),
        ],
        out_shape=[
            jax.ShapeDtypeStruct((B, nchunk, LANES), BF16),
            jax.ShapeDtypeStruct((B, HEAD_DIM, nchunk), BF16),
        ],
        compiler_params=_params("arbitrary"),
        name="nsa_compress",
    )(x, pe_top, pe_bot, w1_top, w1_bot, w2)


def _nsa_kernel(qt_ref, cmp_ref, cmpvt_ref, sel_ref, selvt_ref, win_ref, winvt_ref, misc_ref, ovlt_ref,
                o_ref, member_ref, cnt_ref, sa_ref, sb_ref, m_ref, l_ref, acc_ref, *, seq, tk):
    H, Q = NSA_HEADS, Q_BLOCK
    HQ = H * Q
    q0 = pl.program_id(1) * Q
    qt = qt_ref[...]
    ncmp = cmp_ref.shape[0]
    n_sel = seq // NSA_SEL_LEN
    sel_shift = NSA_SEL_LEN.bit_length() - 1

    def heads(a):
        return jnp.concatenate([a] * H, axis=1)

    wlen = NSA_WINDOW + Q
    start = pl.multiple_of(jnp.maximum(q0 - NSA_WINDOW, 0), Q)
    s = _dot(win_ref[pl.ds(start, wlen), :], qt)
    kpos = start + lax.broadcasted_iota(jnp.int32, (wlen, Q), 0)
    tw = q0 + lax.broadcasted_iota(jnp.int32, (wlen, Q), 1)
    s = s + heads(jnp.where((kpos <= tw) & (kpos > tw - NSA_WINDOW), 0.0, NEG_INF))
    e = jnp.exp2(s - jnp.max(s, axis=0, keepdims=True))
    sblk = start // Q
    vt = jnp.concatenate([winvt_ref[sblk + u] for u in range(wlen // Q)], axis=1)
    o_win = _dot(vt, e.astype(BF16)) / jnp.sum(e, axis=0, keepdims=True)

    s = _dot(cmp_ref[...], qt)
    nn = lax.broadcasted_iota(jnp.int32, (ncmp, Q), 0)
    tt = q0 + lax.broadcasted_iota(jnp.int32, (ncmp, Q), 1)
    cmask = heads((nn * NSA_CMP_STRIDE + (NSA_CMP_LEN - 1) <= tt) & (nn < ncmp - 1))
    s = jnp.where(cmask, s, NEG_INF)
    m = jnp.max(s, axis=0, keepdims=True)
    e = jnp.where(cmask, jnp.exp2(s - m), 0.0)
    den = jnp.sum(e, axis=0, keepdims=True)
    p_cmp = e / jnp.where(den > 0, den, 1.0)
    o_cmp = _dot(cmpvt_ref[...], p_cmp.astype(BF16))

    psum = p_cmp[:, 0:Q] + p_cmp[:, Q:2 * Q] + p_cmp[:, 2 * Q:3 * Q] + p_cmp[:, 3 * Q:4 * Q]
    hi = psum.astype(BF16)
    lo = (psum - hi.astype(F32)).astype(BF16)
    ovlt = ovlt_ref[...]
    imp = _dot(ovlt, hi) + _dot(ovlt, lo)
    jj = lax.broadcasted_iota(jnp.int32, (n_sel, Q), 0)
    tq = q0 + lax.broadcasted_iota(jnp.int32, (n_sel, Q), 1)
    valid = jj * NSA_SEL_LEN <= tq
    forced = (jj == (tq >> sel_shift)) | (jj == 0)
    imp = jnp.where(forced, NSA_FORCE_SCORE, jnp.where(valid, imp, -1.0))
    bpt = tk // NSA_SEL_LEN
    vpt = tk // Q
    n_full = q0 // tk

    def sel_scores(k, dst_ref):
        dst_ref[...] = _dot(sel_ref[pl.ds(pl.multiple_of(k * tk, tk), tk), :], qt)

    sel_scores(0, sa_ref)
    m_ref[...] = jnp.full((1, HQ), NEG_INF, F32)
    l_ref[...] = jnp.zeros((1, HQ), F32)
    acc_ref[...] = jnp.zeros((HEAD_DIM, HQ), F32)

    ngrp = n_sel // SUBLANES
    grp = [imp[r * SUBLANES:(r + 1) * SUBLANES, :] for r in range(ngrp)]
    jrow = lax.broadcasted_iota(jnp.int32, (SUBLANES, Q), 0)
    cnt_ref[...] = jnp.zeros((n_sel, Q), F32)
    last_started = (q0 + Q - 1) >> sel_shift
    for ib in range(ngrp):
        @pl.when(ib * SUBLANES <= last_started)
        def _(ib=ib):
            cnt = [cnt_ref[r * SUBLANES:(r + 1) * SUBLANES, :] for r in range(ngrp)]
            for i in range(ib * SUBLANES, (ib + 1) * SUBLANES):
                row = jnp.broadcast_to(imp[i:i + 1, :], (SUBLANES, Q))
                for r in range(ngrp):
                    if r > ib:
                        hit = jnp.where(row >= grp[r], 1.0, 0.0)
                    elif r < ib:
                        hit = jnp.where(row > grp[r], 1.0, 0.0)
                    else:
                        hit = jnp.where(jrow + r * SUBLANES > i, jnp.where(row >= grp[r], 1.0, 0.0),
                                        jnp.where(row > grp[r], 1.0, 0.0))
                    cnt[r] = cnt[r] + hit
            for r in range(ngrp):
                cnt_ref[r * SUBLANES:(r + 1) * SUBLANES, :] = cnt[r]
    top_k = min(NSA_SEL_TOPK, n_sel)
    member_ref[...] = jnp.where(cnt_ref[...] < top_k, 0.0, NEG_INF)

    def sel_update(k, src_ref, causal):
        bias = jnp.concatenate(
            [jnp.broadcast_to(member_ref[pl.ds(k * bpt + jb, 1), :], (NSA_SEL_LEN, Q)) for jb in range(bpt)],
            axis=0)
        if causal:
            kpos = k * tk + lax.broadcasted_iota(jnp.int32, (tk, Q), 0)
            tcol = q0 + lax.broadcasted_iota(jnp.int32, (tk, Q), 1)
            bias = jnp.where(kpos <= tcol, bias, NEG_INF)
        s = src_ref[...] + heads(bias)
        m = m_ref[...]
        m_new = jnp.maximum(m, jnp.max(s, axis=0, keepdims=True))
        e = jnp.exp2(s - m_new)
        alpha = jnp.exp2(m - m_new)
        l_ref[...] = alpha * l_ref[...] + jnp.sum(e, axis=0, keepdims=True)
        vt = jnp.concatenate([selvt_ref[k * vpt + u] for u in range(vpt)], axis=1)
        acc_ref[...] = alpha * acc_ref[...] + _dot(vt, e.astype(BF16))
        m_ref[...] = m_new

    def sel_pair(j, _):
        sel_scores(2 * j + 1, sb_ref)
        sel_update(2 * j, sa_ref, False)
        sel_scores(2 * j + 2, sa_ref)
        sel_update(2 * j + 1, sb_ref, False)
        return 0

    lax.fori_loop(0, n_full // 2, sel_pair, 0)

    @pl.when(n_full % 2 == 1)
    def _():
        sel_scores(n_full, sb_ref)
        sel_update(n_full - 1, sa_ref, False)
        sel_update(n_full, sb_ref, True)

    @pl.when(n_full % 2 == 0)
    def _():
        sel_update(n_full, sa_ref, True)

    o_sel = acc_ref[...] / l_ref[...]

    g = jax.nn.sigmoid(misc_ref[...].T)
    outs = []
    for h in range(H):
        r0 = MISC_GATE_LANE + 3 * h
        cols = slice(h * Q, (h + 1) * Q)
        outs.append(g[r0:r0 + 1, :] * o_cmp[:, cols] + g[r0 + 1:r0 + 2, :] * o_sel[:, cols]
                    + g[r0 + 2:r0 + 3, :] * o_win[:, cols])
    o_ref[...] = jnp.concatenate(outs, axis=0).T.astype(o_ref.dtype)


def _nsa_attention(qt, cmp_out, cmp_vt, sel_kv, sel_vt, win_kv, win_vt, misc, ovlt, *, tk=512):
    B, S, _ = sel_kv.shape
    ncmp = cmp_out.shape[1]
    n_sel = S // NSA_SEL_LEN
    nqb = S // Q_BLOCK
    assert S % tk == 0 and tk % Q_BLOCK == 0 and S >= NSA_WINDOW + Q_BLOCK and n_sel % SUBLANES == 0
    kern = functools.partial(_nsa_kernel, seq=S, tk=tk)
    return pl.pallas_call(
        kern,
        grid=(B, nqb),
        in_specs=[
            pl.BlockSpec((None, None, 2 * HEAD_DIM, NSA_HEADS * Q_BLOCK), lambda b, i: (b, i, 0, 0)),
            pl.BlockSpec((None, ncmp, LANES), lambda b, i: (b, 0, 0)),
            pl.BlockSpec((None, HEAD_DIM, ncmp), lambda b, i: (b, 0, 0)),
            pl.BlockSpec((None, S, LANES), lambda b, i: (b, 0, 0)),
            pl.BlockSpec((None, nqb, HEAD_DIM, Q_BLOCK), lambda b, i: (b, 0, 0, 0)),
            pl.BlockSpec((None, S, LANES), lambda b, i: (b, 0, 0)),
            pl.BlockSpec((None, nqb, HEAD_DIM, Q_BLOCK), lambda b, i: (b, 0, 0, 0)),
            pl.BlockSpec((None, Q_BLOCK, LANES), lambda b, i: (b, i, 0)),
            pl.BlockSpec((n_sel, ncmp), lambda b, i: (0, 0)),
        ],
        out_specs=pl.BlockSpec((None, Q_BLOCK, NSA_Q_W), lambda b, i: (b, i, 0)),
        out_shape=jax.ShapeDtypeStruct((B, S, NSA_Q_W), BF16),
        scratch_shapes=[
            pltpu.VMEM((n_sel, Q_BLOCK), F32),
            pltpu.VMEM((n_sel, Q_BLOCK), F32),
            pltpu.VMEM((tk, NSA_HEADS * Q_BLOCK), F32),
            pltpu.VMEM((tk, NSA_HEADS * Q_BLOCK), F32),
            pltpu.VMEM((1, NSA_HEADS * Q_BLOCK), F32),
            pltpu.VMEM((1, NSA_HEADS * Q_BLOCK), F32),
            pltpu.VMEM((HEAD_DIM, NSA_HEADS * Q_BLOCK), F32),
        ],
        compiler_params=_params("arbitrary", "arbitrary"),
        name="nsa_attention",
    )(qt, cmp_out, cmp_vt, sel_kv, sel_vt, win_kv, win_vt, misc, ovlt)


def _dil_kernel(q_ref, k_ref, v_ref, o_ref):
    Q = Q_BLOCK
    q0 = pl.program_id(1) * Q
    q = q_ref[...]
    outs = [[], []]
    lses = [[], []]
    for g, (window, dil) in enumerate(DIL_PATTERNS):
        span = window + Q
        start = pl.multiple_of(jnp.maximum(q0 - window, 0), Q)
        kg = k_ref[pl.ds(start, span), g * LANES:(g + 1) * LANES]
        vg = v_ref[pl.ds(start, span), g * LANES:(g + 1) * LANES]
        dist = (q0 - start) + lax.broadcasted_iota(jnp.int32, (Q, span), 0) \
            - lax.broadcasted_iota(jnp.int32, (Q, span), 1)
        valid = (dist >= 0) & (dist <= window) & ((dist & (dil - 1)) == 0)
        for hh in range(DIL_HEADS_PER_GROUP):
            c0 = g * LANES + hh * HEAD_DIM
            s = _dot_nt(q[:, c0:c0 + HEAD_DIM], kg[:, hh * HEAD_DIM:(hh + 1) * HEAD_DIM])
            p, lse = _masked_softmax(s, valid)
            outs[hh].append(_dot(p.astype(BF16), vg[:, hh * HEAD_DIM:(hh + 1) * HEAD_DIM]))
            lses[hh].append(lse)
    res = []
    for hh in range(DIL_HEADS_PER_GROUP):
        mx = jnp.maximum(jnp.maximum(lses[hh][0], lses[hh][1]), lses[hh][2])
        w = [jnp.exp(l - mx) for l in lses[hh]]
        wsum = w[0] + w[1] + w[2]
        res.append((w[0] / wsum) * outs[hh][0] + (w[1] / wsum) * outs[hh][1] + (w[2] / wsum) * outs[hh][2])
    o_ref[...] = jnp.concatenate(res, axis=1).astype(o_ref.dtype)


def _dilated_attention(q, k, v):
    B, S, _ = q.shape
    assert all(d & (d - 1) == 0 for _, d in DIL_PATTERNS)
    assert S >= max(w for w, _ in DIL_PATTERNS) + Q_BLOCK
    return pl.pallas_call(
        _dil_kernel,
        grid=(B, S // Q_BLOCK),
        in_specs=[
            pl.BlockSpec((None, Q_BLOCK, DIL_W), lambda b, i: (b, i, 0)),
            pl.BlockSpec((None, S, DIL_W), lambda b, i: (b, 0, 0)),
            pl.BlockSpec((None, S, DIL_W), lambda b, i: (b, 0, 0)),
        ],
        out_specs=pl.BlockSpec((None, Q_BLOCK, DIL_OUT_W), lambda b, i: (b, i, 0)),
        out_shape=jax.ShapeDtypeStruct((B, S, DIL_OUT_W), BF16),
        compiler_params=_params("arbitrary", "arbitrary"),
        name="dilated_attention",
    )(q, k, v)


def _fox_prep_kernel(misc_ref, bias_ref, q_ref, k_ref, kaug_ref, qtaug_ref, carry_ref):
    tb = misc_ref.shape[0]

    @pl.when(pl.program_id(1) == 0)
    def _():
        carry_ref[...] = jnp.zeros_like(carry_ref)

    r = lax.broadcasted_iota(jnp.int32, (LANES, LANES), 0)
    c = lax.broadcasted_iota(jnp.int32, (LANES, LANES), 1)
    tri = jnp.where(r >= c, 1.0, 0.0).astype(F32)
    carry = carry_ref[...]
    cums = []
    for blk in range(tb // LANES):
        x = misc_ref[blk * LANES:(blk + 1) * LANES, :] + bias_ref[...]
        log_f = -(jnp.maximum(-x, 0.0) + jnp.log1p(jnp.exp(-jnp.abs(x))))
        cs = jnp.dot(tri, log_f, preferred_element_type=F32, precision=lax.Precision.HIGHEST) + carry
        cums.append(cs)
        carry = cs[LANES - 1:LANES, :]
    carry_ref[...] = carry
    cum = jnp.concatenate(cums, axis=0) * LOG2E

    lane = lax.broadcasted_iota(jnp.int32, (tb, LANES), 1)
    a0 = HEAD_DIM
    for h in range(FOX_HEADS):
        cb = jnp.broadcast_to(cum[:, h:h + 1], (tb, LANES))
        hi = cb.astype(BF16).astype(F32)
        r1 = cb - hi
        mid = r1.astype(BF16).astype(F32)
        lo = r1 - mid
        pieces = jnp.where(lane == a0, hi, jnp.where(lane == a0 + 1, mid, lo))
        g0 = (h // 2) * LANES
        qg = q_ref[:, g0:g0 + LANES].astype(F32)
        kg = k_ref[:, g0:g0 + LANES].astype(F32)
        if h % 2 == 1:
            qg = pltpu.roll(qg, HEAD_DIM, 1)
            kg = pltpu.roll(kg, HEAD_DIM, 1)
        in_c = (lane >= a0) & (lane < a0 + FOX_AUG)
        in_1 = (lane >= a0 + FOX_AUG) & (lane < a0 + 2 * FOX_AUG)
        q_aug = jnp.where(lane < a0, qg, jnp.where(in_c, pieces, jnp.where(in_1, 1.0, 0.0)))
        k_neg = -pltpu.roll(pieces, FOX_AUG, 1)
        k_aug = jnp.where(lane < a0, kg, jnp.where(in_c, 1.0, jnp.where(in_1, k_neg, 0.0)))
        kaug_ref[h] = k_aug.astype(BF16)
        qtaug_ref[h] = q_aug.T.astype(BF16)


def _fox_prep(misc, bias_row, q, k, *, tb=512):
    B, S, _ = misc.shape
    assert S % tb == 0
    return pl.pallas_call(
        _fox_prep_kernel,
        grid=(B, S // tb),
        in_specs=[
            pl.BlockSpec((None, tb, LANES), lambda b, i: (b, i, 0)),
            pl.BlockSpec((1, LANES), lambda b, i: (0, 0)),
            pl.BlockSpec((None, tb, FOX_W), lambda b, i: (b, i, 0)),
            pl.BlockSpec((None, tb, FOX_W), lambda b, i: (b, i, 0)),
        ],
        out_specs=[
            pl.BlockSpec((None, FOX_HEADS, tb, LANES), lambda b, i: (b, 0, i, 0)),
            pl.BlockSpec((None, FOX_HEADS, LANES, tb), lambda b, i: (b, 0, 0, i)),
        ],
        out_shape=[
            jax.ShapeDtypeStruct((B, FOX_HEADS, S, LANES), BF16),
            jax.ShapeDtypeStruct((B, FOX_HEADS, LANES, S), BF16),
        ],
        scratch_shapes=[pltpu.VMEM((1, LANES), F32)],
        compiler_params=_params("arbitrary", "arbitrary"),
        name="fox_prep",
    )(misc, bias_row, q, k)


def _fox_kernel(qt_ref, k_ref, vt_ref, o_ref, sa_ref, sb_ref, m_ref, l_ref, acc_ref, *, tq, tk):
    nh = qt_ref.shape[0]
    q0 = pl.program_id(2) * tq
    n_full = q0 // tk
    vpt = tk // Q_BLOCK

    def scores(k, dst_ref):
        base = pl.multiple_of(k * tk, tk)
        for h in range(nh):
            dst_ref[h] = _dot(k_ref[h, pl.ds(base, tk), :], qt_ref[h])

    def update(k, src_ref, causal):
        for h in range(nh):
            s = src_ref[h]
            if causal:
                kpos = k * tk + lax.broadcasted_iota(jnp.int32, (tk, tq), 0)
                tcol = q0 + lax.broadcasted_iota(jnp.int32, (tk, tq), 1)
                s = jnp.where(kpos <= tcol, s, NEG_INF)
            m = m_ref[h]
            m_new = jnp.maximum(m, jnp.max(s, axis=0, keepdims=True))
            e = jnp.exp2(s - m_new)
            alpha = jnp.exp2(m - m_new)
            l_ref[h] = alpha * l_ref[h] + jnp.sum(e, axis=0, keepdims=True)
            vt = jnp.concatenate([vt_ref[h, k * vpt + u] for u in range(vpt)], axis=1)
            acc_ref[h] = alpha * acc_ref[h] + _dot(vt, e.astype(BF16))
            m_ref[h] = m_new

    scores(0, sa_ref)
    m_ref[...] = jnp.full(m_ref.shape, NEG_INF, F32)
    l_ref[...] = jnp.zeros(l_ref.shape, F32)
    acc_ref[...] = jnp.zeros(acc_ref.shape, F32)

    def pair(j, _):
        scores(2 * j + 1, sb_ref)
        update(2 * j, sa_ref, False)
        scores(2 * j + 2, sa_ref)
        update(2 * j + 1, sb_ref, False)
        return 0

    lax.fori_loop(0, n_full // 2, pair, 0)

    @pl.when(n_full % 2 == 1)
    def _():
        scores(n_full, sb_ref)
        update(n_full - 1, sa_ref, False)
        update(n_full, sb_ref, True)

    @pl.when(n_full % 2 == 0)
    def _():
        update(n_full, sa_ref, True)

    outs = [acc_ref[h] / l_ref[h] for h in range(nh)]
    o_ref[...] = jnp.concatenate(outs, axis=0).T.astype(o_ref.dtype)


def _fox_attention(qt_aug, k_aug, vt, *, tq=256, tk=512, heads_per_step=FOX_HEADS):
    B, H, S, _ = k_aug.shape
    hps = heads_per_step
    assert S % tk == 0 and tk % tq == 0 and tk % Q_BLOCK == 0 and H % hps == 0 and (hps * HEAD_DIM) % LANES == 0
    nqb = S // Q_BLOCK
    kern = functools.partial(_fox_kernel, tq=tq, tk=tk)
    return pl.pallas_call(
        kern,
        grid=(B, H // hps, S // tq),
        in_specs=[
            pl.BlockSpec((None, hps, LANES, tq), lambda b, p, i: (b, p, 0, i)),
            pl.BlockSpec((None, hps, S, LANES), lambda b, p, i: (b, p, 0, 0)),
            pl.BlockSpec((None, hps, nqb, HEAD_DIM, Q_BLOCK), lambda b, p, i: (b, p, 0, 0, 0)),
        ],
        out_specs=pl.BlockSpec((None, tq, hps * HEAD_DIM), lambda b, p, i: (b, i, p)),
        out_shape=jax.ShapeDtypeStruct((B, S, H * HEAD_DIM), BF16),
        scratch_shapes=[
            pltpu.VMEM((hps, tk, tq), F32),
            pltpu.VMEM((hps, tk, tq), F32),
            pltpu.VMEM((hps, 1, tq), F32),
            pltpu.VMEM((hps, 1, tq), F32),
            pltpu.VMEM((hps, HEAD_DIM, tq), F32),
        ],
        compiler_params=_params("arbitrary", "arbitrary", "arbitrary"),
        name="fox_attention",
    )(qt_aug, k_aug, vt)


def _merge_kernel(x_ref, mod_ref, g_ref, ya_ref, yb_ref, yc_ref, wg_ref, bra_ref, brb_ref, brc_ref, wo_ref, o_ref):
    x = x_ref[...]
    D = x.shape[1]
    n = _norm_modulate(x, g_ref[...], mod_ref[3:4, :], mod_ref[4:5, :]).astype(BF16)
    merged = jax.nn.sigmoid(_dot(n, wg_ref[:, 0:D])) * _dot(ya_ref[...], bra_ref[...])
    merged = merged + jax.nn.sigmoid(_dot(n, wg_ref[:, D:2 * D])) * _dot(yb_ref[...], brb_ref[...])
    merged = merged + jax.nn.sigmoid(_dot(n, wg_ref[:, 2 * D:3 * D])) * _dot(yc_ref[...], brc_ref[...])
    o_ref[...] = x + mod_ref[5:6, :] * _dot(merged.astype(BF16), wo_ref[...])


def _merge(h, mod, g, ya, yb, yc, w_gate, br_a, br_b, br_c, w_out, *, seq, tm=PROJ_TM):
    T, D = h.shape
    assert T % tm == 0 and seq % tm == 0
    tpb = seq // tm

    def resident(shape):
        return pl.BlockSpec(shape, lambda i: (0, 0), pipeline_mode=pl.Buffered(1))

    return pl.pallas_call(
        _merge_kernel,
        grid=(T // tm,),
        in_specs=[
            pl.BlockSpec((tm, D), lambda i: (i, 0)),
            pl.BlockSpec((None, 9, D), lambda i: (i // tpb, 0, 0)),
            pl.BlockSpec((1, D), lambda i: (0, 0)),
            pl.BlockSpec((tm, NSA_Q_W), lambda i: (i, 0)),
            pl.BlockSpec((tm, DIL_OUT_W), lambda i: (i, 0)),
            pl.BlockSpec((tm, FOX_W), lambda i: (i, 0)),
            resident((D, 3 * D)),
            resident((NSA_Q_W, D)),
            resident((DIL_OUT_W, D)),
            resident((FOX_W, D)),
            resident((D, D)),
        ],
        out_specs=pl.BlockSpec((tm, D), lambda i: (i, 0)),
        out_shape=jax.ShapeDtypeStruct((T, D), F32),
        compiler_params=_params("arbitrary"),
        name="merge_out",
    )(h, mod, g, ya, yb, yc, w_gate, br_a, br_b, br_c, w_out)


def _rope_tables(seq):
    inv_freq = ROPE_THETA ** (-jnp.arange(0, ROPE_DIM, 2, dtype=F32) / ROPE_DIM)
    ang = jnp.arange(seq, dtype=F32)[:, None] * inv_freq[None, :]
    cos, sin = jnp.cos(ang), jnp.sin(ang)
    d = np.arange(LANES) % HEAD_DIM
    idx = d % ROPE_HALF
    first = jnp.asarray(d < ROPE_HALF)
    second = jnp.asarray((d >= ROPE_HALF) & (d < ROPE_DIM))
    c_a = jnp.where(first | second, cos[:, idx], 1.0)
    s1_a = jnp.where(first, -sin[:, idx], 0.0)
    s2_a = jnp.where(second, sin[:, idx], 0.0)
    head0 = jnp.asarray(np.arange(LANES) < HEAD_DIM)
    c_b = jnp.where(head0, c_a, 1.0)
    s1_b = jnp.where(head0, s1_a, 0.0)
    s2_b = jnp.where(head0, s2_a, 0.0)
    return (jnp.concatenate([c_a, c_b], axis=1), jnp.concatenate([s1_a, s1_b], axis=1),
            jnp.concatenate([s2_a, s2_b], axis=1))


def _pack_mix_w_in(w):
    D = w.shape[0]
    o = 0
    nsa_q = w[:, o:o + NSA_Q_W]; o += NSA_Q_W
    kv = w[:, o:o + 6 * HEAD_DIM]; o += 6 * HEAD_DIM
    nsa_g = w[:, o:o + 3 * NSA_HEADS]; o += 3 * NSA_HEADS
    dil = w[:, o:o + 3 * DIL_W]; o += 3 * DIL_W
    fox = w[:, o:o + 3 * FOX_W]; o += 3 * FOX_W
    fox_f = w[:, o:o + FOX_HEADS]; o += FOX_HEADS
    merge_g = w[:, o:]
    pad0 = jnp.zeros((D, MISC_GATE_LANE - MISC_FOX_LANE - FOX_HEADS), w.dtype)
    pad1 = jnp.zeros((D, LANES - MISC_GATE_LANE - 3 * NSA_HEADS), w.dtype)
    packed = jnp.concatenate([nsa_q, kv, dil, fox, fox_f, pad0, nsa_g, pad1], axis=1)
    assert packed.shape[1] == PROJ_COLS
    return packed.astype(BF16), merge_g.astype(BF16)


def _pack_compress(pe, w1, w2):
    half = NSA_CMP_STRIDE
    hid = NSA_CMP_HIDDEN
    w1k = w1[0].reshape(NSA_CMP_LEN, HEAD_DIM, hid)
    w1v = w1[1].reshape(NSA_CMP_LEN, HEAD_DIM, hid)
    z = jnp.zeros((half, HEAD_DIM, hid), w1.dtype)

    def halfpack(lo):
        kk = jnp.concatenate([w1k[lo:lo + half], z], axis=-1)
        vv = jnp.concatenate([z, w1v[lo:lo + half]], axis=-1)
        return jnp.concatenate([kk, vv], axis=1).reshape(half * LANES, 2 * hid).astype(BF16)

    def pepack(lo):
        return jnp.concatenate([pe[0, lo:lo + half], pe[1, lo:lo + half]], axis=-1).reshape(1, half * LANES)

    zz = jnp.zeros((hid, HEAD_DIM), w2.dtype)
    w2p = jnp.concatenate([jnp.concatenate([w2[0], zz], axis=1),
                           jnp.concatenate([zz, w2[1]], axis=1)], axis=0).astype(BF16)
    return pepack(0), pepack(half), halfpack(0), halfpack(half), w2p


def _overlap_matrix_t(seq):
    n_chunk = seq // NSA_CMP_STRIDE
    n_cmp = (seq - NSA_CMP_LEN) // NSA_CMP_STRIDE + 1
    n_sel = seq // NSA_SEL_LEN
    cmp_start = np.arange(n_cmp) * NSA_CMP_STRIDE
    sel_start = np.arange(n_sel) * NSA_SEL_LEN
    ov = np.minimum(cmp_start[:, None] + NSA_CMP_LEN, sel_start[None, :] + NSA_SEL_LEN) \
        - np.maximum(cmp_start[:, None], sel_start[None, :])
    full = np.zeros((n_sel, n_chunk), np.float32)
    full[:, :n_cmp] = (np.clip(ov, 0, None) / NSA_CMP_LEN).T
    return jnp.asarray(full, dtype=BF16)


def kernel(x, c, ada_w, ada_b, norm_g, final_norm_g, ffn_w_in, ffn_w_out, mix_w_in, nsa_cmp_pe, nsa_cmp_w1,
           nsa_cmp_w2, fox_f_bias, br_w_nsa, br_w_dil, br_w_fox, mix_w_out):
    B, S, D = x.shape
    L = ada_w.shape[0]
    T = B * S
    mod_all = _ada_modulation(c, ada_w, ada_b).reshape(L, B, 9, D)
    tabs = _rope_tables(S)
    ovlt = _overlap_matrix_t(S)
    fg = final_norm_g.reshape(1, D)
    h = x.reshape(T, D)
    for l in range(L):
        mod = mod_all[l]
        h = _ffn(h, mod, norm_g[l, 0].reshape(1, D), ffn_w_in[l, 0].astype(BF16), ffn_w_out[l, 0].astype(BF16),
                 fg, mod_base=0, final=False, seq=S)

        w_proj, w_gate = _pack_mix_w_in(mix_w_in[l])
        g1 = norm_g[l, 1].reshape(1, D)
        (nsa_qt, cmp_kv, sel_kv, sel_vt, win_kv, win_vt, dil_q, dil_k, dil_v, fox_q, fox_k, fox_vt,
         misc) = _mixer_proj(h, mod, g1, w_proj, tabs, batch=B, seq=S)

        def bsd(a):
            return a.reshape(B, S, a.shape[-1])

        misc = bsd(misc)
        cmp_out, cmp_vt = _compress(bsd(cmp_kv), *_pack_compress(nsa_cmp_pe[l], nsa_cmp_w1[l], nsa_cmp_w2[l]))
        y_a = _nsa_attention(nsa_qt, cmp_out, cmp_vt, bsd(sel_kv), sel_vt, bsd(win_kv), win_vt, misc, ovlt)
        y_b = _dilated_attention(bsd(dil_q), bsd(dil_k), bsd(dil_v))
        bias_row = jnp.pad(fox_f_bias[l].reshape(1, FOX_HEADS),
                           ((0, 0), (MISC_FOX_LANE, LANES - MISC_FOX_LANE - FOX_HEADS)))
        k_aug, qt_aug = _fox_prep(misc, bias_row, bsd(fox_q), bsd(fox_k))
        y_c = _fox_attention(qt_aug, k_aug, fox_vt)

        h = _merge(h, mod, g1, y_a.reshape(T, NSA_Q_W), y_b.reshape(T, DIL_OUT_W), y_c.reshape(T, FOX_W),
                   w_gate, br_w_nsa[l].astype(BF16), br_w_dil[l].astype(BF16), br_w_fox[l].astype(BF16),
                   mix_w_out[l].astype(BF16), seq=S)

        h = _ffn(h, mod, norm_g[l, 2].reshape(1, D), ffn_w_in[l, 1].astype(BF16), ffn_w_out[l, 1].astype(BF16),
                 fg, mod_base=6, final=(l == L - 1), seq=S)
    return h.reshape(B, S, D)
```

```python
import functools
import math

import numpy as np
import jax
import jax.numpy as jnp
from jax import lax
from jax.experimental import pallas as pl
from jax.experimental.pallas import tpu as pltpu

F32 = jnp.float32
BF16 = jnp.bfloat16

HEAD_DIM = 64
ROPE_DIM = 16
ROPE_HALF = ROPE_DIM // 2
ROPE_THETA = 500000.0
Q_BLOCK = 128
NEG_INF = -1e30
RMS_EPS = 1e-6
QK_SCALE = HEAD_DIM ** -0.5
LOG2E = math.log2(math.e)

NSA_HEADS = 4
NSA_CMP_LEN = 32
NSA_CMP_STRIDE = 16
NSA_CMP_HIDDEN = 128
NSA_SEL_LEN = 64
NSA_SEL_TOPK = 16
NSA_WINDOW = 512
NSA_FORCE_SCORE = 1e4

DIL_PATTERNS = ((128, 1), (512, 4), (2048, 16))
DIL_HEADS_PER_GROUP = 2
DIL_HEADS = DIL_HEADS_PER_GROUP * len(DIL_PATTERNS)
FOX_HEADS = 6

NSA_Q_W = NSA_HEADS * HEAD_DIM
DIL_W = DIL_HEADS * HEAD_DIM
DIL_OUT_W = DIL_HEADS_PER_GROUP * HEAD_DIM
FOX_W = FOX_HEADS * HEAD_DIM

LANES = 128
SUBLANES = 8
MISC_FOX_LANE = 0
MISC_GATE_LANE = 8
PROJ_TM = 512
FOX_AUG = 3

PROJ_GROUPS = (
    ("nsa_q", NSA_Q_W, "A", QK_SCALE * LOG2E),
    ("cmp_kv", LANES, "B", 1.0),
    ("sel_kv", LANES, "B", 1.0),
    ("win_kv", LANES, "B", 1.0),
    ("dil_q", DIL_W, "A", QK_SCALE * LOG2E),
    ("dil_k", DIL_W, "A", 1.0),
    ("dil_v", DIL_W, None, 1.0),
    ("fox_q", FOX_W, None, QK_SCALE * LOG2E),
    ("fox_k", FOX_W, None, 1.0),
    ("fox_v", FOX_W, None, 1.0),
    ("misc", LANES, None, 1.0),
)
PROJ_COLS = sum(g[1] for g in PROJ_GROUPS)

VMEM_LIMIT = 56 * 1024 * 1024


def _dot(a, b):
    return jnp.dot(a, b, preferred_element_type=F32)


def _dot_nt(a, b):
    return lax.dot_general(a, b, (((1,), (1,)), ((), ())), preferred_element_type=F32)


def _norm_modulate(x, g, shift, scale):
    ms = jnp.mean(x * x, axis=-1, keepdims=True)
    y = x * lax.rsqrt(ms + RMS_EPS) * g
    return y * (1.0 + scale) + shift


def _masked_softmax(s, mask):
    s = jnp.where(mask, s, NEG_INF)
    m = jnp.max(s, axis=-1, keepdims=True)
    e = jnp.where(mask, jnp.exp(s - m), 0.0)
    den = jnp.sum(e, axis=-1, keepdims=True)
    den = jnp.where(den > 0, den, 1.0)
    return e / den, m + jnp.log(den)


def _params(*sem):
    return pltpu.CompilerParams(dimension_semantics=sem, vmem_limit_bytes=VMEM_LIMIT)


def _ada_kernel(c_ref, w_ref, b_ref, o_ref):
    c = c_ref[...]
    cond = c * jax.nn.sigmoid(c)
    o_ref[...] = jnp.dot(cond, w_ref[...], preferred_element_type=F32,
                         precision=lax.Precision.HIGHEST) + b_ref[...]


def _ada_modulation(c, ada_w, ada_b):
    L, D, N = ada_w.shape
    B = c.shape[0]
    tn = 1152
    assert N % tn == 0
    return pl.pallas_call(
        _ada_kernel,
        grid=(L, N // tn),
        in_specs=[
            pl.BlockSpec((B, D), lambda l, j: (0, 0)),
            pl.BlockSpec((None, D, tn), lambda l, j: (l, 0, j)),
            pl.BlockSpec((None, 1, tn), lambda l, j: (l, 0, j)),
        ],
        out_specs=pl.BlockSpec((None, B, tn), lambda l, j: (l, 0, j)),
        out_shape=jax.ShapeDtypeStruct((L, B, N), F32),
        compiler_params=_params("arbitrary", "arbitrary"),
        name="ada_modulation",
    )(c, ada_w, ada_b.reshape(L, 1, N))


def _ffn_kernel(x_ref, mod_ref, g_ref, win_ref, wout_ref, fg_ref, o_ref, a_ref, *, mod_base, d_ff, chunk, final):
    x = x_ref[...]
    n = _norm_modulate(x, g_ref[...], mod_ref[mod_base:mod_base + 1, :],
                       mod_ref[mod_base + 1:mod_base + 2, :]).astype(BF16)
    for j in range(d_ff // chunk):
        gate = _dot(n, win_ref[:, j * chunk:(j + 1) * chunk])
        up = _dot(n, win_ref[:, d_ff + j * chunk:d_ff + (j + 1) * chunk])
        a_ref[:, j * chunk:(j + 1) * chunk] = (gate * jax.nn.sigmoid(gate) * up).astype(BF16)
    f = _dot(a_ref[...], wout_ref[...])
    out = x + (0.5 * mod_ref[mod_base + 2:mod_base + 3, :]) * f
    if final:
        ms = jnp.mean(out * out, axis=-1, keepdims=True)
        out = out * lax.rsqrt(ms + RMS_EPS) * fg_ref[...]
    o_ref[...] = out


def _ffn(h, mod, g, w_in, w_out, final_g, *, layer, which, mod_base, final, seq, tm=PROJ_TM):
    T, D = h.shape
    d_ff = w_out.shape[2]
    chunk = 256
    assert T % tm == 0 and seq % tm == 0 and d_ff % chunk == 0
    tpb = seq // tm
    kern = functools.partial(_ffn_kernel, mod_base=mod_base, d_ff=d_ff, chunk=chunk, final=final)
    return pl.pallas_call(
        kern,
        grid=(T // tm,),
        in_specs=[
            pl.BlockSpec((tm, D), lambda i: (i, 0)),
            pl.BlockSpec((None, 9, D), lambda i: (i // tpb, 0, 0)),
            pl.BlockSpec((1, D), lambda i: (0, 0)),
            pl.BlockSpec((None, None, D, 2 * d_ff), lambda i: (layer, which, 0, 0), pipeline_mode=pl.Buffered(1)),
            pl.BlockSpec((None, None, d_ff, D), lambda i: (layer, which, 0, 0), pipeline_mode=pl.Buffered(1)),
            pl.BlockSpec((1, D), lambda i: (0, 0)),
        ],
        out_specs=pl.BlockSpec((tm, D), lambda i: (i, 0)),
        out_shape=jax.ShapeDtypeStruct((T, D), F32),
        scratch_shapes=[pltpu.VMEM((tm, d_ff), BF16)],
        compiler_params=_params("arbitrary"),
        name="ffn",
    )(h, mod, g, w_in, w_out, final_g)


def _rope_group(v, c, s1, s2):
    return v * c + pltpu.roll(v, LANES - ROPE_HALF, 1) * s1 + pltpu.roll(v, ROPE_HALF, 1) * s2


def _proj_kernel(x_ref, mod_ref, g_ref, w_ref, tc_ref, ts1_ref, ts2_ref,
                 nsa_qt_ref, cmp_ref, sel_ref, selvt_ref, win_ref, winvt_ref,
                 dq_ref, dk_ref, dv_ref, fq_ref, fk_ref, fvt_ref, misc_ref):
    x = x_ref[...]
    tm = x.shape[0]
    nblk = tm // Q_BLOCK
    n = _norm_modulate(x, g_ref[...], mod_ref[3:4, :], mod_ref[4:5, :]).astype(BF16)

    wide = {}

    def group(name, sub):
        off = 0
        for gname, width, rope, scale in PROJ_GROUPS:
            if gname == name:
                break
            off += width
        lo = off + sub * LANES
        blk = lo // (2 * LANES)
        if blk not in wide:
            wide[blk] = _dot(n, w_ref[:, blk * 2 * LANES:(blk + 1) * 2 * LANES])
        v = wide[blk][:, lo % (2 * LANES):lo % (2 * LANES) + LANES]
        if rope is not None:
            t0 = 0 if rope == "A" else LANES
            v = _rope_group(v, tc_ref[:, t0:t0 + LANES], ts1_ref[:, t0:t0 + LANES], ts2_ref[:, t0:t0 + LANES])
        if scale != 1.0:
            v = v * scale
        return v

    zero_half = jnp.zeros((HEAD_DIM, NSA_HEADS * Q_BLOCK), BF16)
    for u in range(nblk):
        nsa_qt_ref[u, HEAD_DIM:2 * HEAD_DIM, :] = zero_half
    for sub in range(NSA_Q_W // LANES):
        vt = group("nsa_q", sub).T.astype(BF16)
        for hh in range(2):
            h = 2 * sub + hh
            for u in range(nblk):
                nsa_qt_ref[u, 0:HEAD_DIM, h * Q_BLOCK:(h + 1) * Q_BLOCK] = \
                    vt[hh * HEAD_DIM:(hh + 1) * HEAD_DIM, u * Q_BLOCK:(u + 1) * Q_BLOCK]

    cmp_ref[...] = group("cmp_kv", 0)
    for name, kv_ref, vt_ref in (("sel_kv", sel_ref, selvt_ref), ("win_kv", win_ref, winvt_ref)):
        v = group(name, 0)
        kv_ref[...] = v.astype(BF16)
        vt = v.T.astype(BF16)
        for u in range(nblk):
            vt_ref[u] = vt[HEAD_DIM:2 * HEAD_DIM, u * Q_BLOCK:(u + 1) * Q_BLOCK]

    for name, o_ref in (("dil_q", dq_ref), ("dil_k", dk_ref), ("dil_v", dv_ref), ("fox_q", fq_ref),
                        ("fox_k", fk_ref)):
        for sub in range(o_ref.shape[1] // LANES):
            o_ref[:, sub * LANES:(sub + 1) * LANES] = group(name, sub).astype(BF16)

    for sub in range(FOX_W // LANES):
        vt = group("fox_v", sub).T.astype(BF16)
        for hh in range(2):
            for u in range(nblk):
                fvt_ref[2 * sub + hh, u] = vt[hh * HEAD_DIM:(hh + 1) * HEAD_DIM, u * Q_BLOCK:(u + 1) * Q_BLOCK]

    misc_ref[...] = group("misc", 0)


def _mixer_proj(h, mod, g, w, tabs, *, layer, batch, seq, tm=PROJ_TM):
    T, D = h.shape
    assert T % tm == 0 and seq % tm == 0 and tm % Q_BLOCK == 0
    tpb = seq // tm
    nblk = tm // Q_BLOCK
    nqb = seq // Q_BLOCK
    tab_spec = pl.BlockSpec((tm, 2 * LANES), lambda i: (i % tpb, 0))

    def flat(width):
        return pl.BlockSpec((tm, width), lambda i: (i, 0))

    vt_spec = pl.BlockSpec((None, nblk, HEAD_DIM, Q_BLOCK), lambda i: (i // tpb, i % tpb, 0, 0))
    out_specs = [
        pl.BlockSpec((None, nblk, 2 * HEAD_DIM, NSA_HEADS * Q_BLOCK), lambda i: (i // tpb, i % tpb, 0, 0)),
        flat(LANES), flat(LANES), vt_spec, flat(LANES), vt_spec,
        flat(DIL_W), flat(DIL_W), flat(DIL_W), flat(FOX_W), flat(FOX_W),
        pl.BlockSpec((None, FOX_HEADS, nblk, HEAD_DIM, Q_BLOCK), lambda i: (i // tpb, 0, i % tpb, 0, 0)),
        flat(LANES),
    ]
    vt_shape = jax.ShapeDtypeStruct((batch, nqb, HEAD_DIM, Q_BLOCK), BF16)
    out_shape = [
        jax.ShapeDtypeStruct((batch, nqb, 2 * HEAD_DIM, NSA_HEADS * Q_BLOCK), BF16),
        jax.ShapeDtypeStruct((T, LANES), F32),
        jax.ShapeDtypeStruct((T, LANES), BF16), vt_shape,
        jax.ShapeDtypeStruct((T, LANES), BF16), vt_shape,
        jax.ShapeDtypeStruct((T, DIL_W), BF16), jax.ShapeDtypeStruct((T, DIL_W), BF16),
        jax.ShapeDtypeStruct((T, DIL_W), BF16),
        jax.ShapeDtypeStruct((T, FOX_W), BF16), jax.ShapeDtypeStruct((T, FOX_W), BF16),
        jax.ShapeDtypeStruct((batch, FOX_HEADS, nqb, HEAD_DIM, Q_BLOCK), BF16),
        jax.ShapeDtypeStruct((T, LANES), F32),
    ]
    return pl.pallas_call(
        _proj_kernel,
        grid=(T // tm,),
        in_specs=[
            pl.BlockSpec((tm, D), lambda i: (i, 0)),
            pl.BlockSpec((None, 9, D), lambda i: (i // tpb, 0, 0)),
            pl.BlockSpec((1, D), lambda i: (0, 0)),
            pl.BlockSpec((None, D, PROJ_COLS), lambda i: (layer, 0, 0), pipeline_mode=pl.Buffered(1)),
            tab_spec, tab_spec, tab_spec,
        ],
        out_specs=out_specs,
        out_shape=out_shape,
        compiler_params=_params("arbitrary"),
        name="mixer_proj",
    )(h, mod, g, w, *tabs)


def _compress_kernel(x_ref, pet_ref, peb_ref, w1t_ref, w1b_ref, w2_ref, o_ref, vt_ref):
    x = x_ref[...]
    a = _dot((x + pet_ref[...]).astype(BF16), w1t_ref[...])
    b = _dot((x + peb_ref[...]).astype(BF16), w1b_ref[...])
    nrow = x.shape[0]
    hid = a + pltpu.roll(b, nrow - 1, 0)
    hid = hid * jax.nn.sigmoid(hid)
    out = _dot(hid.astype(BF16), w2_ref[...])
    o_ref[...] = out.astype(o_ref.dtype)
    vt_ref[...] = out.T[HEAD_DIM:2 * HEAD_DIM, :].astype(vt_ref.dtype)


def _compress(cmp_kv, pe_top, pe_bot, w1_top, w1_bot, w2):
    B, S, _ = cmp_kv.shape
    nchunk = S // NSA_CMP_STRIDE
    x = cmp_kv.reshape(B, nchunk, NSA_CMP_STRIDE * LANES)
    kw = NSA_CMP_STRIDE * LANES
    hw = 2 * NSA_CMP_HIDDEN
    return pl.pallas_call(
        _compress_kernel,
        grid=(B,),
        in_specs=[
            pl.BlockSpec((None, nchunk, kw), lambda b: (b, 0, 0)),
            pl.BlockSpec((1, kw), lambda b: (0, 0)),
            pl.BlockSpec((1, kw), lambda b: (0, 0)),
            pl.BlockSpec((kw, hw), lambda b: (0, 0)),
            pl.BlockSpec((kw, hw), lambda b: (0, 0)),
            pl.BlockSpec((hw, LANES), lambda b: (0, 0)),
        ],
        out_specs=[
            pl.BlockSpec((None, nchunk, LANES), lambda b: (b, 0, 0)),
            pl.BlockSpec((None, HEAD_DIM, nchunk), lambda b: (b, 0, 0)),
        ],
        out_shape=[
            jax.ShapeDtypeStruct((B, nchunk, LANES), BF16),
            jax.ShapeDtypeStruct((B, HEAD_DIM, nchunk), BF16),
        ],
        compiler_params=_params("arbitrary"),
        name="nsa_compress",
    )(x, pe_top, pe_bot, w1_top, w1_bot, w2)


def _nsa_kernel(qt_ref, cmp_ref, cmpvt_ref, sel_ref, selvt_ref, win_ref, winvt_ref, misc_ref, ovlt_ref,
                o_ref, member_ref, cnt_ref, sa_ref, sb_ref, m_ref, l_ref, acc_ref, *, seq, tk):
    H, Q = NSA_HEADS, Q_BLOCK
    HQ = H * Q
    q0 = pl.program_id(1) * Q
    qt = qt_ref[...]
    ncmp = cmp_ref.shape[0]
    n_sel = seq // NSA_SEL_LEN
    sel_shift = NSA_SEL_LEN.bit_length() - 1

    def heads(a):
        return jnp.concatenate([a] * H, axis=1)

    wlen = NSA_WINDOW + Q
    start = pl.multiple_of(jnp.maximum(q0 - NSA_WINDOW, 0), Q)
    s = _dot(win_ref[pl.ds(start, wlen), :], qt)
    kpos = start + lax.broadcasted_iota(jnp.int32, (wlen, Q), 0)
    tw = q0 + lax.broadcasted_iota(jnp.int32, (wlen, Q), 1)
    s = s + heads(jnp.where((kpos <= tw) & (kpos > tw - NSA_WINDOW), 0.0, NEG_INF))
    e = jnp.exp2(s - jnp.max(s, axis=0, keepdims=True))
    sblk = start // Q
    vt = jnp.concatenate([winvt_ref[sblk + u] for u in range(wlen // Q)], axis=1)
    o_win = _dot(vt, e.astype(BF16)) / jnp.sum(e, axis=0, keepdims=True)

    s = _dot(cmp_ref[...], qt)
    nn = lax.broadcasted_iota(jnp.int32, (ncmp, Q), 0)
    tt = q0 + lax.broadcasted_iota(jnp.int32, (ncmp, Q), 1)
    cmask = heads((nn * NSA_CMP_STRIDE + (NSA_CMP_LEN - 1) <= tt) & (nn < ncmp - 1))
    s = jnp.where(cmask, s, NEG_INF)
    m = jnp.max(s, axis=0, keepdims=True)
    e = jnp.where(cmask, jnp.exp2(s - m), 0.0)
    den = jnp.sum(e, axis=0, keepdims=True)
    p_cmp = e / jnp.where(den > 0, den, 1.0)
    o_cmp = _dot(cmpvt_ref[...], p_cmp.astype(BF16))

    psum = p_cmp[:, 0:Q] + p_cmp[:, Q:2 * Q] + p_cmp[:, 2 * Q:3 * Q] + p_cmp[:, 3 * Q:4 * Q]
    hi = psum.astype(BF16)
    lo = (psum - hi.astype(F32)).astype(BF16)
    ovlt = ovlt_ref[...]
    imp = _dot(ovlt, hi) + _dot(ovlt, lo)
    jj = lax.broadcasted_iota(jnp.int32, (n_sel, Q), 0)
    tq = q0 + lax.broadcasted_iota(jnp.int32, (n_sel, Q), 1)
    valid = jj * NSA_SEL_LEN <= tq
    forced = (jj == (tq >> sel_shift)) | (jj == 0)
    imp = jnp.where(forced, NSA_FORCE_SCORE, jnp.where(valid, imp, -1.0))
    bpt = tk // NSA_SEL_LEN
    vpt = tk // Q
    n_full = q0 // tk

    def sel_scores(k, dst_ref):
        dst_ref[...] = _dot(sel_ref[pl.ds(pl.multiple_of(k * tk, tk), tk), :], qt)

    sel_scores(0, sa_ref)
    m_ref[...] = jnp.full((1, HQ), NEG_INF, F32)
    l_ref[...] = jnp.zeros((1, HQ), F32)
    acc_ref[...] = jnp.zeros((HEAD_DIM, HQ), F32)

    ngrp = n_sel // SUBLANES
    grp = [imp[r * SUBLANES:(r + 1) * SUBLANES, :] for r in range(ngrp)]
    jrow = lax.broadcasted_iota(jnp.int32, (SUBLANES, Q), 0)
    cnt_ref[...] = jnp.zeros((n_sel, Q), F32)
    last_started = (q0 + Q - 1) >> sel_shift
    for ib in range(ngrp):
        @pl.when(ib * SUBLANES <= last_started)
        def _(ib=ib):
            cnt = [cnt_ref[r * SUBLANES:(r + 1) * SUBLANES, :] for r in range(ngrp)]
            for i in range(ib * SUBLANES, (ib + 1) * SUBLANES):
                row = jnp.broadcast_to(imp[i:i + 1, :], (SUBLANES, Q))
                for r in range(ngrp):
                    if r > ib:
                        hit = jnp.where(row >= grp[r], 1.0, 0.0)
                    elif r < ib:
                        hit = jnp.where(row > grp[r], 1.0, 0.0)
                    else:
                        hit = jnp.where(jrow + r * SUBLANES > i, jnp.where(row >= grp[r], 1.0, 0.0),
                                        jnp.where(row > grp[r], 1.0, 0.0))
                    cnt[r] = cnt[r] + hit
            for r in range(ngrp):
                cnt_ref[r * SUBLANES:(r + 1) * SUBLANES, :] = cnt[r]
    top_k = min(NSA_SEL_TOPK, n_sel)
    member_ref[...] = jnp.where(cnt_ref[...] < top_k, 0.0, NEG_INF)

    def sel_update(k, src_ref, causal):
        bias = jnp.concatenate(
            [jnp.broadcast_to(member_ref[pl.ds(k * bpt + jb, 1), :], (NSA_SEL_LEN, Q)) for jb in range(bpt)],
            axis=0)
        if causal:
            kpos = k * tk + lax.broadcasted_iota(jnp.int32, (tk, Q), 0)
            tcol = q0 + lax.broadcasted_iota(jnp.int32, (tk, Q), 1)
            bias = jnp.where(kpos <= tcol, bias, NEG_INF)
        s = src_ref[...] + heads(bias)
        m = m_ref[...]
        m_new = jnp.maximum(m, jnp.max(s, axis=0, keepdims=True))
        e = jnp.exp2(s - m_new)
        alpha = jnp.exp2(m - m_new)
        l_ref[...] = alpha * l_ref[...] + jnp.sum(e, axis=0, keepdims=True)
        vt = jnp.concatenate([selvt_ref[k * vpt + u] for u in range(vpt)], axis=1)
        acc_ref[...] = alpha * acc_ref[...] + _dot(vt, e.astype(BF16))
        m_ref[...] = m_new

    def sel_pair(j, _):
        sel_scores(2 * j + 1, sb_ref)
        sel_update(2 * j, sa_ref, False)
        sel_scores(2 * j + 2, sa_ref)
        sel_update(2 * j + 1, sb_ref, False)
        return 0

    lax.fori_loop(0, n_full // 2, sel_pair, 0)

    @pl.when(n_full % 2 == 1)
    def _():
        sel_scores(n_full, sb_ref)
        sel_update(n_full - 1, sa_ref, False)
        sel_update(n_full, sb_ref, True)

    @pl.when(n_full % 2 == 0)
    def _():
        sel_update(n_full, sa_ref, True)

    o_sel = acc_ref[...] / l_ref[...]

    g = jax.nn.sigmoid(misc_ref[...].T)
    outs = []
    for h in range(H):
        r0 = MISC_GATE_LANE + 3 * h
        cols = slice(h * Q, (h + 1) * Q)
        outs.append(g[r0:r0 + 1, :] * o_cmp[:, cols] + g[r0 + 1:r0 + 2, :] * o_sel[:, cols]
                    + g[r0 + 2:r0 + 3, :] * o_win[:, cols])
    o_ref[...] = jnp.concatenate(outs, axis=0).T.astype(o_ref.dtype)


def _nsa_attention(qt, cmp_out, cmp_vt, sel_kv, sel_vt, win_kv, win_vt, misc, ovlt, *, tk=512):
    B, S, _ = sel_kv.shape
    ncmp = cmp_out.shape[1]
    n_sel = S // NSA_SEL_LEN
    nqb = S // Q_BLOCK
    assert S % tk == 0 and tk % Q_BLOCK == 0 and S >= NSA_WINDOW + Q_BLOCK and n_sel % SUBLANES == 0
    kern = functools.partial(_nsa_kernel, seq=S, tk=tk)
    return pl.pallas_call(
        kern,
        grid=(B, nqb),
        in_specs=[
            pl.BlockSpec((None, None, 2 * HEAD_DIM, NSA_HEADS * Q_BLOCK), lambda b, i: (b, i, 0, 0)),
            pl.BlockSpec((None, ncmp, LANES), lambda b, i: (b, 0, 0)),
            pl.BlockSpec((None, HEAD_DIM, ncmp), lambda b, i: (b, 0, 0)),
            pl.BlockSpec((None, S, LANES), lambda b, i: (b, 0, 0)),
            pl.BlockSpec((None, nqb, HEAD_DIM, Q_BLOCK), lambda b, i: (b, 0, 0, 0)),
            pl.BlockSpec((None, S, LANES), lambda b, i: (b, 0, 0)),
            pl.BlockSpec((None, nqb, HEAD_DIM, Q_BLOCK), lambda b, i: (b, 0, 0, 0)),
            pl.BlockSpec((None, Q_BLOCK, LANES), lambda b, i: (b, i, 0)),
            pl.BlockSpec((n_sel, ncmp), lambda b, i: (0, 0)),
        ],
        out_specs=pl.BlockSpec((None, Q_BLOCK, NSA_Q_W), lambda b, i: (b, i, 0)),
        out_shape=jax.ShapeDtypeStruct((B, S, NSA_Q_W), BF16),
        scratch_shapes=[
            pltpu.VMEM((n_sel, Q_BLOCK), F32),
            pltpu.VMEM((n_sel, Q_BLOCK), F32),
            pltpu.VMEM((tk, NSA_HEADS * Q_BLOCK), F32),
            pltpu.VMEM((tk, NSA_HEADS * Q_BLOCK), F32),
            pltpu.VMEM((1, NSA_HEADS * Q_BLOCK), F32),
            pltpu.VMEM((1, NSA_HEADS * Q_BLOCK), F32),
            pltpu.VMEM((HEAD_DIM, NSA_HEADS * Q_BLOCK), F32),
        ],
        compiler_params=_params("arbitrary", "arbitrary"),
        name="nsa_attention",
    )(qt, cmp_out, cmp_vt, sel_kv, sel_vt, win_kv, win_vt, misc, ovlt)


def _dil_kernel(q_ref, k_ref, v_ref, o_ref, lse_ref, vt_ref, *, band):
    Q = Q_BLOCK
    nblk = q_ref.shape[0] // Q
    row = lax.broadcasted_iota(jnp.int32, (LANES, Q), 0)

    def transpose_v(j, _):
        vt_ref[j] = v_ref[pl.ds(pl.multiple_of(j * Q, Q), Q), :].astype(F32).T.astype(BF16)
        return 0

    lax.fori_loop(0, nblk, transpose_v, 0)

    def scores(jq):
        kb = jnp.maximum(jq - 1, 0)
        a0 = pl.multiple_of(jq * Q, Q)
        k0 = pl.multiple_of(kb * Q, Q)
        qt = q_ref[pl.ds(a0, Q), :].astype(F32).T
        rhs = jnp.concatenate([jnp.where(row < HEAD_DIM, qt, 0.0), jnp.where(row >= HEAD_DIM, qt, 0.0)],
                              axis=1).astype(BF16)
        return _dot(k_ref[pl.ds(k0, 2 * Q), :], rhs)

    def finish(jq, s):
        kb = jnp.maximum(jq - 1, 0)
        a0 = pl.multiple_of(jq * Q, Q)
        dist = (jq - kb) * Q + lax.broadcasted_iota(jnp.int32, (2 * Q, Q), 1) \
            - lax.broadcasted_iota(jnp.int32, (2 * Q, Q), 0)
        bias = jnp.where((dist >= 0) & (dist <= band), 0.0, NEG_INF)
        s = s + jnp.concatenate([bias, bias], axis=1)
        m = jnp.max(s, axis=0, keepdims=True)
        e = jnp.exp2(s - m)
        l = jnp.sum(e, axis=0, keepdims=True)
        eb = e.astype(BF16)
        vt = jnp.concatenate([vt_ref[kb], vt_ref[kb + 1]], axis=1)
        inv = 1.0 / l
        o0 = _dot(vt[0:HEAD_DIM], eb[:, 0:Q]) * inv[:, 0:Q]
        o1 = _dot(vt[HEAD_DIM:2 * HEAD_DIM], eb[:, Q:2 * Q]) * inv[:, Q:2 * Q]
        lse = m * (1.0 / LOG2E) + jnp.log(l)
        lt = jnp.concatenate([jnp.broadcast_to(lse[:, 0:Q], (HEAD_DIM, Q)),
                              jnp.broadcast_to(lse[:, Q:2 * Q], (HEAD_DIM, Q))], axis=0)
        o_ref[pl.ds(a0, Q), :] = jnp.concatenate([o0, o1], axis=0).T
        lse_ref[pl.ds(a0, Q), :] = lt.T

    per_step = min(4, nblk)

    def step(j, _):
        ss = [scores(per_step * j + u) for u in range(per_step)]
        for u in range(per_step):
            finish(per_step * j + u, ss[u])
        return 0

    lax.fori_loop(0, nblk // per_step, step, 0)


def _dilated_group(q, k, v, group, window, dil):
    B, S, W = q.shape
    n = S // dil
    nblk = n // Q_BLOCK
    assert dil & (dil - 1) == 0 and window // dil == Q_BLOCK and nblk >= 2 and nblk % min(4, nblk) == 0
    lane_blocks = W // LANES
    in_spec = pl.BlockSpec((None, n, LANES), lambda b, r: (b, 0, lane_blocks * r + group))
    out_spec = pl.BlockSpec((None, n, LANES), lambda b, r: (b, 0, r))
    out_sds = jax.ShapeDtypeStruct((B, n, dil * LANES), F32)
    o, lse = pl.pallas_call(
        functools.partial(_dil_kernel, band=window // dil),
        grid=(B, dil),
        in_specs=[in_spec, in_spec, in_spec],
        out_specs=[out_spec, out_spec],
        out_shape=[out_sds, out_sds],
        scratch_shapes=[pltpu.VMEM((n // Q_BLOCK, LANES, Q_BLOCK), BF16)],
        compiler_params=_params("arbitrary", "arbitrary"),
        name=f"dilated_attention_d{dil}",
    )(q.reshape(B, n, dil * W), k.reshape(B, n, dil * W), v.reshape(B, n, dil * W))
    return o.reshape(B * S, LANES), lse.reshape(B * S, LANES)


def _dilated_attention(q, k, v):
    outs = [_dilated_group(q, k, v, g, window, dil) for g, (window, dil) in enumerate(DIL_PATTERNS)]
    return [o for o, _ in outs] + [l for _, l in outs]


def _fox_prep_kernel(misc_ref, bias_ref, q_ref, k_ref, kaug_ref, qtaug_ref, carry_ref):
    tb = misc_ref.shape[0]

    @pl.when(pl.program_id(1) == 0)
    def _():
        carry_ref[...] = jnp.zeros_like(carry_ref)

    r = lax.broadcasted_iota(jnp.int32, (LANES, LANES), 0)
    c = lax.broadcasted_iota(jnp.int32, (LANES, LANES), 1)
    tri = jnp.where(r >= c, 1.0, 0.0).astype(F32)
    carry = carry_ref[...]
    cums = []
    for blk in range(tb // LANES):
        x = misc_ref[blk * LANES:(blk + 1) * LANES, :] + bias_ref[...]
        log_f = -(jnp.maximum(-x, 0.0) + jnp.log1p(jnp.exp(-jnp.abs(x))))
        cs = jnp.dot(tri, log_f, preferred_element_type=F32, precision=lax.Precision.HIGHEST) + carry
        cums.append(cs)
        carry = cs[LANES - 1:LANES, :]
    carry_ref[...] = carry
    cum = jnp.concatenate(cums, axis=0) * LOG2E

    lane = lax.broadcasted_iota(jnp.int32, (tb, LANES), 1)
    a0 = HEAD_DIM
    for h in range(FOX_HEADS):
        cb = jnp.broadcast_to(cum[:, h:h + 1], (tb, LANES))
        hi = cb.astype(BF16).astype(F32)
        r1 = cb - hi
        mid = r1.astype(BF16).astype(F32)
        lo = r1 - mid
        pieces = jnp.where(lane == a0, hi, jnp.where(lane == a0 + 1, mid, lo))
        g0 = (h // 2) * LANES
        qg = q_ref[:, g0:g0 + LANES].astype(F32)
        kg = k_ref[:, g0:g0 + LANES].astype(F32)
        if h % 2 == 1:
            qg = pltpu.roll(qg, HEAD_DIM, 1)
            kg = pltpu.roll(kg, HEAD_DIM, 1)
        in_c = (lane >= a0) & (lane < a0 + FOX_AUG)
        in_1 = (lane >= a0 + FOX_AUG) & (lane < a0 + 2 * FOX_AUG)
        q_aug = jnp.where(lane < a0, qg, jnp.where(in_c, pieces, jnp.where(in_1, 1.0, 0.0)))
        k_neg = -pltpu.roll(pieces, FOX_AUG, 1)
        k_aug = jnp.where(lane < a0, kg, jnp.where(in_c, 1.0, jnp.where(in_1, k_neg, 0.0)))
        kaug_ref[h] = k_aug.astype(BF16)
        qtaug_ref[h] = q_aug.T.astype(BF16)


def _fox_prep(misc, bias_row, q, k, *, tb=512):
    B, S, _ = misc.shape
    assert S % tb == 0
    return pl.pallas_call(
        _fox_prep_kernel,
        grid=(B, S // tb),
        in_specs=[
            pl.BlockSpec((None, tb, LANES), lambda b, i: (b, i, 0)),
            pl.BlockSpec((1, LANES), lambda b, i: (0, 0)),
            pl.BlockSpec((None, tb, FOX_W), lambda b, i: (b, i, 0)),
            pl.BlockSpec((None, tb, FOX_W), lambda b, i: (b, i, 0)),
        ],
        out_specs=[
            pl.BlockSpec((None, FOX_HEADS, tb, LANES), lambda b, i: (b, 0, i, 0)),
            pl.BlockSpec((None, FOX_HEADS, LANES, tb), lambda b, i: (b, 0, 0, i)),
        ],
        out_shape=[
            jax.ShapeDtypeStruct((B, FOX_HEADS, S, LANES), BF16),
            jax.ShapeDtypeStruct((B, FOX_HEADS, LANES, S), BF16),
        ],
        scratch_shapes=[pltpu.VMEM((1, LANES), F32)],
        compiler_params=_params("arbitrary", "arbitrary"),
        name="fox_prep",
    )(misc, bias_row, q, k)


def _fox_kernel(qt_ref, k_ref, vt_ref, o_ref, sa_ref, sb_ref, m_ref, l_ref, acc_ref, *, tq, tk):
    nh = qt_ref.shape[0]
    q0 = pl.program_id(2) * tq
    n_full = q0 // tk
    vpt = tk // Q_BLOCK

    def scores(k, dst_ref):
        base = pl.multiple_of(k * tk, tk)
        for h in range(nh):
            dst_ref[h] = _dot(k_ref[h, pl.ds(base, tk), :], qt_ref[h])

    def update(k, src_ref, causal):
        for h in range(nh):
            s = src_ref[h]
            if causal:
                kpos = k * tk + lax.broadcasted_iota(jnp.int32, (tk, tq), 0)
                tcol = q0 + lax.broadcasted_iota(jnp.int32, (tk, tq), 1)
                s = jnp.where(kpos <= tcol, s, NEG_INF)
            m = m_ref[h]
            m_new = jnp.maximum(m, jnp.max(s, axis=0, keepdims=True))
            e = jnp.exp2(s - m_new)
            alpha = jnp.exp2(m - m_new)
            l_ref[h] = alpha * l_ref[h] + jnp.sum(e, axis=0, keepdims=True)
            vt = jnp.concatenate([vt_ref[h, k * vpt + u] for u in range(vpt)], axis=1)
            acc_ref[h] = alpha * acc_ref[h] + _dot(vt, e.astype(BF16))
            m_ref[h] = m_new

    scores(0, sa_ref)
    m_ref[...] = jnp.full(m_ref.shape, NEG_INF, F32)
    l_ref[...] = jnp.zeros(l_ref.shape, F32)
    acc_ref[...] = jnp.zeros(acc_ref.shape, F32)

    def pair(j, _):
        scores(2 * j + 1, sb_ref)
        update(2 * j, sa_ref, False)
        scores(2 * j + 2, sa_ref)
        update(2 * j + 1, sb_ref, False)
        return 0

    lax.fori_loop(0, n_full // 2, pair, 0)

    @pl.when(n_full % 2 == 1)
    def _():
        scores(n_full, sb_ref)
        update(n_full - 1, sa_ref, False)
        update(n_full, sb_ref, True)

    @pl.when(n_full % 2 == 0)
    def _():
        update(n_full, sa_ref, True)

    outs = [acc_ref[h] / l_ref[h] for h in range(nh)]
    o_ref[...] = jnp.concatenate(outs, axis=0).T.astype(o_ref.dtype)


def _fox_attention(qt_aug, k_aug, vt, *, tq=256, tk=512, heads_per_step=FOX_HEADS):
    B, H, S, _ = k_aug.shape
    hps = heads_per_step
    assert S % tk == 0 and tk % tq == 0 and tk % Q_BLOCK == 0 and H % hps == 0 and (hps * HEAD_DIM) % LANES == 0
    nqb = S // Q_BLOCK
    kern = functools.partial(_fox_kernel, tq=tq, tk=tk)
    return pl.pallas_call(
        kern,
        grid=(B, H // hps, S // tq),
        in_specs=[
            pl.BlockSpec((None, hps, LANES, tq), lambda b, p, i: (b, p, 0, i)),
            pl.BlockSpec((None, hps, S, LANES), lambda b, p, i: (b, p, 0, 0)),
            pl.BlockSpec((None, hps, nqb, HEAD_DIM, Q_BLOCK), lambda b, p, i: (b, p, 0, 0, 0)),
        ],
        out_specs=pl.BlockSpec((None, tq, hps * HEAD_DIM), lambda b, p, i: (b, i, p)),
        out_shape=jax.ShapeDtypeStruct((B, S, H * HEAD_DIM), BF16),
        scratch_shapes=[
            pltpu.VMEM((hps, tk, tq), F32),
            pltpu.VMEM((hps, tk, tq), F32),
            pltpu.VMEM((hps, 1, tq), F32),
            pltpu.VMEM((hps, 1, tq), F32),
            pltpu.VMEM((hps, HEAD_DIM, tq), F32),
        ],
        compiler_params=_params("arbitrary", "arbitrary", "arbitrary"),
        name="fox_attention",
    )(qt_aug, k_aug, vt)


def _merge_kernel(x_ref, mod_ref, g_ref, ya_ref, yc_ref, d0_ref, d1_ref, d2_ref, l0_ref, l1_ref, l2_ref,
                  wg_ref, bra_ref, brb_ref, brc_ref, wo_ref, o_ref):
    x = x_ref[...]
    D = x.shape[1]
    n = _norm_modulate(x, g_ref[...], mod_ref[3:4, :], mod_ref[4:5, :]).astype(BF16)
    lse = [l0_ref[...], l1_ref[...], l2_ref[...]]
    mx = jnp.maximum(jnp.maximum(lse[0], lse[1]), lse[2])
    w = [jnp.exp(l - mx) for l in lse]
    wsum = w[0] + w[1] + w[2]
    yb = (w[0] / wsum) * d0_ref[...] + (w[1] / wsum) * d1_ref[...] + (w[2] / wsum) * d2_ref[...]
    merged = jax.nn.sigmoid(_dot(n, wg_ref[:, 0:D])) * _dot(ya_ref[...], bra_ref[...])
    merged = merged + jax.nn.sigmoid(_dot(n, wg_ref[:, D:2 * D])) * _dot(yb.astype(BF16), brb_ref[...])
    merged = merged + jax.nn.sigmoid(_dot(n, wg_ref[:, 2 * D:3 * D])) * _dot(yc_ref[...], brc_ref[...])
    o_ref[...] = x + mod_ref[5:6, :] * _dot(merged.astype(BF16), wo_ref[...])


def _merge(h, mod, g, ya, yc, dil_parts, w_gate, br_a, br_b, br_c, w_out, *, layer, seq, tm=PROJ_TM):
    T, D = h.shape
    assert T % tm == 0 and seq % tm == 0 and len(dil_parts) == 2 * len(DIL_PATTERNS)
    tpb = seq // tm

    def resident(shape):
        return pl.BlockSpec((None,) + shape, lambda i: (layer, 0, 0), pipeline_mode=pl.Buffered(1))

    dil_spec = pl.BlockSpec((tm, DIL_OUT_W), lambda i: (i, 0))
    return pl.pallas_call(
        _merge_kernel,
        grid=(T // tm,),
        in_specs=[
            pl.BlockSpec((tm, D), lambda i: (i, 0)),
            pl.BlockSpec((None, 9, D), lambda i: (i // tpb, 0, 0)),
            pl.BlockSpec((1, D), lambda i: (0, 0)),
            pl.BlockSpec((tm, NSA_Q_W), lambda i: (i, 0)),
            pl.BlockSpec((tm, FOX_W), lambda i: (i, 0)),
            dil_spec, dil_spec, dil_spec, dil_spec, dil_spec, dil_spec,
            resident((D, 3 * D)),
            resident((NSA_Q_W, D)),
            resident((DIL_OUT_W, D)),
            resident((FOX_W, D)),
            resident((D, D)),
        ],
        out_specs=pl.BlockSpec((tm, D), lambda i: (i, 0)),
        out_shape=jax.ShapeDtypeStruct((T, D), F32),
        compiler_params=_params("arbitrary"),
        name="merge_out",
    )(h, mod, g, ya, yc, *dil_parts, w_gate, br_a, br_b, br_c, w_out)


def _rope_tables(seq):
    inv_freq = ROPE_THETA ** (-jnp.arange(0, ROPE_DIM, 2, dtype=F32) / ROPE_DIM)
    ang = jnp.arange(seq, dtype=F32)[:, None] * inv_freq[None, :]
    cos, sin = jnp.cos(ang), jnp.sin(ang)
    d = np.arange(LANES) % HEAD_DIM
    idx = d % ROPE_HALF
    first = jnp.asarray(d < ROPE_HALF)
    second = jnp.asarray((d >= ROPE_HALF) & (d < ROPE_DIM))
    c_a = jnp.where(first | second, cos[:, idx], 1.0)
    s1_a = jnp.where(first, -sin[:, idx], 0.0)
    s2_a = jnp.where(second, sin[:, idx], 0.0)
    head0 = jnp.asarray(np.arange(LANES) < HEAD_DIM)
    c_b = jnp.where(head0, c_a, 1.0)
    s1_b = jnp.where(head0, s1_a, 0.0)
    s2_b = jnp.where(head0, s2_a, 0.0)
    return (jnp.concatenate([c_a, c_b], axis=1), jnp.concatenate([s1_a, s1_b], axis=1),
            jnp.concatenate([s2_a, s2_b], axis=1))


def _pack_mix_kernel(w_ref, proj_ref, gate_ref):
    rows = w_ref.shape[0]
    head = NSA_Q_W + 6 * HEAD_DIM
    body = 3 * DIL_W + 3 * FOX_W
    g0 = head
    b0 = g0 + 3 * NSA_HEADS
    f0 = b0 + body
    m0 = f0 + FOX_HEADS
    proj_ref[:, 0:head] = w_ref[:, 0:head].astype(BF16)
    proj_ref[:, head:head + body] = w_ref[:, b0:b0 + body].astype(BF16)
    misc = jnp.concatenate([
        w_ref[:, f0:f0 + FOX_HEADS], jnp.zeros((rows, MISC_GATE_LANE - MISC_FOX_LANE - FOX_HEADS), F32),
        w_ref[:, g0:g0 + 3 * NSA_HEADS], jnp.zeros((rows, LANES - MISC_GATE_LANE - 3 * NSA_HEADS), F32)], axis=1)
    proj_ref[:, head + body:head + body + LANES] = misc.astype(BF16)
    gate_ref[...] = w_ref[:, m0:m0 + gate_ref.shape[1]].astype(BF16)


def _pack_mix_w_in(w, *, tr=256):
    L, D, C = w.shape
    gate_cols = C - (PROJ_COLS - LANES) - FOX_HEADS - 3 * NSA_HEADS
    assert D % tr == 0 and gate_cols % LANES == 0
    return pl.pallas_call(
        _pack_mix_kernel,
        grid=(L, D // tr),
        in_specs=[pl.BlockSpec((None, tr, C), lambda l, i: (l, i, 0))],
        out_specs=[pl.BlockSpec((None, tr, PROJ_COLS), lambda l, i: (l, i, 0)),
                   pl.BlockSpec((None, tr, gate_cols), lambda l, i: (l, i, 0))],
        out_shape=[jax.ShapeDtypeStruct((L, D, PROJ_COLS), BF16), jax.ShapeDtypeStruct((L, D, gate_cols), BF16)],
        compiler_params=_params("arbitrary", "arbitrary"),
        name="pack_mix_w_in",
    )(w)


def _pack_compress(pe, w1, w2):
    half = NSA_CMP_STRIDE
    hid = NSA_CMP_HIDDEN
    w1k = w1[0].reshape(NSA_CMP_LEN, HEAD_DIM, hid)
    w1v = w1[1].reshape(NSA_CMP_LEN, HEAD_DIM, hid)
    z = jnp.zeros((half, HEAD_DIM, hid), w1.dtype)

    def halfpack(lo):
        kk = jnp.concatenate([w1k[lo:lo + half], z], axis=-1)
        vv = jnp.concatenate([z, w1v[lo:lo + half]], axis=-1)
        return jnp.concatenate([kk, vv], axis=1).reshape(half * LANES, 2 * hid).astype(BF16)

    def pepack(lo):
        return jnp.concatenate([pe[0, lo:lo + half], pe[1, lo:lo + half]], axis=-1).reshape(1, half * LANES)

    zz = jnp.zeros((hid, HEAD_DIM), w2.dtype)
    w2p = jnp.concatenate([jnp.concatenate([w2[0], zz], axis=1),
                           jnp.concatenate([zz, w2[1]], axis=1)], axis=0).astype(BF16)
    return pepack(0), pepack(half), halfpack(0), halfpack(half), w2p


def _overlap_matrix_t(seq):
    n_chunk = seq // NSA_CMP_STRIDE
    n_cmp = (seq - NSA_CMP_LEN) // NSA_CMP_STRIDE + 1
    n_sel = seq // NSA_SEL_LEN
    cmp_start = np.arange(n_cmp) * NSA_CMP_STRIDE
    sel_start = np.arange(n_sel) * NSA_SEL_LEN
    ov = np.minimum(cmp_start[:, None] + NSA_CMP_LEN, sel_start[None, :] + NSA_SEL_LEN) \
        - np.maximum(cmp_start[:, None], sel_start[None, :])
    full = np.zeros((n_sel, n_chunk), np.float32)
    full[:, :n_cmp] = (np.clip(ov, 0, None) / NSA_CMP_LEN).T
    return jnp.asarray(full, dtype=BF16)


def kernel(x, c, ada_w, ada_b, norm_g, final_norm_g, ffn_w_in, ffn_w_out, mix_w_in, nsa_cmp_pe, nsa_cmp_w1,
           nsa_cmp_w2, fox_f_bias, br_w_nsa, br_w_dil, br_w_fox, mix_w_out):
    B, S, D = x.shape
    L = ada_w.shape[0]
    T = B * S
    mod_all = _ada_modulation(c, ada_w, ada_b).reshape(L, B, 9, D)
    tabs = _rope_tables(S)
    ovlt = _overlap_matrix_t(S)
    fg = final_norm_g.reshape(1, D)
    ffn_in, ffn_out = ffn_w_in.astype(BF16), ffn_w_out.astype(BF16)
    w_proj, w_gate = _pack_mix_w_in(mix_w_in)
    br_a, br_b, br_c = br_w_nsa.astype(BF16), br_w_dil.astype(BF16), br_w_fox.astype(BF16)
    w_mix_out = mix_w_out.astype(BF16)
    h = x.reshape(T, D)
    for l in range(L):
        mod = mod_all[l]
        h = _ffn(h, mod, norm_g[l, 0].reshape(1, D), ffn_in, ffn_out, fg,
                 layer=l, which=0, mod_base=0, final=False, seq=S)

        g1 = norm_g[l, 1].reshape(1, D)
        (nsa_qt, cmp_kv, sel_kv, sel_vt, win_kv, win_vt, dil_q, dil_k, dil_v, fox_q, fox_k, fox_vt,
         misc) = _mixer_proj(h, mod, g1, w_proj, tabs, layer=l, batch=B, seq=S)

        def bsd(a):
            return a.reshape(B, S, a.shape[-1])

        misc = bsd(misc)
        cmp_out, cmp_vt = _compress(bsd(cmp_kv), *_pack_compress(nsa_cmp_pe[l], nsa_cmp_w1[l], nsa_cmp_w2[l]))
        y_a = _nsa_attention(nsa_qt, cmp_out, cmp_vt, bsd(sel_kv), sel_vt, bsd(win_kv), win_vt, misc, ovlt)
        dil_parts = _dilated_attention(bsd(dil_q), bsd(dil_k), bsd(dil_v))
        bias_row = jnp.pad(fox_f_bias[l].reshape(1, FOX_HEADS),
                           ((0, 0), (MISC_FOX_LANE, LANES - MISC_FOX_LANE - FOX_HEADS)))
        k_aug, qt_aug = _fox_prep(misc, bias_row, bsd(fox_q), bsd(fox_k))
        y_c = _fox_attention(qt_aug, k_aug, fox_vt)

        h = _merge(h, mod, g1, y_a.reshape(T, NSA_Q_W), y_c.reshape(T, FOX_W), dil_parts,
                   w_gate, br_a, br_b, br_c, w_mix_out, layer=l, seq=S)

        h = _ffn(h, mod, norm_g[l, 2].reshape(1, D), ffn_in, ffn_out, fg,
                 layer=l, which=1, mod_base=6, final=(l == L - 1), seq=S)
    return h.reshape(B, S, D)
```

```python
import functools
import math

import numpy as np
import jax
import jax.numpy as jnp
from jax import lax
from jax.experimental import pallas as pl
from jax.experimental.pallas import tpu as pltpu

F32 = jnp.float32
BF16 = jnp.bfloat16

HEAD_DIM = 64
ROPE_DIM = 16
ROPE_HALF = ROPE_DIM // 2
ROPE_THETA = 500000.0
Q_BLOCK = 128
NEG_INF = -1e30
RMS_EPS = 1e-6
QK_SCALE = HEAD_DIM ** -0.5
LOG2E = math.log2(math.e)

NSA_HEADS = 4
NSA_CMP_LEN = 32
NSA_CMP_STRIDE = 16
NSA_CMP_HIDDEN = 128
NSA_SEL_LEN = 64
NSA_SEL_TOPK = 16
NSA_WINDOW = 512
NSA_FORCE_SCORE = 1e4

DIL_PATTERNS = ((128, 1), (512, 4), (2048, 16))
DIL_HEADS_PER_GROUP = 2
DIL_HEADS = DIL_HEADS_PER_GROUP * len(DIL_PATTERNS)
FOX_HEADS = 6

NSA_Q_W = NSA_HEADS * HEAD_DIM
DIL_W = DIL_HEADS * HEAD_DIM
DIL_OUT_W = DIL_HEADS_PER_GROUP * HEAD_DIM
FOX_W = FOX_HEADS * HEAD_DIM

LANES = 128
SUBLANES = 8
MISC_FOX_LANE = 0
MISC_GATE_LANE = 8
PROJ_TM = 512
FOX_AUG = 3
DIL_BLOCKS_PER_STEP = 8

PROJ_GROUPS = (
    ("nsa_q", NSA_Q_W, "A", QK_SCALE * LOG2E),
    ("cmp_kv", LANES, "B", 1.0),
    ("sel_kv", LANES, "B", 1.0),
    ("win_kv", LANES, "B", 1.0),
    ("dil_q", DIL_W, "A", QK_SCALE * LOG2E),
    ("dil_k", DIL_W, "A", 1.0),
    ("dil_v", DIL_W, None, 1.0),
    ("fox_q", FOX_W, None, QK_SCALE * LOG2E),
    ("fox_k", FOX_W, None, 1.0),
    ("fox_v", FOX_W, None, 1.0),
    ("misc", LANES, None, 1.0),
)
PROJ_COLS = sum(g[1] for g in PROJ_GROUPS)

VMEM_LIMIT = 56 * 1024 * 1024


def _dot(a, b):
    return jnp.dot(a, b, preferred_element_type=F32)


def _dot_nt(a, b):
    return lax.dot_general(a, b, (((1,), (1,)), ((), ())), preferred_element_type=F32)


def _norm_modulate(x, g, shift, scale):
    ms = jnp.mean(x * x, axis=-1, keepdims=True)
    y = x * lax.rsqrt(ms + RMS_EPS) * g
    return y * (1.0 + scale) + shift


def _masked_softmax(s, mask):
    s = jnp.where(mask, s, NEG_INF)
    m = jnp.max(s, axis=-1, keepdims=True)
    e = jnp.where(mask, jnp.exp(s - m), 0.0)
    den = jnp.sum(e, axis=-1, keepdims=True)
    den = jnp.where(den > 0, den, 1.0)
    return e / den, m + jnp.log(den)


def _params(*sem):
    return pltpu.CompilerParams(dimension_semantics=sem, vmem_limit_bytes=VMEM_LIMIT)


def _ada_kernel(c_ref, w_ref, b_ref, o_ref):
    c = c_ref[...]
    cond = c * jax.nn.sigmoid(c)
    o_ref[...] = jnp.dot(cond, w_ref[...], preferred_element_type=F32,
                         precision=lax.Precision.HIGHEST) + b_ref[...]


def _ada_modulation(c, ada_w, ada_b):
    L, D, N = ada_w.shape
    B = c.shape[0]
    tn = 1152
    assert N % tn == 0
    return pl.pallas_call(
        _ada_kernel,
        grid=(L, N // tn),
        in_specs=[
            pl.BlockSpec((B, D), lambda l, j: (0, 0)),
            pl.BlockSpec((None, D, tn), lambda l, j: (l, 0, j)),
            pl.BlockSpec((None, 1, tn), lambda l, j: (l, 0, j)),
        ],
        out_specs=pl.BlockSpec((None, B, tn), lambda l, j: (l, 0, j)),
        out_shape=jax.ShapeDtypeStruct((L, B, N), F32),
        compiler_params=_params("arbitrary", "arbitrary"),
        name="ada_modulation",
    )(c, ada_w, ada_b.reshape(L, 1, N))


def _ffn_kernel(x_ref, mod_ref, g_ref, win_ref, wout_ref, fg_ref, o_ref, a_ref, *, mod_base, d_ff, chunk, final):
    x = x_ref[...]
    n = _norm_modulate(x, g_ref[...], mod_ref[mod_base:mod_base + 1, :],
                       mod_ref[mod_base + 1:mod_base + 2, :]).astype(BF16)
    for j in range(d_ff // chunk):
        gate = _dot(n, win_ref[:, j * chunk:(j + 1) * chunk])
        up = _dot(n, win_ref[:, d_ff + j * chunk:d_ff + (j + 1) * chunk])
        a_ref[:, j * chunk:(j + 1) * chunk] = (gate * jax.nn.sigmoid(gate) * up).astype(BF16)
    f = _dot(a_ref[...], wout_ref[...])
    out = x + (0.5 * mod_ref[mod_base + 2:mod_base + 3, :]) * f
    if final:
        ms = jnp.mean(out * out, axis=-1, keepdims=True)
        out = out * lax.rsqrt(ms + RMS_EPS) * fg_ref[...]
    o_ref[...] = out


def _ffn(h, mod, g, w_in, w_out, final_g, *, layer, which, mod_base, final, seq, tm=PROJ_TM):
    T, D = h.shape
    d_ff = w_out.shape[2]
    chunk = 256
    assert T % tm == 0 and seq % tm == 0 and d_ff % chunk == 0
    tpb = seq // tm
    kern = functools.partial(_ffn_kernel, mod_base=mod_base, d_ff=d_ff, chunk=chunk, final=final)
    return pl.pallas_call(
        kern,
        grid=(T // tm,),
        in_specs=[
            pl.BlockSpec((tm, D), lambda i: (i, 0)),
            pl.BlockSpec((None, 9, D), lambda i: (i // tpb, 0, 0)),
            pl.BlockSpec((1, D), lambda i: (0, 0)),
            pl.BlockSpec((None, None, D, 2 * d_ff), lambda i: (layer, which, 0, 0), pipeline_mode=pl.Buffered(1)),
            pl.BlockSpec((None, None, d_ff, D), lambda i: (layer, which, 0, 0), pipeline_mode=pl.Buffered(1)),
            pl.BlockSpec((1, D), lambda i: (0, 0)),
        ],
        out_specs=pl.BlockSpec((tm, D), lambda i: (i, 0)),
        out_shape=jax.ShapeDtypeStruct((T, D), F32),
        scratch_shapes=[pltpu.VMEM((tm, d_ff), BF16)],
        compiler_params=_params("arbitrary"),
        name="ffn",
    )(h, mod, g, w_in, w_out, final_g)


def _rope_group(v, c, s1, s2):
    return v * c + pltpu.roll(v, LANES - ROPE_HALF, 1) * s1 + pltpu.roll(v, ROPE_HALF, 1) * s2


def _proj_kernel(x_ref, mod_ref, g_ref, w_ref, tc_ref, ts1_ref, ts2_ref,
                 nsa_qt_ref, cmp_ref, sel_ref, selvt_ref, win_ref, winvt_ref,
                 dq_ref, dk_ref, dv_ref, fq_ref, fk_ref, fvt_ref, misc_ref):
    x = x_ref[...]
    tm = x.shape[0]
    nblk = tm // Q_BLOCK
    n = _norm_modulate(x, g_ref[...], mod_ref[3:4, :], mod_ref[4:5, :]).astype(BF16)

    wide = {}

    def group(name, sub):
        off = 0
        for gname, width, rope, scale in PROJ_GROUPS:
            if gname == name:
                break
            off += width
        lo = off + sub * LANES
        blk = lo // (2 * LANES)
        if blk not in wide:
            wide[blk] = _dot(n, w_ref[:, blk * 2 * LANES:(blk + 1) * 2 * LANES])
        v = wide[blk][:, lo % (2 * LANES):lo % (2 * LANES) + LANES]
        if rope is not None:
            t0 = 0 if rope == "A" else LANES
            v = _rope_group(v, tc_ref[:, t0:t0 + LANES], ts1_ref[:, t0:t0 + LANES], ts2_ref[:, t0:t0 + LANES])
        if scale != 1.0:
            v = v * scale
        return v

    zero_half = jnp.zeros((HEAD_DIM, NSA_HEADS * Q_BLOCK), BF16)
    for u in range(nblk):
        nsa_qt_ref[u, HEAD_DIM:2 * HEAD_DIM, :] = zero_half
    for sub in range(NSA_Q_W // LANES):
        vt = group("nsa_q", sub).T.astype(BF16)
        for hh in range(2):
            h = 2 * sub + hh
            for u in range(nblk):
                nsa_qt_ref[u, 0:HEAD_DIM, h * Q_BLOCK:(h + 1) * Q_BLOCK] = \
                    vt[hh * HEAD_DIM:(hh + 1) * HEAD_DIM, u * Q_BLOCK:(u + 1) * Q_BLOCK]

    cmp_ref[...] = group("cmp_kv", 0)
    for name, kv_ref, vt_ref in (("sel_kv", sel_ref, selvt_ref), ("win_kv", win_ref, winvt_ref)):
        v = group(name, 0)
        kv_ref[...] = v.astype(BF16)
        vt = v.T.astype(BF16)
        for u in range(nblk):
            vt_ref[u] = vt[HEAD_DIM:2 * HEAD_DIM, u * Q_BLOCK:(u + 1) * Q_BLOCK]

    for name, o_ref in (("dil_q", dq_ref), ("dil_k", dk_ref), ("dil_v", dv_ref), ("fox_q", fq_ref),
                        ("fox_k", fk_ref)):
        for sub in range(o_ref.shape[1] // LANES):
            o_ref[:, sub * LANES:(sub + 1) * LANES] = group(name, sub).astype(o_ref.dtype)

    for sub in range(FOX_W // LANES):
        vt = group("fox_v", sub).T.astype(BF16)
        for hh in range(2):
            for u in range(nblk):
                fvt_ref[2 * sub + hh, u] = vt[hh * HEAD_DIM:(hh + 1) * HEAD_DIM, u * Q_BLOCK:(u + 1) * Q_BLOCK]

    misc_ref[...] = group("misc", 0)


def _mixer_proj(h, mod, g, w, tabs, *, layer, batch, seq, tm=PROJ_TM):
    T, D = h.shape
    assert T % tm == 0 and seq % tm == 0 and tm % Q_BLOCK == 0
    tpb = seq // tm
    nblk = tm // Q_BLOCK
    nqb = seq // Q_BLOCK
    tab_spec = pl.BlockSpec((tm, 2 * LANES), lambda i: (i % tpb, 0))

    def flat(width):
        return pl.BlockSpec((tm, width), lambda i: (i, 0))

    vt_spec = pl.BlockSpec((None, nblk, HEAD_DIM, Q_BLOCK), lambda i: (i // tpb, i % tpb, 0, 0))
    out_specs = [
        pl.BlockSpec((None, nblk, 2 * HEAD_DIM, NSA_HEADS * Q_BLOCK), lambda i: (i // tpb, i % tpb, 0, 0)),
        flat(LANES), flat(LANES), vt_spec, flat(LANES), vt_spec,
        flat(DIL_W), flat(DIL_W), flat(DIL_W), flat(FOX_W), flat(FOX_W),
        pl.BlockSpec((None, FOX_HEADS, nblk, HEAD_DIM, Q_BLOCK), lambda i: (i // tpb, 0, i % tpb, 0, 0)),
        flat(LANES),
    ]
    vt_shape = jax.ShapeDtypeStruct((batch, nqb, HEAD_DIM, Q_BLOCK), BF16)
    out_shape = [
        jax.ShapeDtypeStruct((batch, nqb, 2 * HEAD_DIM, NSA_HEADS * Q_BLOCK), BF16),
        jax.ShapeDtypeStruct((T, LANES), F32),
        jax.ShapeDtypeStruct((T, LANES), BF16), vt_shape,
        jax.ShapeDtypeStruct((T, LANES), BF16), vt_shape,
        jax.ShapeDtypeStruct((T, DIL_W), F32), jax.ShapeDtypeStruct((T, DIL_W), F32),
        jax.ShapeDtypeStruct((T, DIL_W), F32),
        jax.ShapeDtypeStruct((T, FOX_W), BF16), jax.ShapeDtypeStruct((T, FOX_W), BF16),
        jax.ShapeDtypeStruct((batch, FOX_HEADS, nqb, HEAD_DIM, Q_BLOCK), BF16),
        jax.ShapeDtypeStruct((T, LANES), F32),
    ]
    return pl.pallas_call(
        _proj_kernel,
        grid=(T // tm,),
        in_specs=[
            pl.BlockSpec((tm, D), lambda i: (i, 0)),
            pl.BlockSpec((None, 9, D), lambda i: (i // tpb, 0, 0)),
            pl.BlockSpec((1, D), lambda i: (0, 0)),
            pl.BlockSpec((None, D, PROJ_COLS), lambda i: (layer, 0, 0), pipeline_mode=pl.Buffered(1)),
            tab_spec, tab_spec, tab_spec,
        ],
        out_specs=out_specs,
        out_shape=out_shape,
        compiler_params=_params("arbitrary"),
        name="mixer_proj",
    )(h, mod, g, w, *tabs)


def _compress_kernel(x_ref, pet_ref, peb_ref, w1t_ref, w1b_ref, w2_ref, o_ref, vt_ref):
    nrow = x_ref.shape[0] // NSA_CMP_STRIDE
    a = jnp.zeros((nrow, 2 * NSA_CMP_HIDDEN), F32)
    b = jnp.zeros((nrow, 2 * NSA_CMP_HIDDEN), F32)
    for t in range(NSA_CMP_STRIDE):
        xt = x_ref[pl.ds(t, nrow, stride=NSA_CMP_STRIDE), :]
        a = a + _dot((xt + pet_ref[t:t + 1, :]).astype(BF16), w1t_ref[t])
        b = b + _dot((xt + peb_ref[t:t + 1, :]).astype(BF16), w1b_ref[t])
    hid = a + pltpu.roll(b, nrow - 1, 0)
    hid = hid * jax.nn.sigmoid(hid)
    out = _dot(hid.astype(BF16), w2_ref[...])
    o_ref[...] = out.astype(o_ref.dtype)
    vt_ref[...] = out.T[HEAD_DIM:2 * HEAD_DIM, :].astype(vt_ref.dtype)


def _compress(cmp_kv, pe_top, pe_bot, w1_top, w1_bot, w2):
    B, S, _ = cmp_kv.shape
    half = NSA_CMP_STRIDE
    nchunk = S // half
    hw = 2 * NSA_CMP_HIDDEN
    return pl.pallas_call(
        _compress_kernel,
        grid=(B,),
        in_specs=[
            pl.BlockSpec((None, S, LANES), lambda b: (b, 0, 0)),
            pl.BlockSpec((half, LANES), lambda b: (0, 0)),
            pl.BlockSpec((half, LANES), lambda b: (0, 0)),
            pl.BlockSpec((half, LANES, hw), lambda b: (0, 0, 0)),
            pl.BlockSpec((half, LANES, hw), lambda b: (0, 0, 0)),
            pl.BlockSpec((hw, LANES), lambda b: (0, 0)),
        ],
        out_specs=[
            pl.BlockSpec((None, nchunk, LANES), lambda b: (b, 0, 0)),
            pl.BlockSpec((None, HEAD_DIM, nchunk), lambda b: (b, 0, 0)),
        ],
        out_shape=[
            jax.ShapeDtypeStruct((B, nchunk, LANES), BF16),
            jax.ShapeDtypeStruct((B, HEAD_DIM, nchunk), BF16),
        ],
        compiler_params=_params("arbitrary"),
        name="nsa_compress",
    )(cmp_kv, pe_top, pe_bot, w1_top, w1_bot, w2)


def _nsa_kernel(qt_ref, cmp_ref, cmpvt_ref, sel_ref, selvt_ref, win_ref, winvt_ref, misc_ref, ovlt_ref,
                o_ref, member_ref, cnt_ref, sa_ref, sb_ref, m_ref, l_ref, acc_ref, *, seq, tk):
    H, Q = NSA_HEADS, Q_BLOCK
    HQ = H * Q
    q0 = pl.program_id(1) * Q
    qt = qt_ref[...]
    ncmp = cmp_ref.shape[0]
    n_sel = seq // NSA_SEL_LEN
    sel_shift = NSA_SEL_LEN.bit_length() - 1

    def heads(a):
        return jnp.concatenate([a] * H, axis=1)

    wlen = NSA_WINDOW + Q
    start = pl.multiple_of(jnp.maximum(q0 - NSA_WINDOW, 0), Q)
    s = _dot(win_ref[pl.ds(start, wlen), :], qt)
    kpos = start + lax.broadcasted_iota(jnp.int32, (wlen, Q), 0)
    tw = q0 + lax.broadcasted_iota(jnp.int32, (wlen, Q), 1)
    s = s + heads(jnp.where((kpos <= tw) & (kpos > tw - NSA_WINDOW), 0.0, NEG_INF))
    e = jnp.exp2(s - jnp.max(s, axis=0, keepdims=True))
    sblk = start // Q
    vt = jnp.concatenate([winvt_ref[sblk + u] for u in range(wlen // Q)], axis=1)
    o_win = _dot(vt, e.astype(BF16)) / jnp.sum(e, axis=0, keepdims=True)

    s = _dot(cmp_ref[...], qt)
    nn = lax.broadcasted_iota(jnp.int32, (ncmp, Q), 0)
    tt = q0 + lax.broadcasted_iota(jnp.int32, (ncmp, Q), 1)
    cmask = heads((nn * NSA_CMP_STRIDE + (NSA_CMP_LEN - 1) <= tt) & (nn < ncmp - 1))
    s = jnp.where(cmask, s, NEG_INF)
    m = jnp.max(s, axis=0, keepdims=True)
    e = jnp.where(cmask, jnp.exp2(s - m), 0.0)
    den = jnp.sum(e, axis=0, keepdims=True)
    p_cmp = e / jnp.where(den > 0, den, 1.0)
    o_cmp = _dot(cmpvt_ref[...], p_cmp.astype(BF16))

    psum = p_cmp[:, 0:Q] + p_cmp[:, Q:2 * Q] + p_cmp[:, 2 * Q:3 * Q] + p_cmp[:, 3 * Q:4 * Q]
    hi = psum.astype(BF16)
    lo = (psum - hi.astype(F32)).astype(BF16)
    ovlt = ovlt_ref[...]
    imp = _dot(ovlt, hi) + _dot(ovlt, lo)
    jj = lax.broadcasted_iota(jnp.int32, (n_sel, Q), 0)
    tq = q0 + lax.broadcasted_iota(jnp.int32, (n_sel, Q), 1)
    valid = jj * NSA_SEL_LEN <= tq
    forced = (jj == (tq >> sel_shift)) | (jj == 0)
    imp = jnp.where(forced, NSA_FORCE_SCORE, jnp.where(valid, imp, -1.0))
    bpt = tk // NSA_SEL_LEN
    vpt = tk // Q
    n_full = q0 // tk

    def sel_scores(k, dst_ref):
        dst_ref[...] = _dot(sel_ref[pl.ds(pl.multiple_of(k * tk, tk), tk), :], qt)

    sel_scores(0, sa_ref)
    m_ref[...] = jnp.full((1, HQ), NEG_INF, F32)
    l_ref[...] = jnp.zeros(l_ref.shape, F32)
    acc_ref[...] = jnp.zeros(acc_ref.shape, F32)

    ngrp = n_sel // SUBLANES
    grp = [imp[r * SUBLANES:(r + 1) * SUBLANES, :] for r in range(ngrp)]
    jrow = lax.broadcasted_iota(jnp.int32, (SUBLANES, Q), 0)
    cnt_ref[...] = jnp.zeros((n_sel, Q), F32)
    last_started = (q0 + Q - 1) >> sel_shift
    for ib in range(ngrp):
        @pl.when(ib * SUBLANES <= last_started)
        def _(ib=ib):
            cnt = [cnt_ref[r * SUBLANES:(r + 1) * SUBLANES, :] for r in range(ngrp)]
            for i in range(ib * SUBLANES, (ib + 1) * SUBLANES):
                row = jnp.broadcast_to(imp[i:i + 1, :], (SUBLANES, Q))
                for r in range(ngrp):
                    if r > ib:
                        hit = jnp.where(row >= grp[r], 1.0, 0.0)
                    elif r < ib:
                        hit = jnp.where(row > grp[r], 1.0, 0.0)
                    else:
                        hit = jnp.where(jrow + r * SUBLANES > i, jnp.where(row >= grp[r], 1.0, 0.0),
                                        jnp.where(row > grp[r], 1.0, 0.0))
                    cnt[r] = cnt[r] + hit
            for r in range(ngrp):
                cnt_ref[r * SUBLANES:(r + 1) * SUBLANES, :] = cnt[r]
    top_k = min(NSA_SEL_TOPK, n_sel)
    member_ref[...] = jnp.where(cnt_ref[...] < top_k, 0.0, NEG_INF)

    def sel_update(k, src_ref, causal):
        bias = jnp.concatenate(
            [jnp.broadcast_to(member_ref[pl.ds(k * bpt + jb, 1), :], (NSA_SEL_LEN, Q)) for jb in range(bpt)],
            axis=0)
        if causal:
            kpos = k * tk + lax.broadcasted_iota(jnp.int32, (tk, Q), 0)
            tcol = q0 + lax.broadcasted_iota(jnp.int32, (tk, Q), 1)
            bias = jnp.where(kpos <= tcol, bias, NEG_INF)
        s = src_ref[...] + heads(bias)
        m = m_ref[...]
        m_new = jnp.maximum(m, jnp.max(s, axis=0, keepdims=True))
        e = jnp.exp2(s - m_new)
        alpha = jnp.exp2(m - m_new)
        l_ref[...] = alpha * l_ref[...] + jnp.sum(e, axis=0, keepdims=True)
        vt = jnp.concatenate([selvt_ref[k * vpt + u] for u in range(vpt)], axis=1)
        acc_ref[...] = alpha * acc_ref[...] + _dot(vt, e.astype(BF16))
        m_ref[...] = m_new

    def sel_pair(j, _):
        sel_scores(2 * j + 1, sb_ref)
        sel_update(2 * j, sa_ref, False)
        sel_scores(2 * j + 2, sa_ref)
        sel_update(2 * j + 1, sb_ref, False)
        return 0

    lax.fori_loop(0, n_full // 2, sel_pair, 0)

    @pl.when(n_full % 2 == 1)
    def _():
        sel_scores(n_full, sb_ref)
        sel_update(n_full - 1, sa_ref, False)
        sel_update(n_full, sb_ref, True)

    @pl.when(n_full % 2 == 0)
    def _():
        sel_update(n_full, sa_ref, True)

    o_sel = acc_ref[...] / l_ref[...]

    g = jax.nn.sigmoid(misc_ref[...].T)
    outs = []
    for h in range(H):
        r0 = MISC_GATE_LANE + 3 * h
        cols = slice(h * Q, (h + 1) * Q)
        outs.append(g[r0:r0 + 1, :] * o_cmp[:, cols] + g[r0 + 1:r0 + 2, :] * o_sel[:, cols]
                    + g[r0 + 2:r0 + 3, :] * o_win[:, cols])
    o_ref[...] = jnp.concatenate(outs, axis=0).T.astype(o_ref.dtype)


def _nsa_attention(qt, cmp_out, cmp_vt, sel_kv, sel_vt, win_kv, win_vt, misc, ovlt, *, tk=512):
    B, S, _ = sel_kv.shape
    ncmp = cmp_out.shape[1]
    n_sel = S // NSA_SEL_LEN
    nqb = S // Q_BLOCK
    assert S % tk == 0 and tk % Q_BLOCK == 0 and S >= NSA_WINDOW + Q_BLOCK and n_sel % SUBLANES == 0
    kern = functools.partial(_nsa_kernel, seq=S, tk=tk)
    return pl.pallas_call(
        kern,
        grid=(B, nqb),
        in_specs=[
            pl.BlockSpec((None, None, 2 * HEAD_DIM, NSA_HEADS * Q_BLOCK), lambda b, i: (b, i, 0, 0)),
            pl.BlockSpec((None, ncmp, LANES), lambda b, i: (b, 0, 0)),
            pl.BlockSpec((None, HEAD_DIM, ncmp), lambda b, i: (b, 0, 0)),
            pl.BlockSpec((None, S, LANES), lambda b, i: (b, 0, 0)),
            pl.BlockSpec((None, nqb, HEAD_DIM, Q_BLOCK), lambda b, i: (b, 0, 0, 0)),
            pl.BlockSpec((None, S, LANES), lambda b, i: (b, 0, 0)),
            pl.BlockSpec((None, nqb, HEAD_DIM, Q_BLOCK), lambda b, i: (b, 0, 0, 0)),
            pl.BlockSpec((None, Q_BLOCK, LANES), lambda b, i: (b, i, 0)),
            pl.BlockSpec((n_sel, ncmp), lambda b, i: (0, 0)),
        ],
        out_specs=pl.BlockSpec((None, Q_BLOCK, NSA_Q_W), lambda b, i: (b, i, 0)),
        out_shape=jax.ShapeDtypeStruct((B, S, NSA_Q_W), BF16),
        scratch_shapes=[
            pltpu.VMEM((n_sel, Q_BLOCK), F32),
            pltpu.VMEM((n_sel, Q_BLOCK), F32),
            pltpu.VMEM((tk, NSA_HEADS * Q_BLOCK), F32),
            pltpu.VMEM((tk, NSA_HEADS * Q_BLOCK), F32),
            pltpu.VMEM((1, NSA_HEADS * Q_BLOCK), F32),
            pltpu.VMEM((1, NSA_HEADS * Q_BLOCK), F32),
            pltpu.VMEM((HEAD_DIM, NSA_HEADS * Q_BLOCK), F32),
        ],
        compiler_params=_params("arbitrary", "arbitrary"),
        name="nsa_attention",
    )(qt, cmp_out, cmp_vt, sel_kv, sel_vt, win_kv, win_vt, misc, ovlt)


def _dil_kernel(q_ref, k_ref, v_ref, o_ref, lse_ref, vt_ref, *, band, dil):
    Q = Q_BLOCK
    r = pl.program_id(1)
    nblk = q_ref.shape[0] // (dil * Q)
    row = lax.broadcasted_iota(jnp.int32, (LANES, Q), 0)

    def rows(first, count):
        if dil == 1:
            return pl.ds(first, count)
        return pl.ds(first * dil + r, count, stride=dil)

    per_step = min(DIL_BLOCKS_PER_STEP, nblk)

    def transpose_v(j, _):
        for u in range(per_step):
            jb = per_step * j + u
            vt_ref[jb] = v_ref[rows(jb * Q, Q), :].T.astype(BF16)
        return 0

    lax.fori_loop(0, nblk // per_step, transpose_v, 0)

    def scores(jq):
        kb = jnp.maximum(jq - 1, 0)
        qt = q_ref[rows(jq * Q, Q), :].T
        rhs = jnp.concatenate([jnp.where(row < HEAD_DIM, qt, 0.0), jnp.where(row >= HEAD_DIM, qt, 0.0)],
                              axis=1).astype(BF16)
        return _dot(k_ref[rows(kb * Q, 2 * Q), :].astype(BF16), rhs)

    def attend(jq, s):
        kb = jnp.maximum(jq - 1, 0)
        dist = (jq - kb) * Q + lax.broadcasted_iota(jnp.int32, (2 * Q, Q), 1) \
            - lax.broadcasted_iota(jnp.int32, (2 * Q, Q), 0)
        bias = jnp.where((dist >= 0) & (dist <= band), 0.0, NEG_INF)
        s = s + jnp.concatenate([bias, bias], axis=1)
        m = jnp.max(s, axis=0, keepdims=True)
        e = jnp.exp2(s - m)
        l = jnp.sum(e, axis=0, keepdims=True)
        eb = e.astype(BF16)
        vt = jnp.concatenate([vt_ref[kb], vt_ref[kb + 1]], axis=1)
        return m, l, _dot(vt[0:HEAD_DIM], eb[:, 0:Q]), _dot(vt[HEAD_DIM:2 * HEAD_DIM], eb[:, Q:2 * Q])

    def emit(jq, m, l, pv0, pv1):
        inv = 1.0 / l
        lse = m * (1.0 / LOG2E) + jnp.log(l)
        lt = jnp.concatenate([jnp.broadcast_to(lse[:, 0:Q], (HEAD_DIM, Q)),
                              jnp.broadcast_to(lse[:, Q:2 * Q], (HEAD_DIM, Q))], axis=0)
        ot = jnp.concatenate([pv0 * inv[:, 0:Q], pv1 * inv[:, Q:2 * Q]], axis=0)
        o_ref[rows(jq * Q, Q), :] = ot.T
        lse_ref[rows(jq * Q, Q), :] = lt.T

    def step(j, _):
        blocks = [per_step * j + u for u in range(per_step)]
        ss = [scores(jq) for jq in blocks]
        parts = [attend(jq, s) for jq, s in zip(blocks, ss)]
        for jq, part in zip(blocks, parts):
            emit(jq, *part)
        return 0

    lax.fori_loop(0, nblk // per_step, step, 0)


def _dilated_group(q, k, v, group, window, dil):
    B, S, W = q.shape
    nblk = S // dil // Q_BLOCK
    assert dil & (dil - 1) == 0 and window // dil == Q_BLOCK and nblk >= 2 and nblk % min(DIL_BLOCKS_PER_STEP, nblk) == 0
    in_spec = pl.BlockSpec((None, S, LANES), lambda b, r: (b, 0, group))
    out_spec = pl.BlockSpec((None, S, LANES), lambda b, r: (b, 0, 0))
    out_sds = jax.ShapeDtypeStruct((B, S, LANES), F32)
    o, lse = pl.pallas_call(
        functools.partial(_dil_kernel, band=window // dil, dil=dil),
        grid=(B, dil),
        in_specs=[in_spec, in_spec, in_spec],
        out_specs=[out_spec, out_spec],
        out_shape=[out_sds, out_sds],
        scratch_shapes=[pltpu.VMEM((nblk, LANES, Q_BLOCK), BF16)],
        compiler_params=_params("arbitrary", "arbitrary"),
        name=f"dilated_attention_d{dil}",
    )(q, k, v)
    return o.reshape(B * S, LANES), lse.reshape(B * S, LANES)


def _dilated_attention(q, k, v):
    outs = [_dilated_group(q, k, v, g, window, dil) for g, (window, dil) in enumerate(DIL_PATTERNS)]
    return [o for o, _ in outs] + [l for _, l in outs]


def _fox_prep_kernel(misc_ref, bias_ref, q_ref, k_ref, kaug_ref, qtaug_ref, carry_ref):
    tb = misc_ref.shape[0]

    @pl.when(pl.program_id(1) == 0)
    def _():
        carry_ref[...] = jnp.zeros_like(carry_ref)

    r = lax.broadcasted_iota(jnp.int32, (LANES, LANES), 0)
    c = lax.broadcasted_iota(jnp.int32, (LANES, LANES), 1)
    tri = jnp.where(r >= c, 1.0, 0.0).astype(F32)
    carry = carry_ref[...]
    cums = []
    for blk in range(tb // LANES):
        x = misc_ref[blk * LANES:(blk + 1) * LANES, :] + bias_ref[...]
        log_f = -(jnp.maximum(-x, 0.0) + jnp.log1p(jnp.exp(-jnp.abs(x))))
        cs = jnp.dot(tri, log_f, preferred_element_type=F32, precision=lax.Precision.HIGHEST) + carry
        cums.append(cs)
        carry = cs[LANES - 1:LANES, :]
    carry_ref[...] = carry
    cum = jnp.concatenate(cums, axis=0) * LOG2E

    lane = lax.broadcasted_iota(jnp.int32, (tb, LANES), 1)
    a0 = HEAD_DIM
    for h in range(FOX_HEADS):
        cb = jnp.broadcast_to(cum[:, h:h + 1], (tb, LANES))
        hi = cb.astype(BF16).astype(F32)
        r1 = cb - hi
        mid = r1.astype(BF16).astype(F32)
        lo = r1 - mid
        pieces = jnp.where(lane == a0, hi, jnp.where(lane == a0 + 1, mid, lo))
        g0 = (h // 2) * LANES
        qg = q_ref[:, g0:g0 + LANES].astype(F32)
        kg = k_ref[:, g0:g0 + LANES].astype(F32)
        if h % 2 == 1:
            qg = pltpu.roll(qg, HEAD_DIM, 1)
            kg = pltpu.roll(kg, HEAD_DIM, 1)
        in_c = (lane >= a0) & (lane < a0 + FOX_AUG)
        in_1 = (lane >= a0 + FOX_AUG) & (lane < a0 + 2 * FOX_AUG)
        q_aug = jnp.where(lane < a0, qg, jnp.where(in_c, pieces, jnp.where(in_1, 1.0, 0.0)))
        k_neg = -pltpu.roll(pieces, FOX_AUG, 1)
        k_aug = jnp.where(lane < a0, kg, jnp.where(in_c, 1.0, jnp.where(in_1, k_neg, 0.0)))
        kaug_ref[h] = k_aug.astype(BF16)
        qtaug_ref[h] = q_aug.T.astype(BF16)


def _fox_prep(misc, bias_row, q, k, *, tb=512):
    B, S, _ = misc.shape
    assert S % tb == 0
    return pl.pallas_call(
        _fox_prep_kernel,
        grid=(B, S // tb),
        in_specs=[
            pl.BlockSpec((None, tb, LANES), lambda b, i: (b, i, 0)),
            pl.BlockSpec((1, LANES), lambda b, i: (0, 0)),
            pl.BlockSpec((None, tb, FOX_W), lambda b, i: (b, i, 0)),
            pl.BlockSpec((None, tb, FOX_W), lambda b, i: (b, i, 0)),
        ],
        out_specs=[
            pl.BlockSpec((None, FOX_HEADS, tb, LANES), lambda b, i: (b, 0, i, 0)),
            pl.BlockSpec((None, FOX_HEADS, LANES, tb), lambda b, i: (b, 0, 0, i)),
        ],
        out_shape=[
            jax.ShapeDtypeStruct((B, FOX_HEADS, S, LANES), BF16),
            jax.ShapeDtypeStruct((B, FOX_HEADS, LANES, S), BF16),
        ],
        scratch_shapes=[pltpu.VMEM((1, LANES), F32)],
        compiler_params=_params("arbitrary", "arbitrary"),
        name="fox_prep",
    )(misc, bias_row, q, k)


def _fox_kernel(qt_ref, k_ref, vt_ref, o_ref, sa_ref, sb_ref, m_ref, l_ref, acc_ref, *, tq, tk):
    nh = qt_ref.shape[0]
    q0 = pl.program_id(2) * tq
    n_full = q0 // tk
    vpt = tk // Q_BLOCK

    def scores(k, dst_ref):
        base = pl.multiple_of(k * tk, tk)
        for h in range(nh):
            dst_ref[h] = _dot(k_ref[h, pl.ds(base, tk), :], qt_ref[h])

    def update(k, src_ref, causal):
        for h in range(nh):
            s = src_ref[h]
            if causal:
                kpos = k * tk + lax.broadcasted_iota(jnp.int32, (tk, tq), 0)
                tcol = q0 + lax.broadcasted_iota(jnp.int32, (tk, tq), 1)
                s = jnp.where(kpos <= tcol, s, NEG_INF)
            m = m_ref[h]
            m_new = jnp.maximum(m, jnp.max(s, axis=0, keepdims=True))
            e = jnp.exp2(s - m_new)
            alpha = jnp.exp2(m - m_new)
            l_ref[h] = alpha * l_ref[h] + jnp.sum(e, axis=0, keepdims=True)
            vt = jnp.concatenate([vt_ref[h, k * vpt + u] for u in range(vpt)], axis=1)
            acc_ref[h] = alpha * acc_ref[h] + _dot(vt, e.astype(BF16))
            m_ref[h] = m_new

    scores(0, sa_ref)
    m_ref[...] = jnp.full(m_ref.shape, NEG_INF, F32)
    l_ref[...] = jnp.zeros(l_ref.shape, F32)
    acc_ref[...] = jnp.zeros(acc_ref.shape, F32)

    def pair(j, _):
        scores(2 * j + 1, sb_ref)
        update(2 * j, sa_ref, False)
        scores(2 * j + 2, sa_ref)
        update(2 * j + 1, sb_ref, False)
        return 0

    lax.fori_loop(0, n_full // 2, pair, 0)

    @pl.when(n_full % 2 == 1)
    def _():
        scores(n_full, sb_ref)
        update(n_full - 1, sa_ref, False)
        update(n_full, sb_ref, True)

    @pl.when(n_full % 2 == 0)
    def _():
        update(n_full, sa_ref, True)

    outs = [acc_ref[h] / l_ref[h] for h in range(nh)]
    o_ref[...] = jnp.concatenate(outs, axis=0).T.astype(o_ref.dtype)


def _fox_attention(qt_aug, k_aug, vt, *, tq=256, tk=512, heads_per_step=FOX_HEADS):
    B, H, S, _ = k_aug.shape
    hps = heads_per_step
    assert S % tk == 0 and tk % tq == 0 and tk % Q_BLOCK == 0 and H % hps == 0 and (hps * HEAD_DIM) % LANES == 0
    nqb = S // Q_BLOCK
    kern = functools.partial(_fox_kernel, tq=tq, tk=tk)
    return pl.pallas_call(
        kern,
        grid=(B, H // hps, S // tq),
        in_specs=[
            pl.BlockSpec((None, hps, LANES, tq), lambda b, p, i: (b, p, 0, i)),
            pl.BlockSpec((None, hps, S, LANES), lambda b, p, i: (b, p, 0, 0)),
            pl.BlockSpec((None, hps, nqb, HEAD_DIM, Q_BLOCK), lambda b, p, i: (b, p, 0, 0, 0)),
        ],
        out_specs=pl.BlockSpec((None, tq, hps * HEAD_DIM), lambda b, p, i: (b, i, p)),
        out_shape=jax.ShapeDtypeStruct((B, S, H * HEAD_DIM), BF16),
        scratch_shapes=[
            pltpu.VMEM((hps, tk, tq), F32),
            pltpu.VMEM((hps, tk, tq), F32),
            pltpu.VMEM((hps, 1, tq), F32),
            pltpu.VMEM((hps, 1, tq), F32),
            pltpu.VMEM((hps, HEAD_DIM, tq), F32),
        ],
        compiler_params=_params("arbitrary", "arbitrary", "arbitrary"),
        name="fox_attention",
    )(qt_aug, k_aug, vt)


def _merge_kernel(x_ref, mod_ref, g_ref, ya_ref, yc_ref, d0_ref, d1_ref, d2_ref, l0_ref, l1_ref, l2_ref,
                  wg_ref, bra_ref, brb_ref, brc_ref, wo_ref, o_ref):
    x = x_ref[...]
    D = x.shape[1]
    n = _norm_modulate(x, g_ref[...], mod_ref[3:4, :], mod_ref[4:5, :]).astype(BF16)
    lse = [l0_ref[...], l1_ref[...], l2_ref[...]]
    mx = jnp.maximum(jnp.maximum(lse[0], lse[1]), lse[2])
    w = [jnp.exp(l - mx) for l in lse]
    wsum = w[0] + w[1] + w[2]
    yb = (w[0] / wsum) * d0_ref[...] + (w[1] / wsum) * d1_ref[...] + (w[2] / wsum) * d2_ref[...]
    merged = jax.nn.sigmoid(_dot(n, wg_ref[:, 0:D])) * _dot(ya_ref[...], bra_ref[...])
    merged = merged + jax.nn.sigmoid(_dot(n, wg_ref[:, D:2 * D])) * _dot(yb.astype(BF16), brb_ref[...])
    merged = merged + jax.nn.sigmoid(_dot(n, wg_ref[:, 2 * D:3 * D])) * _dot(yc_ref[...], brc_ref[...])
    o_ref[...] = x + mod_ref[5:6, :] * _dot(merged.astype(BF16), wo_ref[...])


def _merge(h, mod, g, ya, yc, dil_parts, w_gate, br_a, br_b, br_c, w_out, *, layer, seq, tm=PROJ_TM):
    T, D = h.shape
    assert T % tm == 0 and seq % tm == 0 and len(dil_parts) == 2 * len(DIL_PATTERNS)
    tpb = seq // tm

    def resident(shape):
        return pl.BlockSpec((None,) + shape, lambda i: (layer, 0, 0), pipeline_mode=pl.Buffered(1))

    dil_spec = pl.BlockSpec((tm, DIL_OUT_W), lambda i: (i, 0))
    return pl.pallas_call(
        _merge_kernel,
        grid=(T // tm,),
        in_specs=[
            pl.BlockSpec((tm, D), lambda i: (i, 0)),
            pl.BlockSpec((None, 9, D), lambda i: (i // tpb, 0, 0)),
            pl.BlockSpec((1, D), lambda i: (0, 0)),
            pl.BlockSpec((tm, NSA_Q_W), lambda i: (i, 0)),
            pl.BlockSpec((tm, FOX_W), lambda i: (i, 0)),
            dil_spec, dil_spec, dil_spec, dil_spec, dil_spec, dil_spec,
            resident((D, 3 * D)),
            resident((NSA_Q_W, D)),
            resident((DIL_OUT_W, D)),
            resident((FOX_W, D)),
            resident((D, D)),
        ],
        out_specs=pl.BlockSpec((tm, D), lambda i: (i, 0)),
        out_shape=jax.ShapeDtypeStruct((T, D), F32),
        compiler_params=_params("arbitrary"),
        name="merge_out",
    )(h, mod, g, ya, yc, *dil_parts, w_gate, br_a, br_b, br_c, w_out)


def _rope_tables(seq):
    inv_freq = ROPE_THETA ** (-jnp.arange(0, ROPE_DIM, 2, dtype=F32) / ROPE_DIM)
    ang = jnp.arange(seq, dtype=F32)[:, None] * inv_freq[None, :]
    cos, sin = jnp.cos(ang), jnp.sin(ang)
    d = np.arange(LANES) % HEAD_DIM
    idx = d % ROPE_HALF
    first = jnp.asarray(d < ROPE_HALF)
    second = jnp.asarray((d >= ROPE_HALF) & (d < ROPE_DIM))
    c_a = jnp.where(first | second, cos[:, idx], 1.0)
    s1_a = jnp.where(first, -sin[:, idx], 0.0)
    s2_a = jnp.where(second, sin[:, idx], 0.0)
    head0 = jnp.asarray(np.arange(LANES) < HEAD_DIM)
    c_b = jnp.where(head0, c_a, 1.0)
    s1_b = jnp.where(head0, s1_a, 0.0)
    s2_b = jnp.where(head0, s2_a, 0.0)
    return (jnp.concatenate([c_a, c_b], axis=1), jnp.concatenate([s1_a, s1_b], axis=1),
            jnp.concatenate([s2_a, s2_b], axis=1))


def _pack_mix_kernel(w_ref, proj_ref, gate_ref):
    rows = w_ref.shape[0]
    head = NSA_Q_W + 6 * HEAD_DIM
    body = 3 * DIL_W + 3 * FOX_W
    g0 = head
    b0 = g0 + 3 * NSA_HEADS
    f0 = b0 + body
    m0 = f0 + FOX_HEADS
    proj_ref[:, 0:head] = w_ref[:, 0:head].astype(BF16)
    proj_ref[:, head:head + body] = w_ref[:, b0:b0 + body].astype(BF16)
    misc = jnp.concatenate([
        w_ref[:, f0:f0 + FOX_HEADS], jnp.zeros((rows, MISC_GATE_LANE - MISC_FOX_LANE - FOX_HEADS), F32),
        w_ref[:, g0:g0 + 3 * NSA_HEADS], jnp.zeros((rows, LANES - MISC_GATE_LANE - 3 * NSA_HEADS), F32)], axis=1)
    proj_ref[:, head + body:head + body + LANES] = misc.astype(BF16)
    gate_ref[...] = w_ref[:, m0:m0 + gate_ref.shape[1]].astype(BF16)


def _pack_mix_w_in(w, *, tr=256):
    L, D, C = w.shape
    gate_cols = C - (PROJ_COLS - LANES) - FOX_HEADS - 3 * NSA_HEADS
    assert D % tr == 0 and gate_cols % LANES == 0
    return pl.pallas_call(
        _pack_mix_kernel,
        grid=(L, D // tr),
        in_specs=[pl.BlockSpec((None, tr, C), lambda l, i: (l, i, 0))],
        out_specs=[pl.BlockSpec((None, tr, PROJ_COLS), lambda l, i: (l, i, 0)),
                   pl.BlockSpec((None, tr, gate_cols), lambda l, i: (l, i, 0))],
        out_shape=[jax.ShapeDtypeStruct((L, D, PROJ_COLS), BF16), jax.ShapeDtypeStruct((L, D, gate_cols), BF16)],
        compiler_params=_params("arbitrary", "arbitrary"),
        name="pack_mix_w_in",
    )(w)


def _pack_compress(pe, w1, w2):
    half = NSA_CMP_STRIDE
    hid = NSA_CMP_HIDDEN
    w1k = w1[0].reshape(NSA_CMP_LEN, HEAD_DIM, hid)
    w1v = w1[1].reshape(NSA_CMP_LEN, HEAD_DIM, hid)
    z = jnp.zeros((half, HEAD_DIM, hid), w1.dtype)

    def halfpack(lo):
        kk = jnp.concatenate([w1k[lo:lo + half], z], axis=-1)
        vv = jnp.concatenate([z, w1v[lo:lo + half]], axis=-1)
        return jnp.concatenate([kk, vv], axis=1).astype(BF16)

    def pepack(lo):
        return jnp.concatenate([pe[0, lo:lo + half], pe[1, lo:lo + half]], axis=-1)

    zz = jnp.zeros((hid, HEAD_DIM), w2.dtype)
    w2p = jnp.concatenate([jnp.concatenate([w2[0], zz], axis=1),
                           jnp.concatenate([zz, w2[1]], axis=1)], axis=0).astype(BF16)
    return pepack(0), pepack(half), halfpack(0), halfpack(half), w2p


def _overlap_matrix_t(seq):
    n_chunk = seq // NSA_CMP_STRIDE
    n_cmp = (seq - NSA_CMP_LEN) // NSA_CMP_STRIDE + 1
    n_sel = seq // NSA_SEL_LEN
    cmp_start = np.arange(n_cmp) * NSA_CMP_STRIDE
    sel_start = np.arange(n_sel) * NSA_SEL_LEN
    ov = np.minimum(cmp_start[:, None] + NSA_CMP_LEN, sel_start[None, :] + NSA_SEL_LEN) \
        - np.maximum(cmp_start[:, None], sel_start[None, :])
    full = np.zeros((n_sel, n_chunk), np.float32)
    full[:, :n_cmp] = (np.clip(ov, 0, None) / NSA_CMP_LEN).T
    return jnp.asarray(full, dtype=BF16)


def kernel(x, c, ada_w, ada_b, norm_g, final_norm_g, ffn_w_in, ffn_w_out, mix_w_in, nsa_cmp_pe, nsa_cmp_w1,
           nsa_cmp_w2, fox_f_bias, br_w_nsa, br_w_dil, br_w_fox, mix_w_out):
    B, S, D = x.shape
    L = ada_w.shape[0]
    T = B * S
    mod_all = _ada_modulation(c, ada_w, ada_b).reshape(L, B, 9, D)
    tabs = _rope_tables(S)
    ovlt = _overlap_matrix_t(S)
    fg = final_norm_g.reshape(1, D)
    ffn_in, ffn_out = ffn_w_in.astype(BF16), ffn_w_out.astype(BF16)
    w_proj, w_gate = _pack_mix_w_in(mix_w_in)
    br_a, br_b, br_c = br_w_nsa.astype(BF16), br_w_dil.astype(BF16), br_w_fox.astype(BF16)
    w_mix_out = mix_w_out.astype(BF16)
    h = x.reshape(T, D)
    for l in range(L):
        mod = mod_all[l]
        h = _ffn(h, mod, norm_g[l, 0].reshape(1, D), ffn_in, ffn_out, fg,
                 layer=l, which=0, mod_base=0, final=False, seq=S)

        g1 = norm_g[l, 1].reshape(1, D)
        (nsa_qt, cmp_kv, sel_kv, sel_vt, win_kv, win_vt, dil_q, dil_k, dil_v, fox_q, fox_k, fox_vt,
         misc) = _mixer_proj(h, mod, g1, w_proj, tabs, layer=l, batch=B, seq=S)

        def bsd(a):
            return a.reshape(B, S, a.shape[-1])

        misc = bsd(misc)
        cmp_out, cmp_vt = _compress(bsd(cmp_kv), *_pack_compress(nsa_cmp_pe[l], nsa_cmp_w1[l], nsa_cmp_w2[l]))
        y_a = _nsa_attention(nsa_qt, cmp_out, cmp_vt, bsd(sel_kv), sel_vt, bsd(win_kv), win_vt, misc, ovlt)
        dil_parts = _dilated_attention(bsd(dil_q), bsd(dil_k), bsd(dil_v))
        bias_row = jnp.pad(fox_f_bias[l].reshape(1, FOX_HEADS),
                           ((0, 0), (MISC_FOX_LANE, LANES - MISC_FOX_LANE - FOX_HEADS)))
        k_aug, qt_aug = _fox_prep(misc, bias_row, bsd(fox_q), bsd(fox_k))
        y_c = _fox_attention(qt_aug, k_aug, fox_vt)

        h = _merge(h, mod, g1, y_a.reshape(T, NSA_Q_W), y_c.reshape(T, FOX_W), dil_parts,
                   w_gate, br_a, br_b, br_c, w_mix_out, layer=l, seq=S)

        h = _ffn(h, mod, norm_g[l, 2].reshape(1, D), ffn_in, ffn_out, fg,
                 layer=l, which=1, mod_base=6, final=(l == L - 1), seq=S)
    return h.reshape(B, S, D)
```

```python
import functools
import math

import numpy as np
import jax
import jax.numpy as jnp
from jax import lax
from jax.experimental import pallas as pl
from jax.experimental.pallas import tpu as pltpu

F32 = jnp.float32
BF16 = jnp.bfloat16

HEAD_DIM = 64
ROPE_DIM = 16
ROPE_HALF = ROPE_DIM // 2
ROPE_THETA = 500000.0
Q_BLOCK = 128
NEG_INF = -1e30
RMS_EPS = 1e-6
QK_SCALE = HEAD_DIM ** -0.5
LOG2E = math.log2(math.e)

NSA_HEADS = 4
NSA_CMP_LEN = 32
NSA_CMP_STRIDE = 16
NSA_CMP_HIDDEN = 128
NSA_SEL_LEN = 64
NSA_SEL_TOPK = 16
NSA_WINDOW = 512
NSA_FORCE_SCORE = 1e4

DIL_PATTERNS = ((128, 1), (512, 4), (2048, 16))
DIL_HEADS_PER_GROUP = 2
DIL_HEADS = DIL_HEADS_PER_GROUP * len(DIL_PATTERNS)
FOX_HEADS = 6

NSA_Q_W = NSA_HEADS * HEAD_DIM
DIL_W = DIL_HEADS * HEAD_DIM
DIL_OUT_W = DIL_HEADS_PER_GROUP * HEAD_DIM
FOX_W = FOX_HEADS * HEAD_DIM

LANES = 128
SUBLANES = 8
MISC_FOX_LANE = 0
MISC_GATE_LANE = 8
PROJ_TM = 512
FFN_TM = 1024
FOX_AUG = 3
DIL_BLOCKS_PER_STEP = 8
NSA_Q = 256

PROJ_GROUPS = (
    ("nsa_q", NSA_Q_W, "A", QK_SCALE * LOG2E),
    ("cmp_kv", LANES, "B", 1.0),
    ("sel_kv", LANES, "B", 1.0),
    ("win_kv", LANES, "B", 1.0),
    ("dil_q", DIL_W, "A", QK_SCALE * LOG2E),
    ("dil_k", DIL_W, "A", 1.0),
    ("dil_v", DIL_W, None, 1.0),
    ("fox_q", FOX_W, None, QK_SCALE * LOG2E),
    ("fox_k", FOX_W, None, 1.0),
    ("fox_v", FOX_W, None, 1.0),
    ("misc", LANES, None, 1.0),
)
PROJ_COLS = sum(g[1] for g in PROJ_GROUPS)

VMEM_LIMIT = 56 * 1024 * 1024


def _dot(a, b):
    return jnp.dot(a, b, preferred_element_type=F32)


def _dot_nt(a, b):
    return lax.dot_general(a, b, (((1,), (1,)), ((), ())), preferred_element_type=F32)


def _norm_modulate(x, g, shift, scale):
    ms = jnp.mean(x * x, axis=-1, keepdims=True)
    y = x * lax.rsqrt(ms + RMS_EPS) * g
    return y * (1.0 + scale) + shift


def _masked_softmax(s, mask):
    s = jnp.where(mask, s, NEG_INF)
    m = jnp.max(s, axis=-1, keepdims=True)
    e = jnp.where(mask, jnp.exp(s - m), 0.0)
    den = jnp.sum(e, axis=-1, keepdims=True)
    den = jnp.where(den > 0, den, 1.0)
    return e / den, m + jnp.log(den)


def _params(*sem):
    return pltpu.CompilerParams(dimension_semantics=sem, vmem_limit_bytes=VMEM_LIMIT)


def _ada_kernel(c_ref, w_ref, b_ref, o_ref):
    c = c_ref[...]
    cond = c * jax.nn.sigmoid(c)
    o_ref[...] = jnp.dot(cond, w_ref[...], preferred_element_type=F32,
                         precision=lax.Precision.HIGHEST) + b_ref[...]


def _ada_modulation(c, ada_w, ada_b):
    L, D, N = ada_w.shape
    B = c.shape[0]
    tn = 1152
    assert N % tn == 0
    return pl.pallas_call(
        _ada_kernel,
        grid=(L, N // tn),
        in_specs=[
            pl.BlockSpec((B, D), lambda l, j: (0, 0)),
            pl.BlockSpec((None, D, tn), lambda l, j: (l, 0, j)),
            pl.BlockSpec((None, 1, tn), lambda l, j: (l, 0, j)),
        ],
        out_specs=pl.BlockSpec((None, B, tn), lambda l, j: (l, 0, j)),
        out_shape=jax.ShapeDtypeStruct((L, B, N), F32),
        compiler_params=_params("arbitrary", "arbitrary"),
        name="ada_modulation",
    )(c, ada_w, ada_b.reshape(L, 1, N))


def _ffn_kernel(x_ref, mod_ref, g_ref, win_ref, wout_ref, fg_ref, o_ref, a_ref, *, mod_base, d_ff, chunk, final):
    x = x_ref[...]
    n = _norm_modulate(x, g_ref[...], mod_ref[mod_base:mod_base + 1, :],
                       mod_ref[mod_base + 1:mod_base + 2, :]).astype(BF16)
    for j in range(d_ff // chunk):
        gate = _dot(n, win_ref[:, j * chunk:(j + 1) * chunk])
        up = _dot(n, win_ref[:, d_ff + j * chunk:d_ff + (j + 1) * chunk])
        a_ref[:, j * chunk:(j + 1) * chunk] = (gate * jax.nn.sigmoid(gate) * up).astype(BF16)
    f = _dot(a_ref[...], wout_ref[...])
    out = x + (0.5 * mod_ref[mod_base + 2:mod_base + 3, :]) * f
    if final:
        ms = jnp.mean(out * out, axis=-1, keepdims=True)
        out = out * lax.rsqrt(ms + RMS_EPS) * fg_ref[...]
    o_ref[...] = out


def _ffn(h, mod, g, w_in, w_out, final_g, *, layer, which, mod_base, final, seq, tm=FFN_TM):
    T, D = h.shape
    d_ff = w_out.shape[2]
    chunk = 256
    assert T % tm == 0 and seq % tm == 0 and d_ff % chunk == 0
    tpb = seq // tm
    kern = functools.partial(_ffn_kernel, mod_base=mod_base, d_ff=d_ff, chunk=chunk, final=final)
    return pl.pallas_call(
        kern,
        grid=(T // tm,),
        in_specs=[
            pl.BlockSpec((tm, D), lambda i: (i, 0)),
            pl.BlockSpec((None, 9, D), lambda i: (i // tpb, 0, 0)),
            pl.BlockSpec((1, D), lambda i: (0, 0)),
            pl.BlockSpec((None, None, D, 2 * d_ff), lambda i: (layer, which, 0, 0), pipeline_mode=pl.Buffered(1)),
            pl.BlockSpec((None, None, d_ff, D), lambda i: (layer, which, 0, 0), pipeline_mode=pl.Buffered(1)),
            pl.BlockSpec((1, D), lambda i: (0, 0)),
        ],
        out_specs=pl.BlockSpec((tm, D), lambda i: (i, 0)),
        out_shape=jax.ShapeDtypeStruct((T, D), F32),
        scratch_shapes=[pltpu.VMEM((tm, d_ff), BF16)],
        compiler_params=_params("arbitrary"),
        name="ffn",
    )(h, mod, g, w_in, w_out, final_g)


def _rope_group(v, c, s1, s2):
    return v * c + pltpu.roll(v, LANES - ROPE_HALF, 1) * s1 + pltpu.roll(v, ROPE_HALF, 1) * s2


def _proj_kernel(x_ref, mod_ref, g_ref, w_ref, tc_ref, ts1_ref, ts2_ref,
                 nsa_qt_ref, cmp_ref, sel_ref, selvt_ref, win_ref, winvt_ref,
                 dq_ref, dk_ref, dv_ref, fq_ref, fk_ref, fvt_ref, misc_ref):
    x = x_ref[...]
    tm = x.shape[0]
    nblk = tm // Q_BLOCK
    n = _norm_modulate(x, g_ref[...], mod_ref[3:4, :], mod_ref[4:5, :]).astype(BF16)

    wide = {}

    def group(name, sub):
        off = 0
        for gname, width, rope, scale in PROJ_GROUPS:
            if gname == name:
                break
            off += width
        lo = off + sub * LANES
        blk = lo // (2 * LANES)
        if blk not in wide:
            wide[blk] = _dot(n, w_ref[:, blk * 2 * LANES:(blk + 1) * 2 * LANES])
        v = wide[blk][:, lo % (2 * LANES):lo % (2 * LANES) + LANES]
        if rope is not None:
            t0 = 0 if rope == "A" else LANES
            v = _rope_group(v, tc_ref[:, t0:t0 + LANES], ts1_ref[:, t0:t0 + LANES], ts2_ref[:, t0:t0 + LANES])
        if scale != 1.0:
            v = v * scale
        return v

    nq = nsa_qt_ref.shape[2] // NSA_HEADS
    zero_half = jnp.zeros((HEAD_DIM, NSA_HEADS * nq), BF16)
    for u in range(tm // nq):
        nsa_qt_ref[u, HEAD_DIM:2 * HEAD_DIM, :] = zero_half
    for sub in range(NSA_Q_W // LANES):
        vt = group("nsa_q", sub).T.astype(BF16)
        for hh in range(2):
            h = 2 * sub + hh
            for u in range(tm // nq):
                nsa_qt_ref[u, 0:HEAD_DIM, h * nq:(h + 1) * nq] = \
                    vt[hh * HEAD_DIM:(hh + 1) * HEAD_DIM, u * nq:(u + 1) * nq]

    cmp_ref[...] = group("cmp_kv", 0)
    for name, kv_ref, vt_ref in (("sel_kv", sel_ref, selvt_ref), ("win_kv", win_ref, winvt_ref)):
        v = group(name, 0)
        kv_ref[...] = v.astype(BF16)
        vt = v.T.astype(BF16)
        for u in range(nblk):
            vt_ref[u] = vt[HEAD_DIM:2 * HEAD_DIM, u * Q_BLOCK:(u + 1) * Q_BLOCK]

    for name, o_ref in (("dil_q", dq_ref), ("dil_k", dk_ref), ("dil_v", dv_ref), ("fox_q", fq_ref),
                        ("fox_k", fk_ref)):
        for sub in range(o_ref.shape[1] // LANES):
            o_ref[:, sub * LANES:(sub + 1) * LANES] = group(name, sub).astype(o_ref.dtype)

    for sub in range(FOX_W // LANES):
        vt = group("fox_v", sub).T.astype(BF16)
        for hh in range(2):
            for u in range(nblk):
                fvt_ref[2 * sub + hh, u] = vt[hh * HEAD_DIM:(hh + 1) * HEAD_DIM, u * Q_BLOCK:(u + 1) * Q_BLOCK]

    misc_ref[...] = group("misc", 0)


def _mixer_proj(h, mod, g, w, tabs, *, layer, batch, seq, nsa_q, tm=PROJ_TM):
    T, D = h.shape
    assert T % tm == 0 and seq % tm == 0 and tm % Q_BLOCK == 0 and tm % nsa_q == 0
    tpb = seq // tm
    nblk = tm // Q_BLOCK
    nqb = seq // Q_BLOCK
    tab_spec = pl.BlockSpec((tm, 2 * LANES), lambda i: (i % tpb, 0))

    def flat(width):
        return pl.BlockSpec((tm, width), lambda i: (i, 0))

    vt_spec = pl.BlockSpec((None, nblk, HEAD_DIM, Q_BLOCK), lambda i: (i // tpb, i % tpb, 0, 0))
    out_specs = [
        pl.BlockSpec((None, tm // nsa_q, 2 * HEAD_DIM, NSA_HEADS * nsa_q), lambda i: (i // tpb, i % tpb, 0, 0)),
        flat(LANES), flat(LANES), vt_spec, flat(LANES), vt_spec,
        flat(DIL_W), flat(DIL_W), flat(DIL_W), flat(FOX_W), flat(FOX_W),
        pl.BlockSpec((None, FOX_HEADS, nblk, HEAD_DIM, Q_BLOCK), lambda i: (i // tpb, 0, i % tpb, 0, 0)),
        flat(LANES),
    ]
    vt_shape = jax.ShapeDtypeStruct((batch, nqb, HEAD_DIM, Q_BLOCK), BF16)
    out_shape = [
        jax.ShapeDtypeStruct((batch, seq // nsa_q, 2 * HEAD_DIM, NSA_HEADS * nsa_q), BF16),
        jax.ShapeDtypeStruct((T, LANES), F32),
        jax.ShapeDtypeStruct((T, LANES), BF16), vt_shape,
        jax.ShapeDtypeStruct((T, LANES), BF16), vt_shape,
        jax.ShapeDtypeStruct((T, DIL_W), F32), jax.ShapeDtypeStruct((T, DIL_W), F32),
        jax.ShapeDtypeStruct((T, DIL_W), F32),
        jax.ShapeDtypeStruct((T, FOX_W), BF16), jax.ShapeDtypeStruct((T, FOX_W), BF16),
        jax.ShapeDtypeStruct((batch, FOX_HEADS, nqb, HEAD_DIM, Q_BLOCK), BF16),
        jax.ShapeDtypeStruct((T, LANES), F32),
    ]
    return pl.pallas_call(
        _proj_kernel,
        grid=(T // tm,),
        in_specs=[
            pl.BlockSpec((tm, D), lambda i: (i, 0)),
            pl.BlockSpec((None, 9, D), lambda i: (i // tpb, 0, 0)),
            pl.BlockSpec((1, D), lambda i: (0, 0)),
            pl.BlockSpec((None, D, PROJ_COLS), lambda i: (layer, 0, 0), pipeline_mode=pl.Buffered(1)),
            tab_spec, tab_spec, tab_spec,
        ],
        out_specs=out_specs,
        out_shape=out_shape,
        compiler_params=_params("arbitrary"),
        name="mixer_proj",
    )(h, mod, g, w, *tabs)


def _compress_kernel(x_ref, pet_ref, peb_ref, w1t_ref, w1b_ref, w2_ref, o_ref, vt_ref):
    nrow = x_ref.shape[0] // NSA_CMP_STRIDE
    a = jnp.zeros((nrow, 2 * NSA_CMP_HIDDEN), F32)
    b = jnp.zeros((nrow, 2 * NSA_CMP_HIDDEN), F32)
    for t in range(NSA_CMP_STRIDE):
        xt = x_ref[pl.ds(t, nrow, stride=NSA_CMP_STRIDE), :]
        a = a + _dot((xt + pet_ref[t:t + 1, :]).astype(BF16), w1t_ref[t])
        b = b + _dot((xt + peb_ref[t:t + 1, :]).astype(BF16), w1b_ref[t])
    hid = a + pltpu.roll(b, nrow - 1, 0)
    hid = hid * jax.nn.sigmoid(hid)
    out = _dot(hid.astype(BF16), w2_ref[...])
    o_ref[...] = out.astype(o_ref.dtype)
    vt_ref[...] = out.T[HEAD_DIM:2 * HEAD_DIM, :].astype(vt_ref.dtype)


def _compress(cmp_kv, pe_top, pe_bot, w1_top, w1_bot, w2):
    B, S, _ = cmp_kv.shape
    half = NSA_CMP_STRIDE
    nchunk = S // half
    hw = 2 * NSA_CMP_HIDDEN
    return pl.pallas_call(
        _compress_kernel,
        grid=(B,),
        in_specs=[
            pl.BlockSpec((None, S, LANES), lambda b: (b, 0, 0)),
            pl.BlockSpec((half, LANES), lambda b: (0, 0)),
            pl.BlockSpec((half, LANES), lambda b: (0, 0)),
            pl.BlockSpec((half, LANES, hw), lambda b: (0, 0, 0)),
            pl.BlockSpec((half, LANES, hw), lambda b: (0, 0, 0)),
            pl.BlockSpec((hw, LANES), lambda b: (0, 0)),
        ],
        out_specs=[
            pl.BlockSpec((None, nchunk, LANES), lambda b: (b, 0, 0)),
            pl.BlockSpec((None, HEAD_DIM, nchunk), lambda b: (b, 0, 0)),
        ],
        out_shape=[
            jax.ShapeDtypeStruct((B, nchunk, LANES), BF16),
            jax.ShapeDtypeStruct((B, HEAD_DIM, nchunk), BF16),
        ],
        compiler_params=_params("arbitrary"),
        name="nsa_compress",
    )(cmp_kv, pe_top, pe_bot, w1_top, w1_bot, w2)


def _nsa_kernel(qt_ref, cmp_ref, cmpvt_ref, sel_ref, selvt_ref, win_ref, winvt_ref, misc_ref, ovlt_ref,
                o_ref, member_ref, cnt_ref, sa_ref, sb_ref, m_ref, l_ref, acc_ref, *, seq, tk):
    H = NSA_HEADS
    Q = qt_ref.shape[1] // H
    VB = Q_BLOCK
    HQ = H * Q
    q0 = pl.program_id(1) * Q
    qt = qt_ref[...]
    ncmp = cmp_ref.shape[0]
    n_sel = seq // NSA_SEL_LEN
    sel_shift = NSA_SEL_LEN.bit_length() - 1

    def heads(a):
        return jnp.concatenate([a] * H, axis=1)

    wlen = NSA_WINDOW + Q
    start = pl.multiple_of(jnp.maximum(q0 - NSA_WINDOW, 0), Q)
    s = _dot(win_ref[pl.ds(start, wlen), :], qt)
    kpos = start + lax.broadcasted_iota(jnp.int32, (wlen, Q), 0)
    tw = q0 + lax.broadcasted_iota(jnp.int32, (wlen, Q), 1)
    s = s + heads(jnp.where((kpos <= tw) & (kpos > tw - NSA_WINDOW), 0.0, NEG_INF))
    e = jnp.exp2(s - jnp.max(s, axis=0, keepdims=True))
    sblk = start // VB
    vt = jnp.concatenate([winvt_ref[sblk + u] for u in range(wlen // VB)], axis=1)
    o_win = _dot(vt, e.astype(BF16)) / jnp.sum(e, axis=0, keepdims=True)

    s = _dot(cmp_ref[...], qt)
    nn = lax.broadcasted_iota(jnp.int32, (ncmp, Q), 0)
    tt = q0 + lax.broadcasted_iota(jnp.int32, (ncmp, Q), 1)
    cmask = heads((nn * NSA_CMP_STRIDE + (NSA_CMP_LEN - 1) <= tt) & (nn < ncmp - 1))
    s = jnp.where(cmask, s, NEG_INF)
    m = jnp.max(s, axis=0, keepdims=True)
    e = jnp.where(cmask, jnp.exp2(s - m), 0.0)
    den = jnp.sum(e, axis=0, keepdims=True)
    p_cmp = e / jnp.where(den > 0, den, 1.0)
    o_cmp = _dot(cmpvt_ref[...], p_cmp.astype(BF16))

    psum = p_cmp[:, 0:Q] + p_cmp[:, Q:2 * Q] + p_cmp[:, 2 * Q:3 * Q] + p_cmp[:, 3 * Q:4 * Q]
    hi = psum.astype(BF16)
    lo = (psum - hi.astype(F32)).astype(BF16)
    ovlt = ovlt_ref[...]
    imp = _dot(ovlt, hi) + _dot(ovlt, lo)
    jj = lax.broadcasted_iota(jnp.int32, (n_sel, Q), 0)
    tq = q0 + lax.broadcasted_iota(jnp.int32, (n_sel, Q), 1)
    valid = jj * NSA_SEL_LEN <= tq
    forced = (jj == (tq >> sel_shift)) | (jj == 0)
    imp = jnp.where(forced, NSA_FORCE_SCORE, jnp.where(valid, imp, -1.0))
    bpt = tk // NSA_SEL_LEN
    vpt = tk // VB
    n_full = q0 // tk

    def sel_scores(k, dst_ref):
        dst_ref[...] = _dot(sel_ref[pl.ds(pl.multiple_of(k * tk, tk), tk), :], qt)

    sel_scores(0, sa_ref)
    m_ref[...] = jnp.full((1, HQ), NEG_INF, F32)
    l_ref[...] = jnp.zeros(l_ref.shape, F32)
    acc_ref[...] = jnp.zeros(acc_ref.shape, F32)

    ngrp = n_sel // SUBLANES
    grp = [imp[r * SUBLANES:(r + 1) * SUBLANES, :] for r in range(ngrp)]
    jrow = lax.broadcasted_iota(jnp.int32, (SUBLANES, Q), 0)
    cnt_ref[...] = jnp.zeros((n_sel, Q), F32)
    last_started = (q0 + Q - 1) >> sel_shift
    for ib in range(ngrp):
        @pl.when(ib * SUBLANES <= last_started)
        def _(ib=ib):
            cnt = [cnt_ref[r * SUBLANES:(r + 1) * SUBLANES, :] for r in range(ngrp)]
            for i in range(ib * SUBLANES, (ib + 1) * SUBLANES):
                row = jnp.broadcast_to(imp[i:i + 1, :], (SUBLANES, Q))
                for r in range(ngrp):
                    if r > ib:
                        hit = jnp.where(row >= grp[r], 1.0, 0.0)
                    elif r < ib:
                        hit = jnp.where(row > grp[r], 1.0, 0.0)
                    else:
                        hit = jnp.where(jrow + r * SUBLANES > i, jnp.where(row >= grp[r], 1.0, 0.0),
                                        jnp.where(row > grp[r], 1.0, 0.0))
                    cnt[r] = cnt[r] + hit
            for r in range(ngrp):
                cnt_ref[r * SUBLANES:(r + 1) * SUBLANES, :] = cnt[r]
    top_k = min(NSA_SEL_TOPK, n_sel)
    member_ref[...] = jnp.where(cnt_ref[...] < top_k, 0.0, NEG_INF)

    def sel_update(k, src_ref, causal):
        bias = jnp.concatenate(
            [jnp.broadcast_to(member_ref[pl.ds(k * bpt + jb, 1), :], (NSA_SEL_LEN, Q)) for jb in range(bpt)],
            axis=0)
        if causal:
            kpos = k * tk + lax.broadcasted_iota(jnp.int32, (tk, Q), 0)
            tcol = q0 + lax.broadcasted_iota(jnp.int32, (tk, Q), 1)
            bias = jnp.where(kpos <= tcol, bias, NEG_INF)
        s = src_ref[...] + heads(bias)
        m = m_ref[...]
        m_new = jnp.maximum(m, jnp.max(s, axis=0, keepdims=True))
        e = jnp.exp2(s - m_new)
        alpha = jnp.exp2(m - m_new)
        l_ref[...] = alpha * l_ref[...] + jnp.sum(e, axis=0, keepdims=True)
        vt = jnp.concatenate([selvt_ref[k * vpt + u] for u in range(vpt)], axis=1)
        acc_ref[...] = alpha * acc_ref[...] + _dot(vt, e.astype(BF16))
        m_ref[...] = m_new

    def sel_pair(j, _):
        sel_scores(2 * j + 1, sb_ref)
        sel_update(2 * j, sa_ref, False)
        sel_scores(2 * j + 2, sa_ref)
        sel_update(2 * j + 1, sb_ref, False)
        return 0

    lax.fori_loop(0, n_full // 2, sel_pair, 0)

    @pl.when(n_full % 2 == 1)
    def _():
        sel_scores(n_full, sb_ref)
        sel_update(n_full - 1, sa_ref, False)
        sel_update(n_full, sb_ref, True)

    @pl.when(n_full % 2 == 0)
    def _():
        sel_update(n_full, sa_ref, True)

    o_sel = acc_ref[...] / l_ref[...]

    g = jax.nn.sigmoid(misc_ref[...].T)
    outs = []
    for h in range(H):
        r0 = MISC_GATE_LANE + 3 * h
        cols = slice(h * Q, (h + 1) * Q)
        outs.append(g[r0:r0 + 1, :] * o_cmp[:, cols] + g[r0 + 1:r0 + 2, :] * o_sel[:, cols]
                    + g[r0 + 2:r0 + 3, :] * o_win[:, cols])
    o_ref[...] = jnp.concatenate(outs, axis=0).T.astype(o_ref.dtype)


def _nsa_attention(qt, cmp_out, cmp_vt, sel_kv, sel_vt, win_kv, win_vt, misc, ovlt, *, tk=512):
    B, S, _ = sel_kv.shape
    ncmp = cmp_out.shape[1]
    n_sel = S // NSA_SEL_LEN
    nvb = S // Q_BLOCK
    nq = qt.shape[3] // NSA_HEADS
    assert S % tk == 0 and tk % nq == 0 and NSA_WINDOW % nq == 0 and nq % Q_BLOCK == 0
    assert S >= NSA_WINDOW + nq and n_sel % SUBLANES == 0
    kern = functools.partial(_nsa_kernel, seq=S, tk=tk)
    return pl.pallas_call(
        kern,
        grid=(B, S // nq),
        in_specs=[
            pl.BlockSpec((None, None, 2 * HEAD_DIM, NSA_HEADS * nq), lambda b, i: (b, i, 0, 0)),
            pl.BlockSpec((None, ncmp, LANES), lambda b, i: (b, 0, 0)),
            pl.BlockSpec((None, HEAD_DIM, ncmp), lambda b, i: (b, 0, 0)),
            pl.BlockSpec((None, S, LANES), lambda b, i: (b, 0, 0)),
            pl.BlockSpec((None, nvb, HEAD_DIM, Q_BLOCK), lambda b, i: (b, 0, 0, 0)),
            pl.BlockSpec((None, S, LANES), lambda b, i: (b, 0, 0)),
            pl.BlockSpec((None, nvb, HEAD_DIM, Q_BLOCK), lambda b, i: (b, 0, 0, 0)),
            pl.BlockSpec((None, nq, LANES), lambda b, i: (b, i, 0)),
            pl.BlockSpec((n_sel, ncmp), lambda b, i: (0, 0)),
        ],
        out_specs=pl.BlockSpec((None, nq, NSA_Q_W), lambda b, i: (b, i, 0)),
        out_shape=jax.ShapeDtypeStruct((B, S, NSA_Q_W), BF16),
        scratch_shapes=[
            pltpu.VMEM((n_sel, nq), F32),
            pltpu.VMEM((n_sel, nq), F32),
            pltpu.VMEM((tk, NSA_HEADS * nq), F32),
            pltpu.VMEM((tk, NSA_HEADS * nq), F32),
            pltpu.VMEM((1, NSA_HEADS * nq), F32),
            pltpu.VMEM((1, NSA_HEADS * nq), F32),
            pltpu.VMEM((HEAD_DIM, NSA_HEADS * nq), F32),
        ],
        compiler_params=_params("arbitrary", "arbitrary"),
        name="nsa_attention",
    )(qt, cmp_out, cmp_vt, sel_kv, sel_vt, win_kv, win_vt, misc, ovlt)


def _dil_kernel(q_ref, k_ref, v_ref, o_ref, lse_ref, vt_ref, *, band, dil, cps):
    Q = Q_BLOCK
    r0 = pl.program_id(1) * cps
    nblk = q_ref.shape[0] // (dil * Q)
    row = lax.broadcasted_iota(jnp.int32, (LANES, Q), 0)

    def rows(c, first, count):
        if dil == 1:
            return pl.ds(first, count)
        return pl.ds(first * dil + r0 + c, count, stride=dil)

    per_step = min(DIL_BLOCKS_PER_STEP // cps, nblk)

    def transpose_v(j, _):
        for c in range(cps):
            for u in range(per_step):
                jb = per_step * j + u
                vt_ref[c * nblk + jb] = v_ref[rows(c, jb * Q, Q), :].T.astype(BF16)
        return 0

    lax.fori_loop(0, nblk // per_step, transpose_v, 0)

    def scores(c, jq):
        kb = jnp.maximum(jq - 1, 0)
        qt = q_ref[rows(c, jq * Q, Q), :].T
        rhs = jnp.concatenate([jnp.where(row < HEAD_DIM, qt, 0.0), jnp.where(row >= HEAD_DIM, qt, 0.0)],
                              axis=1).astype(BF16)
        return _dot(k_ref[rows(c, kb * Q, 2 * Q), :].astype(BF16), rhs)

    def attend(c, jq, s):
        kb = jnp.maximum(jq - 1, 0)
        dist = (jq - kb) * Q + lax.broadcasted_iota(jnp.int32, (2 * Q, Q), 1) \
            - lax.broadcasted_iota(jnp.int32, (2 * Q, Q), 0)
        bias = jnp.where((dist >= 0) & (dist <= band), 0.0, NEG_INF)
        s = s + jnp.concatenate([bias, bias], axis=1)
        m = jnp.max(s, axis=0, keepdims=True)
        e = jnp.exp2(s - m)
        l = jnp.sum(e, axis=0, keepdims=True)
        eb = e.astype(BF16)
        vt = jnp.concatenate([vt_ref[c * nblk + kb], vt_ref[c * nblk + kb + 1]], axis=1)
        return m, l, _dot(vt[0:HEAD_DIM], eb[:, 0:Q]), _dot(vt[HEAD_DIM:2 * HEAD_DIM], eb[:, Q:2 * Q])

    def emit(c, jq, m, l, pv0, pv1):
        inv = 1.0 / l
        lse = m * (1.0 / LOG2E) + jnp.log(l)
        lt = jnp.concatenate([jnp.broadcast_to(lse[:, 0:Q], (HEAD_DIM, Q)),
                              jnp.broadcast_to(lse[:, Q:2 * Q], (HEAD_DIM, Q))], axis=0)
        ot = jnp.concatenate([pv0 * inv[:, 0:Q], pv1 * inv[:, Q:2 * Q]], axis=0)
        o_ref[rows(c, jq * Q, Q), :] = ot.T
        lse_ref[rows(c, jq * Q, Q), :] = lt.T

    def step(j, _):
        blocks = [(c, per_step * j + u) for c in range(cps) for u in range(per_step)]
        ss = [scores(c, jq) for c, jq in blocks]
        parts = [attend(c, jq, s) for (c, jq), s in zip(blocks, ss)]
        for (c, jq), part in zip(blocks, parts):
            emit(c, jq, *part)
        return 0

    lax.fori_loop(0, nblk // per_step, step, 0)


def _dilated_group(q, k, v, group, window, dil):
    B, S, W = q.shape
    nblk = S // dil // Q_BLOCK
    cps = min(dil, max(1, DIL_BLOCKS_PER_STEP // nblk))
    per_step = min(DIL_BLOCKS_PER_STEP // cps, nblk)
    assert dil & (dil - 1) == 0 and window // dil == Q_BLOCK and nblk >= 2
    assert dil % cps == 0 and nblk % per_step == 0
    in_spec = pl.BlockSpec((None, S, LANES), lambda b, r: (b, 0, group))
    out_spec = pl.BlockSpec((None, S, LANES), lambda b, r: (b, 0, 0))
    out_sds = jax.ShapeDtypeStruct((B, S, LANES), F32)
    o, lse = pl.pallas_call(
        functools.partial(_dil_kernel, band=window // dil, dil=dil, cps=cps),
        grid=(B, dil // cps),
        in_specs=[in_spec, in_spec, in_spec],
        out_specs=[out_spec, out_spec],
        out_shape=[out_sds, out_sds],
        scratch_shapes=[pltpu.VMEM((cps * nblk, LANES, Q_BLOCK), BF16)],
        compiler_params=_params("arbitrary", "arbitrary"),
        name=f"dilated_attention_d{dil}",
    )(q, k, v)
    return o.reshape(B * S, LANES), lse.reshape(B * S, LANES)


def _dilated_attention(q, k, v):
    outs = [_dilated_group(q, k, v, g, window, dil) for g, (window, dil) in enumerate(DIL_PATTERNS)]
    return [o for o, _ in outs] + [l for _, l in outs]


def _fox_prep_kernel(misc_ref, bias_ref, q_ref, k_ref, pq_ref, pk_ref, kaug_ref, qtaug_ref, carry_ref):
    tb = misc_ref.shape[0]

    @pl.when(pl.program_id(1) == 0)
    def _():
        carry_ref[...] = jnp.zeros_like(carry_ref)

    r = lax.broadcasted_iota(jnp.int32, (LANES, LANES), 0)
    c = lax.broadcasted_iota(jnp.int32, (LANES, LANES), 1)
    tri = jnp.where(r >= c, 1.0, 0.0).astype(F32)
    carry = carry_ref[...]
    cums = []
    for blk in range(tb // LANES):
        x = misc_ref[blk * LANES:(blk + 1) * LANES, :] + bias_ref[...]
        log_f = -(jnp.maximum(-x, 0.0) + jnp.log1p(jnp.exp(-jnp.abs(x))))
        cs = jnp.dot(tri, log_f, preferred_element_type=F32, precision=lax.Precision.HIGHEST) + carry
        cums.append(cs)
        carry = cs[LANES - 1:LANES, :]
    carry_ref[...] = carry
    lane = lax.broadcasted_iota(jnp.int32, (tb, LANES), 1)
    cum = jnp.where(lane < FOX_HEADS, jnp.concatenate(cums, axis=0) * LOG2E, 0.0)
    hi = cum.astype(BF16).astype(F32)
    r1 = cum - hi
    mid = r1.astype(BF16).astype(F32)
    lo = r1 - mid
    pieces = (hi + pltpu.roll(mid, FOX_HEADS, 1) + pltpu.roll(lo, 2 * FOX_HEADS, 1)
              + jnp.where(lane == FOX_AUG * FOX_HEADS, 1.0, 0.0)).astype(BF16)
    for h in range(FOX_HEADS):
        g0 = (h // 2) * LANES
        q_aug = _dot(jnp.concatenate([q_ref[:, g0:g0 + LANES], pieces], axis=1), pq_ref[h])
        k_aug = _dot(jnp.concatenate([k_ref[:, g0:g0 + LANES], pieces], axis=1), pk_ref[h])
        kaug_ref[h] = k_aug.astype(BF16)
        qtaug_ref[h] = q_aug.T.astype(BF16)


def _fox_placement():
    a0, nh = HEAD_DIM, FOX_HEADS
    pq = np.zeros((nh, 2 * LANES, LANES), np.float32)
    pk = np.zeros((nh, 2 * LANES, LANES), np.float32)
    one = LANES + FOX_AUG * nh
    for h in range(nh):
        for d in range(HEAD_DIM):
            pq[h, (h % 2) * HEAD_DIM + d, d] = 1.0
            pk[h, (h % 2) * HEAD_DIM + d, d] = 1.0
        for j in range(FOX_AUG):
            pq[h, LANES + j * nh + h, a0 + j] = 1.0
            pq[h, one, a0 + FOX_AUG + j] = 1.0
            pk[h, one, a0 + j] = 1.0
            pk[h, LANES + j * nh + h, a0 + FOX_AUG + j] = -1.0
    return jnp.asarray(pq, BF16), jnp.asarray(pk, BF16)


def _fox_prep(misc, bias_row, q, k, *, tb=512):
    B, S, _ = misc.shape
    assert S % tb == 0
    place_spec = pl.BlockSpec((FOX_HEADS, 2 * LANES, LANES), lambda b, i: (0, 0, 0))
    return pl.pallas_call(
        _fox_prep_kernel,
        grid=(B, S // tb),
        in_specs=[
            pl.BlockSpec((None, tb, LANES), lambda b, i: (b, i, 0)),
            pl.BlockSpec((1, LANES), lambda b, i: (0, 0)),
            pl.BlockSpec((None, tb, FOX_W), lambda b, i: (b, i, 0)),
            pl.BlockSpec((None, tb, FOX_W), lambda b, i: (b, i, 0)),
            place_spec, place_spec,
        ],
        out_specs=[
            pl.BlockSpec((None, FOX_HEADS, tb, LANES), lambda b, i: (b, 0, i, 0)),
            pl.BlockSpec((None, FOX_HEADS, LANES, tb), lambda b, i: (b, 0, 0, i)),
        ],
        out_shape=[
            jax.ShapeDtypeStruct((B, FOX_HEADS, S, LANES), BF16),
            jax.ShapeDtypeStruct((B, FOX_HEADS, LANES, S), BF16),
        ],
        scratch_shapes=[pltpu.VMEM((1, LANES), F32)],
        compiler_params=_params("arbitrary", "arbitrary"),
        name="fox_prep",
    )(misc, bias_row, q, k, *_fox_placement())


def _fox_kernel(qt_ref, k_ref, vt_ref, o_ref, sa_ref, sb_ref, m_ref, l_ref, acc_ref, *, tq, tk):
    nh = qt_ref.shape[0]
    q0 = pl.program_id(2) * tq
    n_full = q0 // tk
    vpt = tk // Q_BLOCK

    def scores(k, dst_ref):
        base = pl.multiple_of(k * tk, tk)
        for h in range(nh):
            dst_ref[h] = _dot(k_ref[h, pl.ds(base, tk), :], qt_ref[h])

    def update(k, src_ref, causal):
        for h in range(nh):
            s = src_ref[h]
            if causal:
                kpos = k * tk + lax.broadcasted_iota(jnp.int32, (tk, tq), 0)
                tcol = q0 + lax.broadcasted_iota(jnp.int32, (tk, tq), 1)
                s = jnp.where(kpos <= tcol, s, NEG_INF)
            m = m_ref[h]
            m_new = jnp.maximum(m, jnp.max(s, axis=0, keepdims=True))
            e = jnp.exp2(s - m_new)
            alpha = jnp.exp2(m - m_new)
            l_ref[h] = alpha * l_ref[h] + jnp.sum(e, axis=0, keepdims=True)
            vt = jnp.concatenate([vt_ref[h, k * vpt + u] for u in range(vpt)], axis=1)
            acc_ref[h] = alpha * acc_ref[h] + _dot(vt, e.astype(BF16))
            m_ref[h] = m_new

    scores(0, sa_ref)
    m_ref[...] = jnp.full(m_ref.shape, NEG_INF, F32)
    l_ref[...] = jnp.zeros(l_ref.shape, F32)
    acc_ref[...] = jnp.zeros(acc_ref.shape, F32)

    def pair(j, _):
        scores(2 * j + 1, sb_ref)
        update(2 * j, sa_ref, False)
        scores(2 * j + 2, sa_ref)
        update(2 * j + 1, sb_ref, False)
        return 0

    lax.fori_loop(0, n_full // 2, pair, 0)

    @pl.when(n_full % 2 == 1)
    def _():
        scores(n_full, sb_ref)
        update(n_full - 1, sa_ref, False)
        update(n_full, sb_ref, True)

    @pl.when(n_full % 2 == 0)
    def _():
        update(n_full, sa_ref, True)

    outs = [acc_ref[h] / l_ref[h] for h in range(nh)]
    o_ref[...] = jnp.concatenate(outs, axis=0).T.astype(o_ref.dtype)


def _fox_attention(qt_aug, k_aug, vt, *, tq=512, tk=512, heads_per_step=FOX_HEADS):
    B, H, S, _ = k_aug.shape
    hps = heads_per_step
    assert S % tk == 0 and tk % tq == 0 and tk % Q_BLOCK == 0 and H % hps == 0 and (hps * HEAD_DIM) % LANES == 0
    nqb = S // Q_BLOCK
    kern = functools.partial(_fox_kernel, tq=tq, tk=tk)
    return pl.pallas_call(
        kern,
        grid=(B, H // hps, S // tq),
        in_specs=[
            pl.BlockSpec((None, hps, LANES, tq), lambda b, p, i: (b, p, 0, i)),
            pl.BlockSpec((None, hps, S, LANES), lambda b, p, i: (b, p, 0, 0)),
            pl.BlockSpec((None, hps, nqb, HEAD_DIM, Q_BLOCK), lambda b, p, i: (b, p, 0, 0, 0)),
        ],
        out_specs=pl.BlockSpec((None, tq, hps * HEAD_DIM), lambda b, p, i: (b, i, p)),
        out_shape=jax.ShapeDtypeStruct((B, S, H * HEAD_DIM), BF16),
        scratch_shapes=[
            pltpu.VMEM((hps, tk, tq), F32),
            pltpu.VMEM((hps, tk, tq), F32),
            pltpu.VMEM((hps, 1, tq), F32),
            pltpu.VMEM((hps, 1, tq), F32),
            pltpu.VMEM((hps, HEAD_DIM, tq), F32),
        ],
        compiler_params=_params("arbitrary", "arbitrary", "arbitrary"),
        name="fox_attention",
    )(qt_aug, k_aug, vt)


def _merge_kernel(x_ref, mod_ref, g_ref, ya_ref, yc_ref, d0_ref, d1_ref, d2_ref, l0_ref, l1_ref, l2_ref,
                  wg_ref, bra_ref, brb_ref, brc_ref, wo_ref, o_ref):
    x = x_ref[...]
    D = x.shape[1]
    n = _norm_modulate(x, g_ref[...], mod_ref[3:4, :], mod_ref[4:5, :]).astype(BF16)
    lse = [l0_ref[...], l1_ref[...], l2_ref[...]]
    mx = jnp.maximum(jnp.maximum(lse[0], lse[1]), lse[2])
    w = [jnp.exp(l - mx) for l in lse]
    wsum = w[0] + w[1] + w[2]
    yb = (w[0] / wsum) * d0_ref[...] + (w[1] / wsum) * d1_ref[...] + (w[2] / wsum) * d2_ref[...]
    merged = jax.nn.sigmoid(_dot(n, wg_ref[:, 0:D])) * _dot(ya_ref[...], bra_ref[...])
    merged = merged + jax.nn.sigmoid(_dot(n, wg_ref[:, D:2 * D])) * _dot(yb.astype(BF16), brb_ref[...])
    merged = merged + jax.nn.sigmoid(_dot(n, wg_ref[:, 2 * D:3 * D])) * _dot(yc_ref[...], brc_ref[...])
    o_ref[...] = x + mod_ref[5:6, :] * _dot(merged.astype(BF16), wo_ref[...])


def _merge(h, mod, g, ya, yc, dil_parts, w_gate, br_a, br_b, br_c, w_out, *, layer, seq, tm=PROJ_TM):
    T, D = h.shape
    assert T % tm == 0 and seq % tm == 0 and len(dil_parts) == 2 * len(DIL_PATTERNS)
    tpb = seq // tm

    def resident(shape):
        return pl.BlockSpec((None,) + shape, lambda i: (layer, 0, 0), pipeline_mode=pl.Buffered(1))

    dil_spec = pl.BlockSpec((tm, DIL_OUT_W), lambda i: (i, 0))
    return pl.pallas_call(
        _merge_kernel,
        grid=(T // tm,),
        in_specs=[
            pl.BlockSpec((tm, D), lambda i: (i, 0)),
            pl.BlockSpec((None, 9, D), lambda i: (i // tpb, 0, 0)),
            pl.BlockSpec((1, D), lambda i: (0, 0)),
            pl.BlockSpec((tm, NSA_Q_W), lambda i: (i, 0)),
            pl.BlockSpec((tm, FOX_W), lambda i: (i, 0)),
            dil_spec, dil_spec, dil_spec, dil_spec, dil_spec, dil_spec,
            resident((D, 3 * D)),
            resident((NSA_Q_W, D)),
            resident((DIL_OUT_W, D)),
            resident((FOX_W, D)),
            resident((D, D)),
        ],
        out_specs=pl.BlockSpec((tm, D), lambda i: (i, 0)),
        out_shape=jax.ShapeDtypeStruct((T, D), F32),
        compiler_params=_params("arbitrary"),
        name="merge_out",
    )(h, mod, g, ya, yc, *dil_parts, w_gate, br_a, br_b, br_c, w_out)


def _rope_tables(seq):
    inv_freq = ROPE_THETA ** (-jnp.arange(0, ROPE_DIM, 2, dtype=F32) / ROPE_DIM)
    ang = jnp.arange(seq, dtype=F32)[:, None] * inv_freq[None, :]
    cos, sin = jnp.cos(ang), jnp.sin(ang)
    d = np.arange(LANES) % HEAD_DIM
    idx = d % ROPE_HALF
    first = jnp.asarray(d < ROPE_HALF)
    second = jnp.asarray((d >= ROPE_HALF) & (d < ROPE_DIM))
    c_a = jnp.where(first | second, cos[:, idx], 1.0)
    s1_a = jnp.where(first, -sin[:, idx], 0.0)
    s2_a = jnp.where(second, sin[:, idx], 0.0)
    head0 = jnp.asarray(np.arange(LANES) < HEAD_DIM)
    c_b = jnp.where(head0, c_a, 1.0)
    s1_b = jnp.where(head0, s1_a, 0.0)
    s2_b = jnp.where(head0, s2_a, 0.0)
    return (jnp.concatenate([c_a, c_b], axis=1), jnp.concatenate([s1_a, s1_b], axis=1),
            jnp.concatenate([s2_a, s2_b], axis=1))


def _pack_mix_kernel(w_ref, proj_ref, gate_ref):
    rows = w_ref.shape[0]
    head = NSA_Q_W + 6 * HEAD_DIM
    body = 3 * DIL_W + 3 * FOX_W
    g0 = head
    b0 = g0 + 3 * NSA_HEADS
    f0 = b0 + body
    m0 = f0 + FOX_HEADS
    proj_ref[:, 0:head] = w_ref[:, 0:head].astype(BF16)
    proj_ref[:, head:head + body] = w_ref[:, b0:b0 + body].astype(BF16)
    misc = jnp.concatenate([
        w_ref[:, f0:f0 + FOX_HEADS], jnp.zeros((rows, MISC_GATE_LANE - MISC_FOX_LANE - FOX_HEADS), F32),
        w_ref[:, g0:g0 + 3 * NSA_HEADS], jnp.zeros((rows, LANES - MISC_GATE_LANE - 3 * NSA_HEADS), F32)], axis=1)
    proj_ref[:, head + body:head + body + LANES] = misc.astype(BF16)
    gate_ref[...] = w_ref[:, m0:m0 + gate_ref.shape[1]].astype(BF16)


def _pack_mix_w_in(w, *, tr=256):
    L, D, C = w.shape
    gate_cols = C - (PROJ_COLS - LANES) - FOX_HEADS - 3 * NSA_HEADS
    assert D % tr == 0 and gate_cols % LANES == 0
    return pl.pallas_call(
        _pack_mix_kernel,
        grid=(L, D // tr),
        in_specs=[pl.BlockSpec((None, tr, C), lambda l, i: (l, i, 0))],
        out_specs=[pl.BlockSpec((None, tr, PROJ_COLS), lambda l, i: (l, i, 0)),
                   pl.BlockSpec((None, tr, gate_cols), lambda l, i: (l, i, 0))],
        out_shape=[jax.ShapeDtypeStruct((L, D, PROJ_COLS), BF16), jax.ShapeDtypeStruct((L, D, gate_cols), BF16)],
        compiler_params=_params("arbitrary", "arbitrary"),
        name="pack_mix_w_in",
    )(w)


def _pack_compress(pe, w1, w2):
    half = NSA_CMP_STRIDE
    hid = NSA_CMP_HIDDEN
    w1k = w1[0].reshape(NSA_CMP_LEN, HEAD_DIM, hid)
    w1v = w1[1].reshape(NSA_CMP_LEN, HEAD_DIM, hid)
    z = jnp.zeros((half, HEAD_DIM, hid), w1.dtype)

    def halfpack(lo):
        kk = jnp.concatenate([w1k[lo:lo + half], z], axis=-1)
        vv = jnp.concatenate([z, w1v[lo:lo + half]], axis=-1)
        return jnp.concatenate([kk, vv], axis=1).astype(BF16)

    def pepack(lo):
        return jnp.concatenate([pe[0, lo:lo + half], pe[1, lo:lo + half]], axis=-1)

    zz = jnp.zeros((hid, HEAD_DIM), w2.dtype)
    w2p = jnp.concatenate([jnp.concatenate([w2[0], zz], axis=1),
                           jnp.concatenate([zz, w2[1]], axis=1)], axis=0).astype(BF16)
    return pepack(0), pepack(half), halfpack(0), halfpack(half), w2p


def _overlap_matrix_t(seq):
    n_chunk = seq // NSA_CMP_STRIDE
    n_cmp = (seq - NSA_CMP_LEN) // NSA_CMP_STRIDE + 1
    n_sel = seq // NSA_SEL_LEN
    cmp_start = np.arange(n_cmp) * NSA_CMP_STRIDE
    sel_start = np.arange(n_sel) * NSA_SEL_LEN
    ov = np.minimum(cmp_start[:, None] + NSA_CMP_LEN, sel_start[None, :] + NSA_SEL_LEN) \
        - np.maximum(cmp_start[:, None], sel_start[None, :])
    full = np.zeros((n_sel, n_chunk), np.float32)
    full[:, :n_cmp] = (np.clip(ov, 0, None) / NSA_CMP_LEN).T
    return jnp.asarray(full, dtype=BF16)


def kernel(x, c, ada_w, ada_b, norm_g, final_norm_g, ffn_w_in, ffn_w_out, mix_w_in, nsa_cmp_pe, nsa_cmp_w1,
           nsa_cmp_w2, fox_f_bias, br_w_nsa, br_w_dil, br_w_fox, mix_w_out):
    B, S, D = x.shape
    L = ada_w.shape[0]
    T = B * S
    mod_all = _ada_modulation(c, ada_w, ada_b).reshape(L, B, 9, D)
    tabs = _rope_tables(S)
    ovlt = _overlap_matrix_t(S)
    fg = final_norm_g.reshape(1, D)
    ffn_in, ffn_out = ffn_w_in.astype(BF16), ffn_w_out.astype(BF16)
    w_proj, w_gate = _pack_mix_w_in(mix_w_in)
    br_a, br_b, br_c = br_w_nsa.astype(BF16), br_w_dil.astype(BF16), br_w_fox.astype(BF16)
    w_mix_out = mix_w_out.astype(BF16)
    h = x.reshape(T, D)
    for l in range(L):
        mod = mod_all[l]
        h = _ffn(h, mod, norm_g[l, 0].reshape(1, D), ffn_in, ffn_out, fg,
                 layer=l, which=0, mod_base=0, final=False, seq=S)

        g1 = norm_g[l, 1].reshape(1, D)
        (nsa_qt, cmp_kv, sel_kv, sel_vt, win_kv, win_vt, dil_q, dil_k, dil_v, fox_q, fox_k, fox_vt,
         misc) = _mixer_proj(h, mod, g1, w_proj, tabs, layer=l, batch=B, seq=S, nsa_q=NSA_Q)

        def bsd(a):
            return a.reshape(B, S, a.shape[-1])

        misc = bsd(misc)
        cmp_out, cmp_vt = _compress(bsd(cmp_kv), *_pack_compress(nsa_cmp_pe[l], nsa_cmp_w1[l], nsa_cmp_w2[l]))
        y_a = _nsa_attention(nsa_qt, cmp_out, cmp_vt, bsd(sel_kv), sel_vt, bsd(win_kv), win_vt, misc, ovlt)
        dil_parts = _dilated_attention(bsd(dil_q), bsd(dil_k), bsd(dil_v))
        bias_row = jnp.pad(fox_f_bias[l].reshape(1, FOX_HEADS),
                           ((0, 0), (MISC_FOX_LANE, LANES - MISC_FOX_LANE - FOX_HEADS)))
        k_aug, qt_aug = _fox_prep(misc, bias_row, bsd(fox_q), bsd(fox_k))
        y_c = _fox_attention(qt_aug, k_aug, fox_vt)

        h = _merge(h, mod, g1, y_a.reshape(T, NSA_Q_W), y_c.reshape(T, FOX_W), dil_parts,
                   w_gate, br_a, br_b, br_c, w_mix_out, layer=l, seq=S)

        h = _ffn(h, mod, norm_g[l, 2].reshape(1, D), ffn_in, ffn_out, fg,
                 layer=l, which=1, mod_base=6, final=(l == L - 1), seq=S)
    return h.reshape(B, S, D)
```

```python
import functools
import math

import numpy as np
import jax
import jax.numpy as jnp
from jax import lax
from jax.experimental import pallas as pl
from jax.experimental.pallas import tpu as pltpu

F32 = jnp.float32
BF16 = jnp.bfloat16

HEAD_DIM = 64
ROPE_DIM = 16
ROPE_HALF = ROPE_DIM // 2
ROPE_THETA = 500000.0
Q_BLOCK = 128
NEG_INF = -1e30
RMS_EPS = 1e-6
QK_SCALE = HEAD_DIM ** -0.5
LOG2E = math.log2(math.e)

NSA_HEADS = 4
NSA_CMP_LEN = 32
NSA_CMP_STRIDE = 16
NSA_CMP_HIDDEN = 128
NSA_SEL_LEN = 64
NSA_SEL_TOPK = 16
NSA_WINDOW = 512
NSA_FORCE_SCORE = 1e4

DIL_PATTERNS = ((128, 1), (512, 4), (2048, 16))
DIL_HEADS_PER_GROUP = 2
DIL_HEADS = DIL_HEADS_PER_GROUP * len(DIL_PATTERNS)
FOX_HEADS = 6

NSA_Q_W = NSA_HEADS * HEAD_DIM
DIL_W = DIL_HEADS * HEAD_DIM
DIL_OUT_W = DIL_HEADS_PER_GROUP * HEAD_DIM
FOX_W = FOX_HEADS * HEAD_DIM

LANES = 128
SUBLANES = 8
MISC_FOX_LANE = 0
MISC_GATE_LANE = 8
PROJ_TM = 1024
FFN_TM = 1024
FOX_AUG = 3
DIL_BLOCKS_PER_STEP = 8
NSA_Q = 256

PROJ_GROUPS = (
    ("nsa_q", NSA_Q_W, "A", QK_SCALE * LOG2E),
    ("cmp_kv", LANES, "B", 1.0),
    ("sel_kv", LANES, "B", 1.0),
    ("win_kv", LANES, "B", 1.0),
    ("dil_q", DIL_W, "A", QK_SCALE * LOG2E),
    ("dil_k", DIL_W, "A", 1.0),
    ("dil_v", DIL_W, None, 1.0),
    ("fox_q", FOX_W, None, QK_SCALE * LOG2E),
    ("fox_k", FOX_W, None, 1.0),
    ("fox_v", FOX_W, None, 1.0),
    ("misc", LANES, None, 1.0),
)
PROJ_COLS = sum(g[1] for g in PROJ_GROUPS)

VMEM_LIMIT = 56 * 1024 * 1024


def _dot(a, b):
    return jnp.dot(a, b, preferred_element_type=F32)


def _dot_nt(a, b):
    return lax.dot_general(a, b, (((1,), (1,)), ((), ())), preferred_element_type=F32)


def _norm_modulate(x, g, shift, scale):
    ms = jnp.mean(x * x, axis=-1, keepdims=True)
    y = x * lax.rsqrt(ms + RMS_EPS) * g
    return y * (1.0 + scale) + shift


def _masked_softmax(s, mask):
    s = jnp.where(mask, s, NEG_INF)
    m = jnp.max(s, axis=-1, keepdims=True)
    e = jnp.where(mask, jnp.exp(s - m), 0.0)
    den = jnp.sum(e, axis=-1, keepdims=True)
    den = jnp.where(den > 0, den, 1.0)
    return e / den, m + jnp.log(den)


def _params(*sem):
    return pltpu.CompilerParams(dimension_semantics=sem, vmem_limit_bytes=VMEM_LIMIT)


def _ada_kernel(c_ref, w_ref, b_ref, o_ref):
    c = c_ref[...]
    cond = c * jax.nn.sigmoid(c)
    o_ref[...] = jnp.dot(cond, w_ref[...], preferred_element_type=F32,
                         precision=lax.Precision.HIGHEST) + b_ref[...]


def _ada_modulation(c, ada_w, ada_b):
    L, D, N = ada_w.shape
    B = c.shape[0]
    tn = 1152
    assert N % tn == 0
    return pl.pallas_call(
        _ada_kernel,
        grid=(L, N // tn),
        in_specs=[
            pl.BlockSpec((B, D), lambda l, j: (0, 0)),
            pl.BlockSpec((None, D, tn), lambda l, j: (l, 0, j)),
            pl.BlockSpec((None, 1, tn), lambda l, j: (l, 0, j)),
        ],
        out_specs=pl.BlockSpec((None, B, tn), lambda l, j: (l, 0, j)),
        out_shape=jax.ShapeDtypeStruct((L, B, N), F32),
        compiler_params=_params("arbitrary", "arbitrary"),
        name="ada_modulation",
    )(c, ada_w, ada_b.reshape(L, 1, N))


def _ffn_kernel(x_ref, mod_ref, g_ref, win_ref, wout_ref, fg_ref, o_ref, a_ref, *, mod_base, d_ff, chunk, final):
    x = x_ref[...]
    n = _norm_modulate(x, g_ref[...], mod_ref[mod_base:mod_base + 1, :],
                       mod_ref[mod_base + 1:mod_base + 2, :]).astype(BF16)
    for j in range(d_ff // chunk):
        gate = _dot(n, win_ref[:, j * chunk:(j + 1) * chunk])
        up = _dot(n, win_ref[:, d_ff + j * chunk:d_ff + (j + 1) * chunk])
        a_ref[:, j * chunk:(j + 1) * chunk] = (gate * jax.nn.sigmoid(gate) * up).astype(BF16)
    f = _dot(a_ref[...], wout_ref[...])
    out = x + (0.5 * mod_ref[mod_base + 2:mod_base + 3, :]) * f
    if final:
        ms = jnp.mean(out * out, axis=-1, keepdims=True)
        out = out * lax.rsqrt(ms + RMS_EPS) * fg_ref[...]
    o_ref[...] = out


def _ffn(h, mod, g, w_in, w_out, final_g, *, layer, which, mod_base, final, seq, tm=FFN_TM):
    T, D = h.shape
    d_ff = w_out.shape[2]
    chunk = 256
    assert T % tm == 0 and seq % tm == 0 and d_ff % chunk == 0
    tpb = seq // tm
    kern = functools.partial(_ffn_kernel, mod_base=mod_base, d_ff=d_ff, chunk=chunk, final=final)
    return pl.pallas_call(
        kern,
        grid=(T // tm,),
        in_specs=[
            pl.BlockSpec((tm, D), lambda i: (i, 0)),
            pl.BlockSpec((None, 9, D), lambda i: (i // tpb, 0, 0)),
            pl.BlockSpec((1, D), lambda i: (0, 0)),
            pl.BlockSpec((None, None, D, 2 * d_ff), lambda i: (layer, which, 0, 0), pipeline_mode=pl.Buffered(1)),
            pl.BlockSpec((None, None, d_ff, D), lambda i: (layer, which, 0, 0), pipeline_mode=pl.Buffered(1)),
            pl.BlockSpec((1, D), lambda i: (0, 0)),
        ],
        out_specs=pl.BlockSpec((tm, D), lambda i: (i, 0)),
        out_shape=jax.ShapeDtypeStruct((T, D), F32),
        scratch_shapes=[pltpu.VMEM((tm, d_ff), BF16)],
        compiler_params=_params("arbitrary"),
        name="ffn",
    )(h, mod, g, w_in, w_out, final_g)


def _rope_group(v, c, s1, s2):
    return v * c + pltpu.roll(v, LANES - ROPE_HALF, 1) * s1 + pltpu.roll(v, ROPE_HALF, 1) * s2


def _proj_kernel(x_ref, mod_ref, g_ref, w_ref, tc_ref, ts1_ref, ts2_ref,
                 nsa_qt_ref, cmp_ref, sel_ref, selvt_ref, win_ref, winvt_ref,
                 dq_ref, dk_ref, dv_ref, fq_ref, fk_ref, fvt_ref, misc_ref):
    x = x_ref[...]
    tm = x.shape[0]
    nblk = tm // Q_BLOCK
    n = _norm_modulate(x, g_ref[...], mod_ref[3:4, :], mod_ref[4:5, :]).astype(BF16)

    wide = {}

    def group(name, sub):
        off = 0
        for gname, width, rope, scale in PROJ_GROUPS:
            if gname == name:
                break
            off += width
        lo = off + sub * LANES
        blk = lo // (2 * LANES)
        if blk not in wide:
            wide[blk] = _dot(n, w_ref[:, blk * 2 * LANES:(blk + 1) * 2 * LANES])
        v = wide[blk][:, lo % (2 * LANES):lo % (2 * LANES) + LANES]
        if rope is not None:
            t0 = 0 if rope == "A" else LANES
            v = _rope_group(v, tc_ref[:, t0:t0 + LANES], ts1_ref[:, t0:t0 + LANES], ts2_ref[:, t0:t0 + LANES])
        if scale != 1.0:
            v = v * scale
        return v

    nq = nsa_qt_ref.shape[2] // NSA_HEADS
    zero_half = jnp.zeros((HEAD_DIM, NSA_HEADS * nq), BF16)
    for u in range(tm // nq):
        nsa_qt_ref[u, HEAD_DIM:2 * HEAD_DIM, :] = zero_half
    for sub in range(NSA_Q_W // LANES):
        vt = group("nsa_q", sub).T.astype(BF16)
        for hh in range(2):
            h = 2 * sub + hh
            for u in range(tm // nq):
                nsa_qt_ref[u, 0:HEAD_DIM, h * nq:(h + 1) * nq] = \
                    vt[hh * HEAD_DIM:(hh + 1) * HEAD_DIM, u * nq:(u + 1) * nq]

    cmp_ref[...] = group("cmp_kv", 0)
    for name, kv_ref, vt_ref in (("sel_kv", sel_ref, selvt_ref), ("win_kv", win_ref, winvt_ref)):
        v = group(name, 0)
        kv_ref[...] = v.astype(BF16)
        vt = v.T.astype(BF16)
        for u in range(nblk):
            vt_ref[u] = vt[HEAD_DIM:2 * HEAD_DIM, u * Q_BLOCK:(u + 1) * Q_BLOCK]

    for name, o_ref in (("dil_q", dq_ref), ("dil_k", dk_ref), ("dil_v", dv_ref), ("fox_q", fq_ref),
                        ("fox_k", fk_ref)):
        for sub in range(o_ref.shape[1] // LANES):
            o_ref[:, sub * LANES:(sub + 1) * LANES] = group(name, sub).astype(o_ref.dtype)

    for sub in range(FOX_W // LANES):
        vt = group("fox_v", sub).T.astype(BF16)
        for hh in range(2):
            for u in range(nblk):
                fvt_ref[2 * sub + hh, u] = vt[hh * HEAD_DIM:(hh + 1) * HEAD_DIM, u * Q_BLOCK:(u + 1) * Q_BLOCK]

    misc_ref[...] = group("misc", 0)


def _mixer_proj(h, mod, g, w, tabs, *, layer, batch, seq, nsa_q, tm=PROJ_TM):
    T, D = h.shape
    assert T % tm == 0 and seq % tm == 0 and tm % Q_BLOCK == 0 and tm % nsa_q == 0
    tpb = seq // tm
    nblk = tm // Q_BLOCK
    nqb = seq // Q_BLOCK
    tab_spec = pl.BlockSpec((tm, 2 * LANES), lambda i: (i % tpb, 0))

    def flat(width):
        return pl.BlockSpec((tm, width), lambda i: (i, 0))

    vt_spec = pl.BlockSpec((None, nblk, HEAD_DIM, Q_BLOCK), lambda i: (i // tpb, i % tpb, 0, 0))
    out_specs = [
        pl.BlockSpec((None, tm // nsa_q, 2 * HEAD_DIM, NSA_HEADS * nsa_q), lambda i: (i // tpb, i % tpb, 0, 0)),
        flat(LANES), flat(LANES), vt_spec, flat(LANES), vt_spec,
        flat(DIL_W), flat(DIL_W), flat(DIL_W), flat(FOX_W), flat(FOX_W),
        pl.BlockSpec((None, FOX_HEADS, nblk, HEAD_DIM, Q_BLOCK), lambda i: (i // tpb, 0, i % tpb, 0, 0)),
        flat(LANES),
    ]
    vt_shape = jax.ShapeDtypeStruct((batch, nqb, HEAD_DIM, Q_BLOCK), BF16)
    out_shape = [
        jax.ShapeDtypeStruct((batch, seq // nsa_q, 2 * HEAD_DIM, NSA_HEADS * nsa_q), BF16),
        jax.ShapeDtypeStruct((T, LANES), F32),
        jax.ShapeDtypeStruct((T, LANES), BF16), vt_shape,
        jax.ShapeDtypeStruct((T, LANES), BF16), vt_shape,
        jax.ShapeDtypeStruct((T, DIL_W), F32), jax.ShapeDtypeStruct((T, DIL_W), F32),
        jax.ShapeDtypeStruct((T, DIL_W), F32),
        jax.ShapeDtypeStruct((T, FOX_W), BF16), jax.ShapeDtypeStruct((T, FOX_W), BF16),
        jax.ShapeDtypeStruct((batch, FOX_HEADS, nqb, HEAD_DIM, Q_BLOCK), BF16),
        jax.ShapeDtypeStruct((T, LANES), F32),
    ]
    return pl.pallas_call(
        _proj_kernel,
        grid=(T // tm,),
        in_specs=[
            pl.BlockSpec((tm, D), lambda i: (i, 0)),
            pl.BlockSpec((None, 9, D), lambda i: (i // tpb, 0, 0)),
            pl.BlockSpec((1, D), lambda i: (0, 0)),
            pl.BlockSpec((None, D, PROJ_COLS), lambda i: (layer, 0, 0), pipeline_mode=pl.Buffered(1)),
            tab_spec, tab_spec, tab_spec,
        ],
        out_specs=out_specs,
        out_shape=out_shape,
        compiler_params=_params("arbitrary"),
        name="mixer_proj",
    )(h, mod, g, w, *tabs)


def _compress_kernel(x_ref, pet_ref, peb_ref, w1t_ref, w1b_ref, w2_ref, o_ref, vt_ref):
    nrow = x_ref.shape[0] // NSA_CMP_STRIDE
    a = jnp.zeros((nrow, 2 * NSA_CMP_HIDDEN), F32)
    b = jnp.zeros((nrow, 2 * NSA_CMP_HIDDEN), F32)
    for t in range(NSA_CMP_STRIDE):
        xt = x_ref[pl.ds(t, nrow, stride=NSA_CMP_STRIDE), :]
        a = a + _dot((xt + pet_ref[t:t + 1, :]).astype(BF16), w1t_ref[t])
        b = b + _dot((xt + peb_ref[t:t + 1, :]).astype(BF16), w1b_ref[t])
    hid = a + pltpu.roll(b, nrow - 1, 0)
    hid = hid * jax.nn.sigmoid(hid)
    out = _dot(hid.astype(BF16), w2_ref[...])
    o_ref[...] = out.astype(o_ref.dtype)
    vt_ref[...] = out.T[HEAD_DIM:2 * HEAD_DIM, :].astype(vt_ref.dtype)


def _compress(cmp_kv, pe_top, pe_bot, w1_top, w1_bot, w2):
    B, S, _ = cmp_kv.shape
    half = NSA_CMP_STRIDE
    nchunk = S // half
    hw = 2 * NSA_CMP_HIDDEN
    return pl.pallas_call(
        _compress_kernel,
        grid=(B,),
        in_specs=[
            pl.BlockSpec((None, S, LANES), lambda b: (b, 0, 0)),
            pl.BlockSpec((half, LANES), lambda b: (0, 0)),
            pl.BlockSpec((half, LANES), lambda b: (0, 0)),
            pl.BlockSpec((half, LANES, hw), lambda b: (0, 0, 0)),
            pl.BlockSpec((half, LANES, hw), lambda b: (0, 0, 0)),
            pl.BlockSpec((hw, LANES), lambda b: (0, 0)),
        ],
        out_specs=[
            pl.BlockSpec((None, nchunk, LANES), lambda b: (b, 0, 0)),
            pl.BlockSpec((None, HEAD_DIM, nchunk), lambda b: (b, 0, 0)),
        ],
        out_shape=[
            jax.ShapeDtypeStruct((B, nchunk, LANES), BF16),
            jax.ShapeDtypeStruct((B, HEAD_DIM, nchunk), BF16),
        ],
        compiler_params=_params("arbitrary"),
        name="nsa_compress",
    )(cmp_kv, pe_top, pe_bot, w1_top, w1_bot, w2)


def _nsa_kernel(qt_ref, cmp_ref, cmpvt_ref, sel_ref, selvt_ref, win_ref, winvt_ref, misc_ref, ovlt_ref,
                o_ref, member_ref, cnt_ref, sa_ref, sb_ref, m_ref, l_ref, acc_ref, *, seq, tk):
    H = NSA_HEADS
    Q = qt_ref.shape[1] // H
    VB = Q_BLOCK
    HQ = H * Q
    q0 = pl.program_id(1) * Q
    qt = qt_ref[...]
    ncmp = cmp_ref.shape[0]
    n_sel = seq // NSA_SEL_LEN
    sel_shift = NSA_SEL_LEN.bit_length() - 1

    def heads(a):
        return jnp.concatenate([a] * H, axis=1)

    wlen = NSA_WINDOW + Q
    start = pl.multiple_of(jnp.maximum(q0 - NSA_WINDOW, 0), Q)
    s = _dot(win_ref[pl.ds(start, wlen), :], qt)
    kpos = start + lax.broadcasted_iota(jnp.int32, (wlen, Q), 0)
    tw = q0 + lax.broadcasted_iota(jnp.int32, (wlen, Q), 1)
    s = s + heads(jnp.where((kpos <= tw) & (kpos > tw - NSA_WINDOW), 0.0, NEG_INF))
    e = jnp.exp2(s - jnp.max(s, axis=0, keepdims=True))
    sblk = start // VB
    vt = jnp.concatenate([winvt_ref[sblk + u] for u in range(wlen // VB)], axis=1)
    o_win = _dot(vt, e.astype(BF16)) / jnp.sum(e, axis=0, keepdims=True)

    s = _dot(cmp_ref[...], qt)
    nn = lax.broadcasted_iota(jnp.int32, (ncmp, Q), 0)
    tt = q0 + lax.broadcasted_iota(jnp.int32, (ncmp, Q), 1)
    cmask = heads((nn * NSA_CMP_STRIDE + (NSA_CMP_LEN - 1) <= tt) & (nn < ncmp - 1))
    s = jnp.where(cmask, s, NEG_INF)
    m = jnp.max(s, axis=0, keepdims=True)
    e = jnp.where(cmask, jnp.exp2(s - m), 0.0)
    den = jnp.sum(e, axis=0, keepdims=True)
    p_cmp = e / jnp.where(den > 0, den, 1.0)
    o_cmp = _dot(cmpvt_ref[...], p_cmp.astype(BF16))

    psum = p_cmp[:, 0:Q] + p_cmp[:, Q:2 * Q] + p_cmp[:, 2 * Q:3 * Q] + p_cmp[:, 3 * Q:4 * Q]
    hi = psum.astype(BF16)
    lo = (psum - hi.astype(F32)).astype(BF16)
    ovlt = ovlt_ref[...]
    imp = _dot(ovlt, hi) + _dot(ovlt, lo)
    jj = lax.broadcasted_iota(jnp.int32, (n_sel, Q), 0)
    tq = q0 + lax.broadcasted_iota(jnp.int32, (n_sel, Q), 1)
    valid = jj * NSA_SEL_LEN <= tq
    forced = (jj == (tq >> sel_shift)) | (jj == 0)
    imp = jnp.where(forced, NSA_FORCE_SCORE, jnp.where(valid, imp, -1.0))
    bpt = tk // NSA_SEL_LEN
    vpt = tk // VB
    n_full = q0 // tk

    def sel_scores(k, dst_ref):
        dst_ref[...] = _dot(sel_ref[pl.ds(pl.multiple_of(k * tk, tk), tk), :], qt)

    sel_scores(0, sa_ref)
    m_ref[...] = jnp.full((1, HQ), NEG_INF, F32)
    l_ref[...] = jnp.zeros(l_ref.shape, F32)
    acc_ref[...] = jnp.zeros(acc_ref.shape, F32)

    ngrp = n_sel // SUBLANES
    grp = [imp[r * SUBLANES:(r + 1) * SUBLANES, :] for r in range(ngrp)]
    jrow = lax.broadcasted_iota(jnp.int32, (SUBLANES, Q), 0)
    cnt_ref[...] = jnp.zeros((n_sel, Q), F32)
    last_started = (q0 + Q - 1) >> sel_shift
    for ib in range(ngrp):
        @pl.when(ib * SUBLANES <= last_started)
        def _(ib=ib):
            cnt = [cnt_ref[r * SUBLANES:(r + 1) * SUBLANES, :] for r in range(ngrp)]
            for i in range(ib * SUBLANES, (ib + 1) * SUBLANES):
                row = jnp.broadcast_to(imp[i:i + 1, :], (SUBLANES, Q))
                for r in range(ngrp):
                    if r > ib:
                        hit = jnp.where(row >= grp[r], 1.0, 0.0)
                    elif r < ib:
                        hit = jnp.where(row > grp[r], 1.0, 0.0)
                    else:
                        hit = jnp.where(jrow + r * SUBLANES > i, jnp.where(row >= grp[r], 1.0, 0.0),
                                        jnp.where(row > grp[r], 1.0, 0.0))
                    cnt[r] = cnt[r] + hit
            for r in range(ngrp):
                cnt_ref[r * SUBLANES:(r + 1) * SUBLANES, :] = cnt[r]
    top_k = min(NSA_SEL_TOPK, n_sel)
    member_ref[...] = jnp.where(cnt_ref[...] < top_k, 0.0, NEG_INF)

    def sel_update(k, src_ref, causal):
        bias = jnp.concatenate(
            [jnp.broadcast_to(member_ref[pl.ds(k * bpt + jb, 1), :], (NSA_SEL_LEN, Q)) for jb in range(bpt)],
            axis=0)
        if causal:
            kpos = k * tk + lax.broadcasted_iota(jnp.int32, (tk, Q), 0)
            tcol = q0 + lax.broadcasted_iota(jnp.int32, (tk, Q), 1)
            bias = jnp.where(kpos <= tcol, bias, NEG_INF)
        s = src_ref[...] + heads(bias)
        m = m_ref[...]
        m_new = jnp.maximum(m, jnp.max(s, axis=0, keepdims=True))
        e = jnp.exp2(s - m_new)
        alpha = jnp.exp2(m - m_new)
        l_ref[...] = alpha * l_ref[...] + jnp.sum(e, axis=0, keepdims=True)
        vt = jnp.concatenate([selvt_ref[k * vpt + u] for u in range(vpt)], axis=1)
        acc_ref[...] = alpha * acc_ref[...] + _dot(vt, e.astype(BF16))
        m_ref[...] = m_new

    def sel_pair(j, _):
        sel_scores(2 * j + 1, sb_ref)
        sel_update(2 * j, sa_ref, False)
        sel_scores(2 * j + 2, sa_ref)
        sel_update(2 * j + 1, sb_ref, False)
        return 0

    lax.fori_loop(0, n_full // 2, sel_pair, 0)

    @pl.when(n_full % 2 == 1)
    def _():
        sel_scores(n_full, sb_ref)
        sel_update(n_full - 1, sa_ref, False)
        sel_update(n_full, sb_ref, True)

    @pl.when(n_full % 2 == 0)
    def _():
        sel_update(n_full, sa_ref, True)

    o_sel = acc_ref[...] / l_ref[...]

    g = jax.nn.sigmoid(misc_ref[...].T)
    outs = []
    for h in range(H):
        r0 = MISC_GATE_LANE + 3 * h
        cols = slice(h * Q, (h + 1) * Q)
        outs.append(g[r0:r0 + 1, :] * o_cmp[:, cols] + g[r0 + 1:r0 + 2, :] * o_sel[:, cols]
                    + g[r0 + 2:r0 + 3, :] * o_win[:, cols])
    o_ref[...] = jnp.concatenate(outs, axis=0).T.astype(o_ref.dtype)


def _nsa_attention(qt, cmp_out, cmp_vt, sel_kv, sel_vt, win_kv, win_vt, misc, ovlt, *, tk=512):
    B, S, _ = sel_kv.shape
    ncmp = cmp_out.shape[1]
    n_sel = S // NSA_SEL_LEN
    nvb = S // Q_BLOCK
    nq = qt.shape[3] // NSA_HEADS
    assert S % tk == 0 and tk % nq == 0 and NSA_WINDOW % nq == 0 and nq % Q_BLOCK == 0
    assert S >= NSA_WINDOW + nq and n_sel % SUBLANES == 0
    kern = functools.partial(_nsa_kernel, seq=S, tk=tk)
    return pl.pallas_call(
        kern,
        grid=(B, S // nq),
        in_specs=[
            pl.BlockSpec((None, None, 2 * HEAD_DIM, NSA_HEADS * nq), lambda b, i: (b, i, 0, 0)),
            pl.BlockSpec((None, ncmp, LANES), lambda b, i: (b, 0, 0)),
            pl.BlockSpec((None, HEAD_DIM, ncmp), lambda b, i: (b, 0, 0)),
            pl.BlockSpec((None, S, LANES), lambda b, i: (b, 0, 0)),
            pl.BlockSpec((None, nvb, HEAD_DIM, Q_BLOCK), lambda b, i: (b, 0, 0, 0)),
            pl.BlockSpec((None, S, LANES), lambda b, i: (b, 0, 0)),
            pl.BlockSpec((None, nvb, HEAD_DIM, Q_BLOCK), lambda b, i: (b, 0, 0, 0)),
            pl.BlockSpec((None, nq, LANES), lambda b, i: (b, i, 0)),
            pl.BlockSpec((n_sel, ncmp), lambda b, i: (0, 0)),
        ],
        out_specs=pl.BlockSpec((None, nq, NSA_Q_W), lambda b, i: (b, i, 0)),
        out_shape=jax.ShapeDtypeStruct((B, S, NSA_Q_W), BF16),
        scratch_shapes=[
            pltpu.VMEM((n_sel, nq), F32),
            pltpu.VMEM((n_sel, nq), F32),
            pltpu.VMEM((tk, NSA_HEADS * nq), F32),
            pltpu.VMEM((tk, NSA_HEADS * nq), F32),
            pltpu.VMEM((1, NSA_HEADS * nq), F32),
            pltpu.VMEM((1, NSA_HEADS * nq), F32),
            pltpu.VMEM((HEAD_DIM, NSA_HEADS * nq), F32),
        ],
        compiler_params=_params("arbitrary", "arbitrary"),
        name="nsa_attention",
    )(qt, cmp_out, cmp_vt, sel_kv, sel_vt, win_kv, win_vt, misc, ovlt)


def _dil_kernel(q_ref, k_ref, v_ref, o_ref, lse_ref, vt_ref, *, band, dil, cps):
    Q = Q_BLOCK
    r0 = pl.program_id(1) * cps
    nblk = q_ref.shape[0] // (dil * Q)
    row = lax.broadcasted_iota(jnp.int32, (LANES, Q), 0)

    def rows(c, first, count):
        if dil == 1:
            return pl.ds(first, count)
        return pl.ds(first * dil + r0 + c, count, stride=dil)

    per_step = min(DIL_BLOCKS_PER_STEP // cps, nblk)

    def transpose_v(j, _):
        for c in range(cps):
            for u in range(per_step):
                jb = per_step * j + u
                vt_ref[c * nblk + jb] = v_ref[rows(c, jb * Q, Q), :].T.astype(BF16)
        return 0

    lax.fori_loop(0, nblk // per_step, transpose_v, 0)

    def scores(c, jq):
        kb = jnp.maximum(jq - 1, 0)
        qt = q_ref[rows(c, jq * Q, Q), :].T
        rhs = jnp.concatenate([jnp.where(row < HEAD_DIM, qt, 0.0), jnp.where(row >= HEAD_DIM, qt, 0.0)],
                              axis=1).astype(BF16)
        return _dot(k_ref[rows(c, kb * Q, 2 * Q), :].astype(BF16), rhs)

    def attend(c, jq, s):
        kb = jnp.maximum(jq - 1, 0)
        dist = (jq - kb) * Q + lax.broadcasted_iota(jnp.int32, (2 * Q, Q), 1) \
            - lax.broadcasted_iota(jnp.int32, (2 * Q, Q), 0)
        bias = jnp.where((dist >= 0) & (dist <= band), 0.0, NEG_INF)
        s = s + jnp.concatenate([bias, bias], axis=1)
        m = jnp.max(s, axis=0, keepdims=True)
        e = jnp.exp2(s - m)
        l = jnp.sum(e, axis=0, keepdims=True)
        eb = e.astype(BF16)
        vt = jnp.concatenate([vt_ref[c * nblk + kb], vt_ref[c * nblk + kb + 1]], axis=1)
        return m, l, _dot(vt[0:HEAD_DIM], eb[:, 0:Q]), _dot(vt[HEAD_DIM:2 * HEAD_DIM], eb[:, Q:2 * Q])

    def emit(c, jq, m, l, pv0, pv1):
        inv = 1.0 / l
        lse = m * (1.0 / LOG2E) + jnp.log(l)
        lt = jnp.concatenate([jnp.broadcast_to(lse[:, 0:Q], (HEAD_DIM, Q)),
                              jnp.broadcast_to(lse[:, Q:2 * Q], (HEAD_DIM, Q))], axis=0)
        ot = jnp.concatenate([pv0 * inv[:, 0:Q], pv1 * inv[:, Q:2 * Q]], axis=0)
        o_ref[rows(c, jq * Q, Q), :] = ot.T
        lse_ref[rows(c, jq * Q, Q), :] = lt.T

    def step(j, _):
        blocks = [(c, per_step * j + u) for c in range(cps) for u in range(per_step)]
        ss = [scores(c, jq) for c, jq in blocks]
        parts = [attend(c, jq, s) for (c, jq), s in zip(blocks, ss)]
        for (c, jq), part in zip(blocks, parts):
            emit(c, jq, *part)
        return 0

    lax.fori_loop(0, nblk // per_step, step, 0)


def _dilated_group(q, k, v, group, window, dil):
    B, S, W = q.shape
    nblk = S // dil // Q_BLOCK
    cps = min(dil, max(1, DIL_BLOCKS_PER_STEP // nblk))
    per_step = min(DIL_BLOCKS_PER_STEP // cps, nblk)
    assert dil & (dil - 1) == 0 and window // dil == Q_BLOCK and nblk >= 2
    assert dil % cps == 0 and nblk % per_step == 0
    in_spec = pl.BlockSpec((None, S, LANES), lambda b, r: (b, 0, group))
    out_spec = pl.BlockSpec((None, S, LANES), lambda b, r: (b, 0, 0))
    out_sds = jax.ShapeDtypeStruct((B, S, LANES), F32)
    o, lse = pl.pallas_call(
        functools.partial(_dil_kernel, band=window // dil, dil=dil, cps=cps),
        grid=(B, dil // cps),
        in_specs=[in_spec, in_spec, in_spec],
        out_specs=[out_spec, out_spec],
        out_shape=[out_sds, out_sds],
        scratch_shapes=[pltpu.VMEM((cps * nblk, LANES, Q_BLOCK), BF16)],
        compiler_params=_params("arbitrary", "arbitrary"),
        name=f"dilated_attention_d{dil}",
    )(q, k, v)
    return o.reshape(B * S, LANES), lse.reshape(B * S, LANES)


def _dilated_attention(q, k, v):
    outs = [_dilated_group(q, k, v, g, window, dil) for g, (window, dil) in enumerate(DIL_PATTERNS)]
    return [o for o, _ in outs] + [l for _, l in outs]


def _fox_prep_kernel(misc_ref, bias_ref, q_ref, k_ref, pq_ref, pk_ref, kaug_ref, qtaug_ref, carry_ref):
    tb = misc_ref.shape[0]

    @pl.when(pl.program_id(1) == 0)
    def _():
        carry_ref[...] = jnp.zeros_like(carry_ref)

    r = lax.broadcasted_iota(jnp.int32, (LANES, LANES), 0)
    c = lax.broadcasted_iota(jnp.int32, (LANES, LANES), 1)
    tri = jnp.where(r >= c, 1.0, 0.0).astype(F32)
    carry = carry_ref[...]
    cums = []
    for blk in range(tb // LANES):
        x = misc_ref[blk * LANES:(blk + 1) * LANES, :] + bias_ref[...]
        log_f = -(jnp.maximum(-x, 0.0) + jnp.log1p(jnp.exp(-jnp.abs(x))))
        cs = jnp.dot(tri, log_f, preferred_element_type=F32, precision=lax.Precision.HIGHEST) + carry
        cums.append(cs)
        carry = cs[LANES - 1:LANES, :]
    carry_ref[...] = carry
    lane = lax.broadcasted_iota(jnp.int32, (tb, LANES), 1)
    cum = jnp.where(lane < FOX_HEADS, jnp.concatenate(cums, axis=0) * LOG2E, 0.0)
    hi = cum.astype(BF16).astype(F32)
    r1 = cum - hi
    mid = r1.astype(BF16).astype(F32)
    lo = r1 - mid
    pieces = (hi + pltpu.roll(mid, FOX_HEADS, 1) + pltpu.roll(lo, 2 * FOX_HEADS, 1)
              + jnp.where(lane == FOX_AUG * FOX_HEADS, 1.0, 0.0)).astype(BF16)
    for h in range(FOX_HEADS):
        g0 = (h // 2) * LANES
        q_aug = _dot(jnp.concatenate([q_ref[:, g0:g0 + LANES], pieces], axis=1), pq_ref[h])
        k_aug = _dot(jnp.concatenate([k_ref[:, g0:g0 + LANES], pieces], axis=1), pk_ref[h])
        kaug_ref[h] = k_aug.astype(BF16)
        qtaug_ref[h] = q_aug.T.astype(BF16)


def _fox_placement():
    a0, nh = HEAD_DIM, FOX_HEADS
    pq = np.zeros((nh, 2 * LANES, LANES), np.float32)
    pk = np.zeros((nh, 2 * LANES, LANES), np.float32)
    one = LANES + FOX_AUG * nh
    for h in range(nh):
        for d in range(HEAD_DIM):
            pq[h, (h % 2) * HEAD_DIM + d, d] = 1.0
            pk[h, (h % 2) * HEAD_DIM + d, d] = 1.0
        for j in range(FOX_AUG):
            pq[h, LANES + j * nh + h, a0 + j] = 1.0
            pq[h, one, a0 + FOX_AUG + j] = 1.0
            pk[h, one, a0 + j] = 1.0
            pk[h, LANES + j * nh + h, a0 + FOX_AUG + j] = -1.0
    return jnp.asarray(pq, BF16), jnp.asarray(pk, BF16)


def _fox_prep(misc, bias_row, q, k, *, tb=PROJ_TM):
    B, S, _ = misc.shape
    assert S % tb == 0
    place_spec = pl.BlockSpec((FOX_HEADS, 2 * LANES, LANES), lambda b, i: (0, 0, 0))
    return pl.pallas_call(
        _fox_prep_kernel,
        grid=(B, S // tb),
        in_specs=[
            pl.BlockSpec((None, tb, LANES), lambda b, i: (b, i, 0)),
            pl.BlockSpec((1, LANES), lambda b, i: (0, 0)),
            pl.BlockSpec((None, tb, FOX_W), lambda b, i: (b, i, 0)),
            pl.BlockSpec((None, tb, FOX_W), lambda b, i: (b, i, 0)),
            place_spec, place_spec,
        ],
        out_specs=[
            pl.BlockSpec((None, FOX_HEADS, tb, LANES), lambda b, i: (b, 0, i, 0)),
            pl.BlockSpec((None, FOX_HEADS, LANES, tb), lambda b, i: (b, 0, 0, i)),
        ],
        out_shape=[
            jax.ShapeDtypeStruct((B, FOX_HEADS, S, LANES), BF16),
            jax.ShapeDtypeStruct((B, FOX_HEADS, LANES, S), BF16),
        ],
        scratch_shapes=[pltpu.VMEM((1, LANES), F32)],
        compiler_params=_params("arbitrary", "arbitrary"),
        name="fox_prep",
    )(misc, bias_row, q, k, *_fox_placement())


def _fox_kernel(qt_ref, k_ref, vt_ref, o_ref, sa_ref, sb_ref, m_ref, l_ref, acc_ref, *, tq, tk):
    nh = qt_ref.shape[0]
    q0 = pl.program_id(2) * tq
    n_full = q0 // tk
    vpt = tk // Q_BLOCK

    def scores(k, dst_ref):
        base = pl.multiple_of(k * tk, tk)
        for h in range(nh):
            dst_ref[h] = _dot(k_ref[h, pl.ds(base, tk), :], qt_ref[h])

    def update(k, src_ref, causal):
        if causal:
            kpos = k * tk + lax.broadcasted_iota(jnp.int32, (tk, tq), 0)
            tcol = q0 + lax.broadcasted_iota(jnp.int32, (tk, tq), 1)
            causal_bias = jnp.where(kpos <= tcol, 0.0, NEG_INF)
        for h in range(nh):
            s = src_ref[h]
            if causal:
                s = s + causal_bias
            m = m_ref[h]
            m_new = jnp.maximum(m, jnp.max(s, axis=0, keepdims=True))
            e = jnp.exp2(s - m_new)
            alpha = jnp.exp2(m - m_new)
            l_ref[h] = alpha * l_ref[h] + jnp.sum(e, axis=0, keepdims=True)
            vt = jnp.concatenate([vt_ref[h, k * vpt + u] for u in range(vpt)], axis=1)
            acc_ref[h] = alpha * acc_ref[h] + _dot(vt, e.astype(BF16))
            m_ref[h] = m_new

    scores(0, sa_ref)
    m_ref[...] = jnp.full(m_ref.shape, NEG_INF, F32)
    l_ref[...] = jnp.zeros(l_ref.shape, F32)
    acc_ref[...] = jnp.zeros(acc_ref.shape, F32)

    def pair(j, _):
        scores(2 * j + 1, sb_ref)
        update(2 * j, sa_ref, False)
        scores(2 * j + 2, sa_ref)
        update(2 * j + 1, sb_ref, False)
        return 0

    lax.fori_loop(0, n_full // 2, pair, 0)

    @pl.when(n_full % 2 == 1)
    def _():
        scores(n_full, sb_ref)
        update(n_full - 1, sa_ref, False)
        update(n_full, sb_ref, True)

    @pl.when(n_full % 2 == 0)
    def _():
        update(n_full, sa_ref, True)

    outs = [acc_ref[h] / l_ref[h] for h in range(nh)]
    o_ref[...] = jnp.concatenate(outs, axis=0).T.astype(o_ref.dtype)


def _fox_attention(qt_aug, k_aug, vt, *, tq=512, tk=512, heads_per_step=FOX_HEADS):
    B, H, S, _ = k_aug.shape
    hps = heads_per_step
    assert S % tk == 0 and tk % tq == 0 and tk % Q_BLOCK == 0 and H % hps == 0 and (hps * HEAD_DIM) % LANES == 0
    nqb = S // Q_BLOCK
    kern = functools.partial(_fox_kernel, tq=tq, tk=tk)
    return pl.pallas_call(
        kern,
        grid=(B, H // hps, S // tq),
        in_specs=[
            pl.BlockSpec((None, hps, LANES, tq), lambda b, p, i: (b, p, 0, i)),
            pl.BlockSpec((None, hps, S, LANES), lambda b, p, i: (b, p, 0, 0)),
            pl.BlockSpec((None, hps, nqb, HEAD_DIM, Q_BLOCK), lambda b, p, i: (b, p, 0, 0, 0)),
        ],
        out_specs=pl.BlockSpec((None, tq, hps * HEAD_DIM), lambda b, p, i: (b, i, p)),
        out_shape=jax.ShapeDtypeStruct((B, S, H * HEAD_DIM), BF16),
        scratch_shapes=[
            pltpu.VMEM((hps, tk, tq), F32),
            pltpu.VMEM((hps, tk, tq), F32),
            pltpu.VMEM((hps, 1, tq), F32),
            pltpu.VMEM((hps, 1, tq), F32),
            pltpu.VMEM((hps, HEAD_DIM, tq), F32),
        ],
        compiler_params=_params("arbitrary", "arbitrary", "arbitrary"),
        name="fox_attention",
    )(qt_aug, k_aug, vt)


def _merge_kernel(x_ref, mod_ref, g_ref, ya_ref, yc_ref, d0_ref, d1_ref, d2_ref, l0_ref, l1_ref, l2_ref,
                  wg_ref, bra_ref, brb_ref, brc_ref, wo_ref, o_ref):
    x = x_ref[...]
    D = x.shape[1]
    n = _norm_modulate(x, g_ref[...], mod_ref[3:4, :], mod_ref[4:5, :]).astype(BF16)
    lse = [l0_ref[...], l1_ref[...], l2_ref[...]]
    mx = jnp.maximum(jnp.maximum(lse[0], lse[1]), lse[2])
    w = [jnp.exp(l - mx) for l in lse]
    wsum = w[0] + w[1] + w[2]
    yb = (w[0] / wsum) * d0_ref[...] + (w[1] / wsum) * d1_ref[...] + (w[2] / wsum) * d2_ref[...]
    merged = jax.nn.sigmoid(_dot(n, wg_ref[:, 0:D])) * _dot(ya_ref[...], bra_ref[...])
    merged = merged + jax.nn.sigmoid(_dot(n, wg_ref[:, D:2 * D])) * _dot(yb.astype(BF16), brb_ref[...])
    merged = merged + jax.nn.sigmoid(_dot(n, wg_ref[:, 2 * D:3 * D])) * _dot(yc_ref[...], brc_ref[...])
    o_ref[...] = x + mod_ref[5:6, :] * _dot(merged.astype(BF16), wo_ref[...])


def _merge(h, mod, g, ya, yc, dil_parts, w_gate, br_a, br_b, br_c, w_out, *, layer, seq, tm=PROJ_TM):
    T, D = h.shape
    assert T % tm == 0 and seq % tm == 0 and len(dil_parts) == 2 * len(DIL_PATTERNS)
    tpb = seq // tm

    def resident(shape):
        return pl.BlockSpec((None,) + shape, lambda i: (layer, 0, 0), pipeline_mode=pl.Buffered(1))

    dil_spec = pl.BlockSpec((tm, DIL_OUT_W), lambda i: (i, 0))
    return pl.pallas_call(
        _merge_kernel,
        grid=(T // tm,),
        in_specs=[
            pl.BlockSpec((tm, D), lambda i: (i, 0)),
            pl.BlockSpec((None, 9, D), lambda i: (i // tpb, 0, 0)),
            pl.BlockSpec((1, D), lambda i: (0, 0)),
            pl.BlockSpec((tm, NSA_Q_W), lambda i: (i, 0)),
            pl.BlockSpec((tm, FOX_W), lambda i: (i, 0)),
            dil_spec, dil_spec, dil_spec, dil_spec, dil_spec, dil_spec,
            resident((D, 3 * D)),
            resident((NSA_Q_W, D)),
            resident((DIL_OUT_W, D)),
            resident((FOX_W, D)),
            resident((D, D)),
        ],
        out_specs=pl.BlockSpec((tm, D), lambda i: (i, 0)),
        out_shape=jax.ShapeDtypeStruct((T, D), F32),
        compiler_params=_params("arbitrary"),
        name="merge_out",
    )(h, mod, g, ya, yc, *dil_parts, w_gate, br_a, br_b, br_c, w_out)


def _rope_tables(seq):
    inv_freq = ROPE_THETA ** (-jnp.arange(0, ROPE_DIM, 2, dtype=F32) / ROPE_DIM)
    ang = jnp.arange(seq, dtype=F32)[:, None] * inv_freq[None, :]
    cos, sin = jnp.cos(ang), jnp.sin(ang)
    d = np.arange(LANES) % HEAD_DIM
    idx = d % ROPE_HALF
    first = jnp.asarray(d < ROPE_HALF)
    second = jnp.asarray((d >= ROPE_HALF) & (d < ROPE_DIM))
    c_a = jnp.where(first | second, cos[:, idx], 1.0)
    s1_a = jnp.where(first, -sin[:, idx], 0.0)
    s2_a = jnp.where(second, sin[:, idx], 0.0)
    head0 = jnp.asarray(np.arange(LANES) < HEAD_DIM)
    c_b = jnp.where(head0, c_a, 1.0)
    s1_b = jnp.where(head0, s1_a, 0.0)
    s2_b = jnp.where(head0, s2_a, 0.0)
    return (jnp.concatenate([c_a, c_b], axis=1), jnp.concatenate([s1_a, s1_b], axis=1),
            jnp.concatenate([s2_a, s2_b], axis=1))


def _pack_mix_kernel(w_ref, proj_ref, gate_ref):
    rows = w_ref.shape[0]
    head = NSA_Q_W + 6 * HEAD_DIM
    body = 3 * DIL_W + 3 * FOX_W
    g0 = head
    b0 = g0 + 3 * NSA_HEADS
    f0 = b0 + body
    m0 = f0 + FOX_HEADS
    proj_ref[:, 0:head] = w_ref[:, 0:head].astype(BF16)
    proj_ref[:, head:head + body] = w_ref[:, b0:b0 + body].astype(BF16)
    misc = jnp.concatenate([
        w_ref[:, f0:f0 + FOX_HEADS], jnp.zeros((rows, MISC_GATE_LANE - MISC_FOX_LANE - FOX_HEADS), F32),
        w_ref[:, g0:g0 + 3 * NSA_HEADS], jnp.zeros((rows, LANES - MISC_GATE_LANE - 3 * NSA_HEADS), F32)], axis=1)
    proj_ref[:, head + body:head + body + LANES] = misc.astype(BF16)
    gate_ref[...] = w_ref[:, m0:m0 + gate_ref.shape[1]].astype(BF16)


def _pack_mix_w_in(w, *, tr=256):
    L, D, C = w.shape
    gate_cols = C - (PROJ_COLS - LANES) - FOX_HEADS - 3 * NSA_HEADS
    assert D % tr == 0 and gate_cols % LANES == 0
    return pl.pallas_call(
        _pack_mix_kernel,
        grid=(L, D // tr),
        in_specs=[pl.BlockSpec((None, tr, C), lambda l, i: (l, i, 0))],
        out_specs=[pl.BlockSpec((None, tr, PROJ_COLS), lambda l, i: (l, i, 0)),
                   pl.BlockSpec((None, tr, gate_cols), lambda l, i: (l, i, 0))],
        out_shape=[jax.ShapeDtypeStruct((L, D, PROJ_COLS), BF16), jax.ShapeDtypeStruct((L, D, gate_cols), BF16)],
        compiler_params=_params("arbitrary", "arbitrary"),
        name="pack_mix_w_in",
    )(w)


def _pack_compress(pe, w1, w2):
    half = NSA_CMP_STRIDE
    hid = NSA_CMP_HIDDEN
    w1k = w1[0].reshape(NSA_CMP_LEN, HEAD_DIM, hid)
    w1v = w1[1].reshape(NSA_CMP_LEN, HEAD_DIM, hid)
    z = jnp.zeros((half, HEAD_DIM, hid), w1.dtype)

    def halfpack(lo):
        kk = jnp.concatenate([w1k[lo:lo + half], z], axis=-1)
        vv = jnp.concatenate([z, w1v[lo:lo + half]], axis=-1)
        return jnp.concatenate([kk, vv], axis=1).astype(BF16)

    def pepack(lo):
        return jnp.concatenate([pe[0, lo:lo + half], pe[1, lo:lo + half]], axis=-1)

    zz = jnp.zeros((hid, HEAD_DIM), w2.dtype)
    w2p = jnp.concatenate([jnp.concatenate([w2[0], zz], axis=1),
                           jnp.concatenate([zz, w2[1]], axis=1)], axis=0).astype(BF16)
    return pepack(0), pepack(half), halfpack(0), halfpack(half), w2p


def _overlap_matrix_t(seq):
    n_chunk = seq // NSA_CMP_STRIDE
    n_cmp = (seq - NSA_CMP_LEN) // NSA_CMP_STRIDE + 1
    n_sel = seq // NSA_SEL_LEN
    cmp_start = np.arange(n_cmp) * NSA_CMP_STRIDE
    sel_start = np.arange(n_sel) * NSA_SEL_LEN
    ov = np.minimum(cmp_start[:, None] + NSA_CMP_LEN, sel_start[None, :] + NSA_SEL_LEN) \
        - np.maximum(cmp_start[:, None], sel_start[None, :])
    full = np.zeros((n_sel, n_chunk), np.float32)
    full[:, :n_cmp] = (np.clip(ov, 0, None) / NSA_CMP_LEN).T
    return jnp.asarray(full, dtype=BF16)


def kernel(x, c, ada_w, ada_b, norm_g, final_norm_g, ffn_w_in, ffn_w_out, mix_w_in, nsa_cmp_pe, nsa_cmp_w1,
           nsa_cmp_w2, fox_f_bias, br_w_nsa, br_w_dil, br_w_fox, mix_w_out):
    B, S, D = x.shape
    L = ada_w.shape[0]
    T = B * S
    mod_all = _ada_modulation(c, ada_w, ada_b).reshape(L, B, 9, D)
    tabs = _rope_tables(S)
    ovlt = _overlap_matrix_t(S)
    fg = final_norm_g.reshape(1, D)
    ffn_in, ffn_out = ffn_w_in.astype(BF16), ffn_w_out.astype(BF16)
    w_proj, w_gate = _pack_mix_w_in(mix_w_in)
    br_a, br_b, br_c = br_w_nsa.astype(BF16), br_w_dil.astype(BF16), br_w_fox.astype(BF16)
    w_mix_out = mix_w_out.astype(BF16)
    h = x.reshape(T, D)
    for l in range(L):
        mod = mod_all[l]
        h = _ffn(h, mod, norm_g[l, 0].reshape(1, D), ffn_in, ffn_out, fg,
                 layer=l, which=0, mod_base=0, final=False, seq=S)

        g1 = norm_g[l, 1].reshape(1, D)
        (nsa_qt, cmp_kv, sel_kv, sel_vt, win_kv, win_vt, dil_q, dil_k, dil_v, fox_q, fox_k, fox_vt,
         misc) = _mixer_proj(h, mod, g1, w_proj, tabs, layer=l, batch=B, seq=S, nsa_q=NSA_Q)

        def bsd(a):
            return a.reshape(B, S, a.shape[-1])

        misc = bsd(misc)
        cmp_out, cmp_vt = _compress(bsd(cmp_kv), *_pack_compress(nsa_cmp_pe[l], nsa_cmp_w1[l], nsa_cmp_w2[l]))
        y_a = _nsa_attention(nsa_qt, cmp_out, cmp_vt, bsd(sel_kv), sel_vt, bsd(win_kv), win_vt, misc, ovlt)
        dil_parts = _dilated_attention(bsd(dil_q), bsd(dil_k), bsd(dil_v))
        bias_row = jnp.pad(fox_f_bias[l].reshape(1, FOX_HEADS),
                           ((0, 0), (MISC_FOX_LANE, LANES - MISC_FOX_LANE - FOX_HEADS)))
        k_aug, qt_aug = _fox_prep(misc, bias_row, bsd(fox_q), bsd(fox_k))
        y_c = _fox_attention(qt_aug, k_aug, fox_vt)

        h = _merge(h, mod, g1, y_a.reshape(T, NSA_Q_W), y_c.reshape(T, FOX_W), dil_parts,
                   w_gate, br_a, br_b, br_c, w_mix_out, layer=l, seq=S)

        h = _ffn(h, mod, norm_g[l, 2].reshape(1, D), ffn_in, ffn_out, fg,
                 layer=l, which=1, mod_base=6, final=(l == L - 1), seq=S)
    return h.reshape(B, S, D)
```

```python
import functools
import math

import numpy as np
import jax
import jax.numpy as jnp
from jax import lax
from jax.experimental import pallas as pl
from jax.experimental.pallas import tpu as pltpu

F32 = jnp.float32
BF16 = jnp.bfloat16

HEAD_DIM = 64
ROPE_DIM = 16
ROPE_HALF = ROPE_DIM // 2
ROPE_THETA = 500000.0
Q_BLOCK = 128
NEG_INF = -1e30
RMS_EPS = 1e-6
QK_SCALE = HEAD_DIM ** -0.5
LOG2E = math.log2(math.e)

NSA_HEADS = 4
NSA_CMP_LEN = 32
NSA_CMP_STRIDE = 16
NSA_CMP_HIDDEN = 128
NSA_SEL_LEN = 64
NSA_SEL_TOPK = 16
NSA_WINDOW = 512
NSA_FORCE_SCORE = 1e4

DIL_PATTERNS = ((128, 1), (512, 4), (2048, 16))
DIL_HEADS_PER_GROUP = 2
DIL_HEADS = DIL_HEADS_PER_GROUP * len(DIL_PATTERNS)
FOX_HEADS = 6

NSA_Q_W = NSA_HEADS * HEAD_DIM
DIL_W = DIL_HEADS * HEAD_DIM
DIL_OUT_W = DIL_HEADS_PER_GROUP * HEAD_DIM
FOX_W = FOX_HEADS * HEAD_DIM

LANES = 128
SUBLANES = 8
MISC_FOX_LANE = 0
MISC_GATE_LANE = 8
PROJ_TM = 1024
FFN_TM = 1024
FOX_AUG = 3
DIL_BLOCKS_PER_STEP = 16
NSA_Q = 512

PROJ_GROUPS = (
    ("nsa_q", NSA_Q_W, "A", QK_SCALE * LOG2E),
    ("cmp_kv", LANES, "B", 1.0),
    ("sel_kv", LANES, "B", 1.0),
    ("win_kv", LANES, "B", 1.0),
    ("dil_q", DIL_W, "A", QK_SCALE * LOG2E),
    ("dil_k", DIL_W, "A", 1.0),
    ("dil_v", DIL_W, None, 1.0),
    ("fox_q", FOX_W, None, QK_SCALE * LOG2E),
    ("fox_k", FOX_W, None, 1.0),
    ("fox_v", FOX_W, None, 1.0),
    ("misc", LANES, None, 1.0),
)
PROJ_COLS = sum(g[1] for g in PROJ_GROUPS)

VMEM_LIMIT = 56 * 1024 * 1024


def _dot(a, b):
    return jnp.dot(a, b, preferred_element_type=F32)


def _dot_nt(a, b):
    return lax.dot_general(a, b, (((1,), (1,)), ((), ())), preferred_element_type=F32)


def _norm_modulate(x, g, shift, scale):
    ms = jnp.mean(x * x, axis=-1, keepdims=True)
    y = x * lax.rsqrt(ms + RMS_EPS) * g
    return y * (1.0 + scale) + shift


def _masked_softmax(s, mask):
    s = jnp.where(mask, s, NEG_INF)
    m = jnp.max(s, axis=-1, keepdims=True)
    e = jnp.where(mask, jnp.exp(s - m), 0.0)
    den = jnp.sum(e, axis=-1, keepdims=True)
    den = jnp.where(den > 0, den, 1.0)
    return e / den, m + jnp.log(den)


def _params(*sem):
    return pltpu.CompilerParams(dimension_semantics=sem, vmem_limit_bytes=VMEM_LIMIT)


def _ada_kernel(c_ref, w_ref, b_ref, o_ref):
    c = c_ref[...]
    cond = c * jax.nn.sigmoid(c)
    o_ref[...] = jnp.dot(cond, w_ref[...], preferred_element_type=F32,
                         precision=lax.Precision.HIGHEST) + b_ref[...]


def _ada_modulation(c, ada_w, ada_b):
    L, D, N = ada_w.shape
    B = c.shape[0]
    tn = 1152
    assert N % tn == 0
    return pl.pallas_call(
        _ada_kernel,
        grid=(L, N // tn),
        in_specs=[
            pl.BlockSpec((B, D), lambda l, j: (0, 0)),
            pl.BlockSpec((None, D, tn), lambda l, j: (l, 0, j)),
            pl.BlockSpec((None, 1, tn), lambda l, j: (l, 0, j)),
        ],
        out_specs=pl.BlockSpec((None, B, tn), lambda l, j: (l, 0, j)),
        out_shape=jax.ShapeDtypeStruct((L, B, N), F32),
        compiler_params=_params("arbitrary", "arbitrary"),
        name="ada_modulation",
    )(c, ada_w, ada_b.reshape(L, 1, N))


def _ffn_kernel(x_ref, mod_ref, g_ref, win_ref, wout_ref, fg_ref, o_ref, a_ref, *, mod_base, d_ff, chunk, final):
    x = x_ref[...]
    n = _norm_modulate(x, g_ref[...], mod_ref[mod_base:mod_base + 1, :],
                       mod_ref[mod_base + 1:mod_base + 2, :]).astype(BF16)
    for j in range(d_ff // chunk):
        gate = _dot(n, win_ref[:, j * chunk:(j + 1) * chunk])
        up = _dot(n, win_ref[:, d_ff + j * chunk:d_ff + (j + 1) * chunk])
        a_ref[:, j * chunk:(j + 1) * chunk] = (gate * jax.nn.sigmoid(gate) * up).astype(BF16)
    f = _dot(a_ref[...], wout_ref[...])
    out = x + (0.5 * mod_ref[mod_base + 2:mod_base + 3, :]) * f
    if final:
        ms = jnp.mean(out * out, axis=-1, keepdims=True)
        out = out * lax.rsqrt(ms + RMS_EPS) * fg_ref[...]
    o_ref[...] = out


def _ffn(h, mod, g, w_in, w_out, final_g, *, layer, which, mod_base, final, seq, tm=FFN_TM):
    T, D = h.shape
    d_ff = w_out.shape[2]
    chunk = 256
    assert T % tm == 0 and seq % tm == 0 and d_ff % chunk == 0
    tpb = seq // tm
    kern = functools.partial(_ffn_kernel, mod_base=mod_base, d_ff=d_ff, chunk=chunk, final=final)
    return pl.pallas_call(
        kern,
        grid=(T // tm,),
        in_specs=[
            pl.BlockSpec((tm, D), lambda i: (i, 0)),
            pl.BlockSpec((None, 9, D), lambda i: (i // tpb, 0, 0)),
            pl.BlockSpec((1, D), lambda i: (0, 0)),
            pl.BlockSpec((None, None, D, 2 * d_ff), lambda i: (layer, which, 0, 0), pipeline_mode=pl.Buffered(1)),
            pl.BlockSpec((None, None, d_ff, D), lambda i: (layer, which, 0, 0), pipeline_mode=pl.Buffered(1)),
            pl.BlockSpec((1, D), lambda i: (0, 0)),
        ],
        out_specs=pl.BlockSpec((tm, D), lambda i: (i, 0)),
        out_shape=jax.ShapeDtypeStruct((T, D), F32),
        scratch_shapes=[pltpu.VMEM((tm, d_ff), BF16)],
        compiler_params=_params("arbitrary"),
        name="ffn",
    )(h, mod, g, w_in, w_out, final_g)


def _rope_group(v, c, s1, s2):
    return v * c + pltpu.roll(v, LANES - ROPE_HALF, 1) * s1 + pltpu.roll(v, ROPE_HALF, 1) * s2


def _proj_kernel(x_ref, mod_ref, g_ref, w_ref, tc_ref, ts1_ref, ts2_ref,
                 nsa_qt_ref, cmp_ref, sel_ref, selvt_ref, win_ref, winvt_ref,
                 dq_ref, dk_ref, dv_ref, fq_ref, fk_ref, fvt_ref, misc_ref):
    x = x_ref[...]
    tm = x.shape[0]
    nblk = tm // Q_BLOCK
    n = _norm_modulate(x, g_ref[...], mod_ref[3:4, :], mod_ref[4:5, :]).astype(BF16)

    wide = {}

    def group(name, sub):
        off = 0
        for gname, width, rope, scale in PROJ_GROUPS:
            if gname == name:
                break
            off += width
        lo = off + sub * LANES
        blk = lo // (2 * LANES)
        if blk not in wide:
            wide[blk] = _dot(n, w_ref[:, blk * 2 * LANES:(blk + 1) * 2 * LANES])
        v = wide[blk][:, lo % (2 * LANES):lo % (2 * LANES) + LANES]
        if rope is not None:
            t0 = 0 if rope == "A" else LANES
            v = _rope_group(v, tc_ref[:, t0:t0 + LANES], ts1_ref[:, t0:t0 + LANES], ts2_ref[:, t0:t0 + LANES])
        if scale != 1.0:
            v = v * scale
        return v

    nq = nsa_qt_ref.shape[2] // NSA_HEADS
    zero_half = jnp.zeros((HEAD_DIM, NSA_HEADS * nq), BF16)
    for u in range(tm // nq):
        nsa_qt_ref[u, HEAD_DIM:2 * HEAD_DIM, :] = zero_half
    for sub in range(NSA_Q_W // LANES):
        vt = group("nsa_q", sub).T.astype(BF16)
        for hh in range(2):
            h = 2 * sub + hh
            for u in range(tm // nq):
                nsa_qt_ref[u, 0:HEAD_DIM, h * nq:(h + 1) * nq] = \
                    vt[hh * HEAD_DIM:(hh + 1) * HEAD_DIM, u * nq:(u + 1) * nq]

    cmp_ref[...] = group("cmp_kv", 0)
    for name, kv_ref, vt_ref in (("sel_kv", sel_ref, selvt_ref), ("win_kv", win_ref, winvt_ref)):
        v = group(name, 0)
        kv_ref[...] = v.astype(BF16)
        vt = v.T.astype(BF16)
        for u in range(nblk):
            vt_ref[u] = vt[HEAD_DIM:2 * HEAD_DIM, u * Q_BLOCK:(u + 1) * Q_BLOCK]

    for name, o_ref in (("dil_q", dq_ref), ("dil_k", dk_ref), ("dil_v", dv_ref), ("fox_q", fq_ref),
                        ("fox_k", fk_ref)):
        for sub in range(o_ref.shape[1] // LANES):
            o_ref[:, sub * LANES:(sub + 1) * LANES] = group(name, sub).astype(o_ref.dtype)

    for sub in range(FOX_W // LANES):
        vt = group("fox_v", sub).T.astype(BF16)
        for hh in range(2):
            for u in range(nblk):
                fvt_ref[2 * sub + hh, u] = vt[hh * HEAD_DIM:(hh + 1) * HEAD_DIM, u * Q_BLOCK:(u + 1) * Q_BLOCK]

    misc_ref[...] = group("misc", 0)


def _mixer_proj(h, mod, g, w, tabs, *, layer, batch, seq, nsa_q, tm=PROJ_TM):
    T, D = h.shape
    assert T % tm == 0 and seq % tm == 0 and tm % Q_BLOCK == 0 and tm % nsa_q == 0
    tpb = seq // tm
    nblk = tm // Q_BLOCK
    nqb = seq // Q_BLOCK
    tab_spec = pl.BlockSpec((tm, 2 * LANES), lambda i: (i % tpb, 0))

    def flat(width):
        return pl.BlockSpec((tm, width), lambda i: (i, 0))

    vt_spec = pl.BlockSpec((None, nblk, HEAD_DIM, Q_BLOCK), lambda i: (i // tpb, i % tpb, 0, 0))
    out_specs = [
        pl.BlockSpec((None, tm // nsa_q, 2 * HEAD_DIM, NSA_HEADS * nsa_q), lambda i: (i // tpb, i % tpb, 0, 0)),
        flat(LANES), flat(LANES), vt_spec, flat(LANES), vt_spec,
        flat(DIL_W), flat(DIL_W), flat(DIL_W), flat(FOX_W), flat(FOX_W),
        pl.BlockSpec((None, FOX_HEADS, nblk, HEAD_DIM, Q_BLOCK), lambda i: (i // tpb, 0, i % tpb, 0, 0)),
        flat(LANES),
    ]
    vt_shape = jax.ShapeDtypeStruct((batch, nqb, HEAD_DIM, Q_BLOCK), BF16)
    out_shape = [
        jax.ShapeDtypeStruct((batch, seq // nsa_q, 2 * HEAD_DIM, NSA_HEADS * nsa_q), BF16),
        jax.ShapeDtypeStruct((T, LANES), F32),
        jax.ShapeDtypeStruct((T, LANES), BF16), vt_shape,
        jax.ShapeDtypeStruct((T, LANES), BF16), vt_shape,
        jax.ShapeDtypeStruct((T, DIL_W), F32), jax.ShapeDtypeStruct((T, DIL_W), F32),
        jax.ShapeDtypeStruct((T, DIL_W), F32),
        jax.ShapeDtypeStruct((T, FOX_W), BF16), jax.ShapeDtypeStruct((T, FOX_W), BF16),
        jax.ShapeDtypeStruct((batch, FOX_HEADS, nqb, HEAD_DIM, Q_BLOCK), BF16),
        jax.ShapeDtypeStruct((T, LANES), F32),
    ]
    return pl.pallas_call(
        _proj_kernel,
        grid=(T // tm,),
        in_specs=[
            pl.BlockSpec((tm, D), lambda i: (i, 0)),
            pl.BlockSpec((None, 9, D), lambda i: (i // tpb, 0, 0)),
            pl.BlockSpec((1, D), lambda i: (0, 0)),
            pl.BlockSpec((None, D, PROJ_COLS), lambda i: (layer, 0, 0), pipeline_mode=pl.Buffered(1)),
            tab_spec, tab_spec, tab_spec,
        ],
        out_specs=out_specs,
        out_shape=out_shape,
        compiler_params=_params("arbitrary"),
        name="mixer_proj",
    )(h, mod, g, w, *tabs)


def _compress_kernel(x_ref, pet_ref, peb_ref, w1t_ref, w1b_ref, w2_ref, o_ref, vt_ref):
    nrow = x_ref.shape[0] // NSA_CMP_STRIDE
    a = jnp.zeros((nrow, 2 * NSA_CMP_HIDDEN), F32)
    b = jnp.zeros((nrow, 2 * NSA_CMP_HIDDEN), F32)
    for t in range(NSA_CMP_STRIDE):
        xt = x_ref[pl.ds(t, nrow, stride=NSA_CMP_STRIDE), :]
        a = a + _dot((xt + pet_ref[t:t + 1, :]).astype(BF16), w1t_ref[t])
        b = b + _dot((xt + peb_ref[t:t + 1, :]).astype(BF16), w1b_ref[t])
    hid = a + pltpu.roll(b, nrow - 1, 0)
    hid = hid * jax.nn.sigmoid(hid)
    out = _dot(hid.astype(BF16), w2_ref[...])
    o_ref[...] = out.astype(o_ref.dtype)
    vt_ref[...] = out.T[HEAD_DIM:2 * HEAD_DIM, :].astype(vt_ref.dtype)


def _compress(cmp_kv, pe_top, pe_bot, w1_top, w1_bot, w2):
    B, S, _ = cmp_kv.shape
    half = NSA_CMP_STRIDE
    nchunk = S // half
    hw = 2 * NSA_CMP_HIDDEN
    return pl.pallas_call(
        _compress_kernel,
        grid=(B,),
        in_specs=[
            pl.BlockSpec((None, S, LANES), lambda b: (b, 0, 0)),
            pl.BlockSpec((half, LANES), lambda b: (0, 0)),
            pl.BlockSpec((half, LANES), lambda b: (0, 0)),
            pl.BlockSpec((half, LANES, hw), lambda b: (0, 0, 0)),
            pl.BlockSpec((half, LANES, hw), lambda b: (0, 0, 0)),
            pl.BlockSpec((hw, LANES), lambda b: (0, 0)),
        ],
        out_specs=[
            pl.BlockSpec((None, nchunk, LANES), lambda b: (b, 0, 0)),
            pl.BlockSpec((None, HEAD_DIM, nchunk), lambda b: (b, 0, 0)),
        ],
        out_shape=[
            jax.ShapeDtypeStruct((B, nchunk, LANES), BF16),
            jax.ShapeDtypeStruct((B, HEAD_DIM, nchunk), BF16),
        ],
        compiler_params=_params("arbitrary"),
        name="nsa_compress",
    )(cmp_kv, pe_top, pe_bot, w1_top, w1_bot, w2)


def _nsa_kernel(qt_ref, cmp_ref, cmpvt_ref, sel_ref, selvt_ref, win_ref, winvt_ref, misc_ref, ovlt_ref,
                o_ref, member_ref, cnt_ref, sa_ref, sb_ref, m_ref, l_ref, acc_ref, *, seq, tk):
    H = NSA_HEADS
    Q = qt_ref.shape[1] // H
    VB = Q_BLOCK
    HQ = H * Q
    q0 = pl.program_id(1) * Q
    qt = qt_ref[...]
    ncmp = cmp_ref.shape[0]
    n_sel = seq // NSA_SEL_LEN
    sel_shift = NSA_SEL_LEN.bit_length() - 1

    def heads(a):
        return jnp.concatenate([a] * H, axis=1)

    wlen = NSA_WINDOW + Q
    start = pl.multiple_of(jnp.maximum(q0 - NSA_WINDOW, 0), Q)
    s = _dot(win_ref[pl.ds(start, wlen), :], qt)
    kpos = start + lax.broadcasted_iota(jnp.int32, (wlen, Q), 0)
    tw = q0 + lax.broadcasted_iota(jnp.int32, (wlen, Q), 1)
    s = s + heads(jnp.where((kpos <= tw) & (kpos > tw - NSA_WINDOW), 0.0, NEG_INF))
    e = jnp.exp2(s - jnp.max(s, axis=0, keepdims=True))
    sblk = start // VB
    vt = jnp.concatenate([winvt_ref[sblk + u] for u in range(wlen // VB)], axis=1)
    o_win = _dot(vt, e.astype(BF16)) / jnp.sum(e, axis=0, keepdims=True)

    s = _dot(cmp_ref[...], qt)
    nn = lax.broadcasted_iota(jnp.int32, (ncmp, Q), 0)
    tt = q0 + lax.broadcasted_iota(jnp.int32, (ncmp, Q), 1)
    cmask = heads((nn * NSA_CMP_STRIDE + (NSA_CMP_LEN - 1) <= tt) & (nn < ncmp - 1))
    s = jnp.where(cmask, s, NEG_INF)
    m = jnp.max(s, axis=0, keepdims=True)
    e = jnp.where(cmask, jnp.exp2(s - m), 0.0)
    den = jnp.sum(e, axis=0, keepdims=True)
    p_cmp = e / jnp.where(den > 0, den, 1.0)
    o_cmp = _dot(cmpvt_ref[...], p_cmp.astype(BF16))

    psum = p_cmp[:, 0:Q] + p_cmp[:, Q:2 * Q] + p_cmp[:, 2 * Q:3 * Q] + p_cmp[:, 3 * Q:4 * Q]
    hi = psum.astype(BF16)
    lo = (psum - hi.astype(F32)).astype(BF16)
    ovlt = ovlt_ref[...]
    imp = _dot(ovlt, hi) + _dot(ovlt, lo)
    jj = lax.broadcasted_iota(jnp.int32, (n_sel, Q), 0)
    tq = q0 + lax.broadcasted_iota(jnp.int32, (n_sel, Q), 1)
    valid = jj * NSA_SEL_LEN <= tq
    forced = (jj == (tq >> sel_shift)) | (jj == 0)
    imp = jnp.where(forced, NSA_FORCE_SCORE, jnp.where(valid, imp, -1.0))
    bpt = tk // NSA_SEL_LEN
    vpt = tk // VB
    n_full = q0 // tk

    def sel_scores(k, dst_ref):
        dst_ref[...] = _dot(sel_ref[pl.ds(pl.multiple_of(k * tk, tk), tk), :], qt)

    sel_scores(0, sa_ref)
    m_ref[...] = jnp.full((1, HQ), NEG_INF, F32)
    l_ref[...] = jnp.zeros(l_ref.shape, F32)
    acc_ref[...] = jnp.zeros(acc_ref.shape, F32)

    ngrp = n_sel // SUBLANES
    grp = [imp[r * SUBLANES:(r + 1) * SUBLANES, :] for r in range(ngrp)]
    jrow = lax.broadcasted_iota(jnp.int32, (SUBLANES, Q), 0)
    cnt_ref[...] = jnp.zeros((n_sel, Q), F32)
    last_started = (q0 + Q - 1) >> sel_shift
    for ib in range(ngrp):
        @pl.when(ib * SUBLANES <= last_started)
        def _(ib=ib):
            cnt = [cnt_ref[r * SUBLANES:(r + 1) * SUBLANES, :] for r in range(ngrp)]
            for i in range(ib * SUBLANES, (ib + 1) * SUBLANES):
                row = jnp.broadcast_to(imp[i:i + 1, :], (SUBLANES, Q))
                for r in range(ngrp):
                    if r > ib:
                        hit = jnp.where(row >= grp[r], 1.0, 0.0)
                    elif r < ib:
                        hit = jnp.where(row > grp[r], 1.0, 0.0)
                    else:
                        hit = jnp.where(jrow + r * SUBLANES > i, jnp.where(row >= grp[r], 1.0, 0.0),
                                        jnp.where(row > grp[r], 1.0, 0.0))
                    cnt[r] = cnt[r] + hit
            for r in range(ngrp):
                cnt_ref[r * SUBLANES:(r + 1) * SUBLANES, :] = cnt[r]
    top_k = min(NSA_SEL_TOPK, n_sel)
    member_ref[...] = jnp.where(cnt_ref[...] < top_k, 0.0, NEG_INF)

    def sel_update(k, src_ref, causal):
        bias = jnp.concatenate(
            [jnp.broadcast_to(member_ref[pl.ds(k * bpt + jb, 1), :], (NSA_SEL_LEN, Q)) for jb in range(bpt)],
            axis=0)
        if causal:
            kpos = k * tk + lax.broadcasted_iota(jnp.int32, (tk, Q), 0)
            tcol = q0 + lax.broadcasted_iota(jnp.int32, (tk, Q), 1)
            bias = jnp.where(kpos <= tcol, bias, NEG_INF)
        s = src_ref[...] + heads(bias)
        m = m_ref[...]
        m_new = jnp.maximum(m, jnp.max(s, axis=0, keepdims=True))
        e = jnp.exp2(s - m_new)
        alpha = jnp.exp2(m - m_new)
        l_ref[...] = alpha * l_ref[...] + jnp.sum(e, axis=0, keepdims=True)
        vt = jnp.concatenate([selvt_ref[k * vpt + u] for u in range(vpt)], axis=1)
        acc_ref[...] = alpha * acc_ref[...] + _dot(vt, e.astype(BF16))
        m_ref[...] = m_new

    def sel_pair(j, _):
        sel_scores(2 * j + 1, sb_ref)
        sel_update(2 * j, sa_ref, False)
        sel_scores(2 * j + 2, sa_ref)
        sel_update(2 * j + 1, sb_ref, False)
        return 0

    lax.fori_loop(0, n_full // 2, sel_pair, 0)

    @pl.when(n_full % 2 == 1)
    def _():
        sel_scores(n_full, sb_ref)
        sel_update(n_full - 1, sa_ref, False)
        sel_update(n_full, sb_ref, True)

    @pl.when(n_full % 2 == 0)
    def _():
        sel_update(n_full, sa_ref, True)

    o_sel = acc_ref[...] / l_ref[...]

    g = jax.nn.sigmoid(misc_ref[...].T)
    outs = []
    for h in range(H):
        r0 = MISC_GATE_LANE + 3 * h
        cols = slice(h * Q, (h + 1) * Q)
        outs.append(g[r0:r0 + 1, :] * o_cmp[:, cols] + g[r0 + 1:r0 + 2, :] * o_sel[:, cols]
                    + g[r0 + 2:r0 + 3, :] * o_win[:, cols])
    o_ref[...] = jnp.concatenate(outs, axis=0).T.astype(o_ref.dtype)


def _nsa_attention(qt, cmp_out, cmp_vt, sel_kv, sel_vt, win_kv, win_vt, misc, ovlt, *, tk=512):
    B, S, _ = sel_kv.shape
    ncmp = cmp_out.shape[1]
    n_sel = S // NSA_SEL_LEN
    nvb = S // Q_BLOCK
    nq = qt.shape[3] // NSA_HEADS
    assert S % tk == 0 and tk % nq == 0 and NSA_WINDOW % nq == 0 and nq % Q_BLOCK == 0
    assert S >= NSA_WINDOW + nq and n_sel % SUBLANES == 0
    kern = functools.partial(_nsa_kernel, seq=S, tk=tk)
    return pl.pallas_call(
        kern,
        grid=(B, S // nq),
        in_specs=[
            pl.BlockSpec((None, None, 2 * HEAD_DIM, NSA_HEADS * nq), lambda b, i: (b, i, 0, 0)),
            pl.BlockSpec((None, ncmp, LANES), lambda b, i: (b, 0, 0)),
            pl.BlockSpec((None, HEAD_DIM, ncmp), lambda b, i: (b, 0, 0)),
            pl.BlockSpec((None, S, LANES), lambda b, i: (b, 0, 0)),
            pl.BlockSpec((None, nvb, HEAD_DIM, Q_BLOCK), lambda b, i: (b, 0, 0, 0)),
            pl.BlockSpec((None, S, LANES), lambda b, i: (b, 0, 0)),
            pl.BlockSpec((None, nvb, HEAD_DIM, Q_BLOCK), lambda b, i: (b, 0, 0, 0)),
            pl.BlockSpec((None, nq, LANES), lambda b, i: (b, i, 0)),
            pl.BlockSpec((n_sel, ncmp), lambda b, i: (0, 0)),
        ],
        out_specs=pl.BlockSpec((None, nq, NSA_Q_W), lambda b, i: (b, i, 0)),
        out_shape=jax.ShapeDtypeStruct((B, S, NSA_Q_W), BF16),
        scratch_shapes=[
            pltpu.VMEM((n_sel, nq), F32),
            pltpu.VMEM((n_sel, nq), F32),
            pltpu.VMEM((tk, NSA_HEADS * nq), F32),
            pltpu.VMEM((tk, NSA_HEADS * nq), F32),
            pltpu.VMEM((1, NSA_HEADS * nq), F32),
            pltpu.VMEM((1, NSA_HEADS * nq), F32),
            pltpu.VMEM((HEAD_DIM, NSA_HEADS * nq), F32),
        ],
        compiler_params=_params("arbitrary", "arbitrary"),
        name="nsa_attention",
    )(qt, cmp_out, cmp_vt, sel_kv, sel_vt, win_kv, win_vt, misc, ovlt)


def _dil_kernel(q_ref, k_ref, v_ref, o_ref, lse_ref, vt_ref, *, band, dil, cps):
    Q = Q_BLOCK
    r0 = pl.program_id(1) * cps
    nblk = q_ref.shape[0] // (dil * Q)
    row = lax.broadcasted_iota(jnp.int32, (LANES, Q), 0)

    def rows(c, first, count):
        if dil == 1:
            return pl.ds(first, count)
        return pl.ds(first * dil + r0 + c, count, stride=dil)

    per_step = min(DIL_BLOCKS_PER_STEP // cps, nblk)

    def transpose_v(j, _):
        for c in range(cps):
            for u in range(per_step):
                jb = per_step * j + u
                vt_ref[c * nblk + jb] = v_ref[rows(c, jb * Q, Q), :].T.astype(BF16)
        return 0

    lax.fori_loop(0, nblk // per_step, transpose_v, 0)

    def scores(c, jq):
        kb = jnp.maximum(jq - 1, 0)
        qt = q_ref[rows(c, jq * Q, Q), :].T
        rhs = jnp.concatenate([jnp.where(row < HEAD_DIM, qt, 0.0), jnp.where(row >= HEAD_DIM, qt, 0.0)],
                              axis=1).astype(BF16)
        return _dot(k_ref[rows(c, kb * Q, 2 * Q), :].astype(BF16), rhs)

    def attend(c, jq, s):
        kb = jnp.maximum(jq - 1, 0)
        dist = (jq - kb) * Q + lax.broadcasted_iota(jnp.int32, (2 * Q, Q), 1) \
            - lax.broadcasted_iota(jnp.int32, (2 * Q, Q), 0)
        bias = jnp.where((dist >= 0) & (dist <= band), 0.0, NEG_INF)
        s = s + jnp.concatenate([bias, bias], axis=1)
        m = jnp.max(s, axis=0, keepdims=True)
        e = jnp.exp2(s - m)
        l = jnp.sum(e, axis=0, keepdims=True)
        eb = e.astype(BF16)
        vt = jnp.concatenate([vt_ref[c * nblk + kb], vt_ref[c * nblk + kb + 1]], axis=1)
        return m, l, _dot(vt[0:HEAD_DIM], eb[:, 0:Q]), _dot(vt[HEAD_DIM:2 * HEAD_DIM], eb[:, Q:2 * Q])

    def emit(c, jq, m, l, pv0, pv1):
        inv = 1.0 / l
        lse = m * (1.0 / LOG2E) + jnp.log(l)
        lt = jnp.concatenate([jnp.broadcast_to(lse[:, 0:Q], (HEAD_DIM, Q)),
                              jnp.broadcast_to(lse[:, Q:2 * Q], (HEAD_DIM, Q))], axis=0)
        ot = jnp.concatenate([pv0 * inv[:, 0:Q], pv1 * inv[:, Q:2 * Q]], axis=0)
        o_ref[rows(c, jq * Q, Q), :] = ot.T
        lse_ref[rows(c, jq * Q, Q), :] = lt.T

    def step(j, _):
        blocks = [(c, per_step * j + u) for c in range(cps) for u in range(per_step)]
        ss = [scores(c, jq) for c, jq in blocks]
        parts = [attend(c, jq, s) for (c, jq), s in zip(blocks, ss)]
        for (c, jq), part in zip(blocks, parts):
            emit(c, jq, *part)
        return 0

    lax.fori_loop(0, nblk // per_step, step, 0)


def _dilated_group(q, k, v, group, window, dil):
    B, S, W = q.shape
    nblk = S // dil // Q_BLOCK
    cps = min(dil, max(1, DIL_BLOCKS_PER_STEP // nblk))
    per_step = min(DIL_BLOCKS_PER_STEP // cps, nblk)
    assert dil & (dil - 1) == 0 and window // dil == Q_BLOCK and nblk >= 2
    assert dil % cps == 0 and nblk % per_step == 0
    in_spec = pl.BlockSpec((None, S, LANES), lambda b, r: (b, 0, group))
    out_spec = pl.BlockSpec((None, S, LANES), lambda b, r: (b, 0, 0))
    out_sds = jax.ShapeDtypeStruct((B, S, LANES), F32)
    o, lse = pl.pallas_call(
        functools.partial(_dil_kernel, band=window // dil, dil=dil, cps=cps),
        grid=(B, dil // cps),
        in_specs=[in_spec, in_spec, in_spec],
        out_specs=[out_spec, out_spec],
        out_shape=[out_sds, out_sds],
        scratch_shapes=[pltpu.VMEM((cps * nblk, LANES, Q_BLOCK), BF16)],
        compiler_params=_params("arbitrary", "arbitrary"),
        name=f"dilated_attention_d{dil}",
    )(q, k, v)
    return o.reshape(B * S, LANES), lse.reshape(B * S, LANES)


def _dilated_attention(q, k, v):
    outs = [_dilated_group(q, k, v, g, window, dil) for g, (window, dil) in enumerate(DIL_PATTERNS)]
    return [o for o, _ in outs] + [l for _, l in outs]


def _fox_prep_kernel(misc_ref, bias_ref, q_ref, k_ref, pq_ref, pk_ref, kaug_ref, qtaug_ref, carry_ref):
    tb = misc_ref.shape[0]

    @pl.when(pl.program_id(1) == 0)
    def _():
        carry_ref[...] = jnp.zeros_like(carry_ref)

    r = lax.broadcasted_iota(jnp.int32, (LANES, LANES), 0)
    c = lax.broadcasted_iota(jnp.int32, (LANES, LANES), 1)
    tri = jnp.where(r >= c, 1.0, 0.0).astype(F32)
    carry = carry_ref[...]
    cums = []
    for blk in range(tb // LANES):
        x = misc_ref[blk * LANES:(blk + 1) * LANES, :] + bias_ref[...]
        log_f = -(jnp.maximum(-x, 0.0) + jnp.log1p(jnp.exp(-jnp.abs(x))))
        cs = jnp.dot(tri, log_f, preferred_element_type=F32, precision=lax.Precision.HIGHEST) + carry
        cums.append(cs)
        carry = cs[LANES - 1:LANES, :]
    carry_ref[...] = carry
    lane = lax.broadcasted_iota(jnp.int32, (tb, LANES), 1)
    cum = jnp.where(lane < FOX_HEADS, jnp.concatenate(cums, axis=0) * LOG2E, 0.0)
    hi = cum.astype(BF16).astype(F32)
    r1 = cum - hi
    mid = r1.astype(BF16).astype(F32)
    lo = r1 - mid
    pieces = (hi + pltpu.roll(mid, FOX_HEADS, 1) + pltpu.roll(lo, 2 * FOX_HEADS, 1)
              + jnp.where(lane == FOX_AUG * FOX_HEADS, 1.0, 0.0)).astype(BF16)
    for h in range(FOX_HEADS):
        g0 = (h // 2) * LANES
        q_aug = _dot(jnp.concatenate([q_ref[:, g0:g0 + LANES], pieces], axis=1), pq_ref[h])
        k_aug = _dot(jnp.concatenate([k_ref[:, g0:g0 + LANES], pieces], axis=1), pk_ref[h])
        kaug_ref[h] = k_aug.astype(BF16)
        qtaug_ref[h] = q_aug.T.astype(BF16)


def _fox_placement():
    a0, nh = HEAD_DIM, FOX_HEADS
    pq = np.zeros((nh, 2 * LANES, LANES), np.float32)
    pk = np.zeros((nh, 2 * LANES, LANES), np.float32)
    one = LANES + FOX_AUG * nh
    for h in range(nh):
        for d in range(HEAD_DIM):
            pq[h, (h % 2) * HEAD_DIM + d, d] = 1.0
            pk[h, (h % 2) * HEAD_DIM + d, d] = 1.0
        for j in range(FOX_AUG):
            pq[h, LANES + j * nh + h, a0 + j] = 1.0
            pq[h, one, a0 + FOX_AUG + j] = 1.0
            pk[h, one, a0 + j] = 1.0
            pk[h, LANES + j * nh + h, a0 + FOX_AUG + j] = -1.0
    return jnp.asarray(pq, BF16), jnp.asarray(pk, BF16)


def _fox_prep(misc, bias_row, q, k, *, tb=PROJ_TM):
    B, S, _ = misc.shape
    assert S % tb == 0
    place_spec = pl.BlockSpec((FOX_HEADS, 2 * LANES, LANES), lambda b, i: (0, 0, 0))
    return pl.pallas_call(
        _fox_prep_kernel,
        grid=(B, S // tb),
        in_specs=[
            pl.BlockSpec((None, tb, LANES), lambda b, i: (b, i, 0)),
            pl.BlockSpec((1, LANES), lambda b, i: (0, 0)),
            pl.BlockSpec((None, tb, FOX_W), lambda b, i: (b, i, 0)),
            pl.BlockSpec((None, tb, FOX_W), lambda b, i: (b, i, 0)),
            place_spec, place_spec,
        ],
        out_specs=[
            pl.BlockSpec((None, FOX_HEADS, tb, LANES), lambda b, i: (b, 0, i, 0)),
            pl.BlockSpec((None, FOX_HEADS, LANES, tb), lambda b, i: (b, 0, 0, i)),
        ],
        out_shape=[
            jax.ShapeDtypeStruct((B, FOX_HEADS, S, LANES), BF16),
            jax.ShapeDtypeStruct((B, FOX_HEADS, LANES, S), BF16),
        ],
        scratch_shapes=[pltpu.VMEM((1, LANES), F32)],
        compiler_params=_params("arbitrary", "arbitrary"),
        name="fox_prep",
    )(misc, bias_row, q, k, *_fox_placement())


def _fox_kernel(qt_ref, k_ref, vt_ref, o_ref, sa_ref, sb_ref, m_ref, l_ref, acc_ref, *, tq, tk):
    nh = qt_ref.shape[0]
    q0 = pl.program_id(2) * tq
    n_full = q0 // tk
    vpt = tk // Q_BLOCK

    def scores(k, dst_ref):
        base = pl.multiple_of(k * tk, tk)
        for h in range(nh):
            dst_ref[h] = _dot(k_ref[h, pl.ds(base, tk), :], qt_ref[h])

    def update(k, src_ref, causal):
        if causal:
            kpos = k * tk + lax.broadcasted_iota(jnp.int32, (tk, tq), 0)
            tcol = q0 + lax.broadcasted_iota(jnp.int32, (tk, tq), 1)
            causal_bias = jnp.where(kpos <= tcol, 0.0, NEG_INF)
        for h in range(nh):
            s = src_ref[h]
            if causal:
                s = s + causal_bias
            m = m_ref[h]
            m_new = jnp.maximum(m, jnp.max(s, axis=0, keepdims=True))
            e = jnp.exp2(s - m_new)
            alpha = jnp.exp2(m - m_new)
            l_ref[h] = alpha * l_ref[h] + jnp.sum(e, axis=0, keepdims=True)
            vt = jnp.concatenate([vt_ref[h, k * vpt + u] for u in range(vpt)], axis=1)
            acc_ref[h] = alpha * acc_ref[h] + _dot(vt, e.astype(BF16))
            m_ref[h] = m_new

    scores(0, sa_ref)
    m_ref[...] = jnp.full(m_ref.shape, NEG_INF, F32)
    l_ref[...] = jnp.zeros(l_ref.shape, F32)
    acc_ref[...] = jnp.zeros(acc_ref.shape, F32)

    def pair(j, _):
        scores(2 * j + 1, sb_ref)
        update(2 * j, sa_ref, False)
        scores(2 * j + 2, sa_ref)
        update(2 * j + 1, sb_ref, False)
        return 0

    lax.fori_loop(0, n_full // 2, pair, 0)

    @pl.when(n_full % 2 == 1)
    def _():
        scores(n_full, sb_ref)
        update(n_full - 1, sa_ref, False)
        update(n_full, sb_ref, True)

    @pl.when(n_full % 2 == 0)
    def _():
        update(n_full, sa_ref, True)

    outs = [acc_ref[h] / l_ref[h] for h in range(nh)]
    o_ref[...] = jnp.concatenate(outs, axis=0).T.astype(o_ref.dtype)


def _fox_attention(qt_aug, k_aug, vt, *, tq=512, tk=512, heads_per_step=FOX_HEADS):
    B, H, S, _ = k_aug.shape
    hps = heads_per_step
    assert S % tk == 0 and tk % tq == 0 and tk % Q_BLOCK == 0 and H % hps == 0 and (hps * HEAD_DIM) % LANES == 0
    nqb = S // Q_BLOCK
    kern = functools.partial(_fox_kernel, tq=tq, tk=tk)
    return pl.pallas_call(
        kern,
        grid=(B, H // hps, S // tq),
        in_specs=[
            pl.BlockSpec((None, hps, LANES, tq), lambda b, p, i: (b, p, 0, i)),
            pl.BlockSpec((None, hps, S, LANES), lambda b, p, i: (b, p, 0, 0)),
            pl.BlockSpec((None, hps, nqb, HEAD_DIM, Q_BLOCK), lambda b, p, i: (b, p, 0, 0, 0)),
        ],
        out_specs=pl.BlockSpec((None, tq, hps * HEAD_DIM), lambda b, p, i: (b, i, p)),
        out_shape=jax.ShapeDtypeStruct((B, S, H * HEAD_DIM), BF16),
        scratch_shapes=[
            pltpu.VMEM((hps, tk, tq), F32),
            pltpu.VMEM((hps, tk, tq), F32),
            pltpu.VMEM((hps, 1, tq), F32),
            pltpu.VMEM((hps, 1, tq), F32),
            pltpu.VMEM((hps, HEAD_DIM, tq), F32),
        ],
        compiler_params=_params("arbitrary", "arbitrary", "arbitrary"),
        name="fox_attention",
    )(qt_aug, k_aug, vt)


def _merge_kernel(x_ref, mod_ref, g_ref, ya_ref, yc_ref, d0_ref, d1_ref, d2_ref, l0_ref, l1_ref, l2_ref,
                  wg_ref, bra_ref, brb_ref, brc_ref, wo_ref, o_ref):
    x = x_ref[...]
    D = x.shape[1]
    n = _norm_modulate(x, g_ref[...], mod_ref[3:4, :], mod_ref[4:5, :]).astype(BF16)
    lse = [l0_ref[...], l1_ref[...], l2_ref[...]]
    mx = jnp.maximum(jnp.maximum(lse[0], lse[1]), lse[2])
    w = [jnp.exp(l - mx) for l in lse]
    wsum = w[0] + w[1] + w[2]
    yb = (w[0] / wsum) * d0_ref[...] + (w[1] / wsum) * d1_ref[...] + (w[2] / wsum) * d2_ref[...]
    merged = jax.nn.sigmoid(_dot(n, wg_ref[:, 0:D])) * _dot(ya_ref[...], bra_ref[...])
    merged = merged + jax.nn.sigmoid(_dot(n, wg_ref[:, D:2 * D])) * _dot(yb.astype(BF16), brb_ref[...])
    merged = merged + jax.nn.sigmoid(_dot(n, wg_ref[:, 2 * D:3 * D])) * _dot(yc_ref[...], brc_ref[...])
    o_ref[...] = x + mod_ref[5:6, :] * _dot(merged.astype(BF16), wo_ref[...])


def _merge(h, mod, g, ya, yc, dil_parts, w_gate, br_a, br_b, br_c, w_out, *, layer, seq, tm=PROJ_TM):
    T, D = h.shape
    assert T % tm == 0 and seq % tm == 0 and len(dil_parts) == 2 * len(DIL_PATTERNS)
    tpb = seq // tm

    def resident(shape):
        return pl.BlockSpec((None,) + shape, lambda i: (layer, 0, 0), pipeline_mode=pl.Buffered(1))

    dil_spec = pl.BlockSpec((tm, DIL_OUT_W), lambda i: (i, 0))
    return pl.pallas_call(
        _merge_kernel,
        grid=(T // tm,),
        in_specs=[
            pl.BlockSpec((tm, D), lambda i: (i, 0)),
            pl.BlockSpec((None, 9, D), lambda i: (i // tpb, 0, 0)),
            pl.BlockSpec((1, D), lambda i: (0, 0)),
            pl.BlockSpec((tm, NSA_Q_W), lambda i: (i, 0)),
            pl.BlockSpec((tm, FOX_W), lambda i: (i, 0)),
            dil_spec, dil_spec, dil_spec, dil_spec, dil_spec, dil_spec,
            resident((D, 3 * D)),
            resident((NSA_Q_W, D)),
            resident((DIL_OUT_W, D)),
            resident((FOX_W, D)),
            resident((D, D)),
        ],
        out_specs=pl.BlockSpec((tm, D), lambda i: (i, 0)),
        out_shape=jax.ShapeDtypeStruct((T, D), F32),
        compiler_params=_params("arbitrary"),
        name="merge_out",
    )(h, mod, g, ya, yc, *dil_parts, w_gate, br_a, br_b, br_c, w_out)


def _rope_tables(seq):
    inv_freq = ROPE_THETA ** (-jnp.arange(0, ROPE_DIM, 2, dtype=F32) / ROPE_DIM)
    ang = jnp.arange(seq, dtype=F32)[:, None] * inv_freq[None, :]
    cos, sin = jnp.cos(ang), jnp.sin(ang)
    d = np.arange(LANES) % HEAD_DIM
    idx = d % ROPE_HALF
    first = jnp.asarray(d < ROPE_HALF)
    second = jnp.asarray((d >= ROPE_HALF) & (d < ROPE_DIM))
    c_a = jnp.where(first | second, cos[:, idx], 1.0)
    s1_a = jnp.where(first, -sin[:, idx], 0.0)
    s2_a = jnp.where(second, sin[:, idx], 0.0)
    head0 = jnp.asarray(np.arange(LANES) < HEAD_DIM)
    c_b = jnp.where(head0, c_a, 1.0)
    s1_b = jnp.where(head0, s1_a, 0.0)
    s2_b = jnp.where(head0, s2_a, 0.0)
    return (jnp.concatenate([c_a, c_b], axis=1), jnp.concatenate([s1_a, s1_b], axis=1),
            jnp.concatenate([s2_a, s2_b], axis=1))


def _pack_mix_kernel(w_ref, proj_ref, gate_ref):
    rows = w_ref.shape[0]
    head = NSA_Q_W + 6 * HEAD_DIM
    body = 3 * DIL_W + 3 * FOX_W
    g0 = head
    b0 = g0 + 3 * NSA_HEADS
    f0 = b0 + body
    m0 = f0 + FOX_HEADS
    proj_ref[:, 0:head] = w_ref[:, 0:head].astype(BF16)
    proj_ref[:, head:head + body] = w_ref[:, b0:b0 + body].astype(BF16)
    misc = jnp.concatenate([
        w_ref[:, f0:f0 + FOX_HEADS], jnp.zeros((rows, MISC_GATE_LANE - MISC_FOX_LANE - FOX_HEADS), F32),
        w_ref[:, g0:g0 + 3 * NSA_HEADS], jnp.zeros((rows, LANES - MISC_GATE_LANE - 3 * NSA_HEADS), F32)], axis=1)
    proj_ref[:, head + body:head + body + LANES] = misc.astype(BF16)
    gate_ref[...] = w_ref[:, m0:m0 + gate_ref.shape[1]].astype(BF16)


def _pack_mix_w_in(w, *, tr=256):
    L, D, C = w.shape
    gate_cols = C - (PROJ_COLS - LANES) - FOX_HEADS - 3 * NSA_HEADS
    assert D % tr == 0 and gate_cols % LANES == 0
    return pl.pallas_call(
        _pack_mix_kernel,
        grid=(L, D // tr),
        in_specs=[pl.BlockSpec((None, tr, C), lambda l, i: (l, i, 0))],
        out_specs=[pl.BlockSpec((None, tr, PROJ_COLS), lambda l, i: (l, i, 0)),
                   pl.BlockSpec((None, tr, gate_cols), lambda l, i: (l, i, 0))],
        out_shape=[jax.ShapeDtypeStruct((L, D, PROJ_COLS), BF16), jax.ShapeDtypeStruct((L, D, gate_cols), BF16)],
        compiler_params=_params("arbitrary", "arbitrary"),
        name="pack_mix_w_in",
    )(w)


def _pack_compress(pe, w1, w2):
    half = NSA_CMP_STRIDE
    hid = NSA_CMP_HIDDEN
    w1k = w1[0].reshape(NSA_CMP_LEN, HEAD_DIM, hid)
    w1v = w1[1].reshape(NSA_CMP_LEN, HEAD_DIM, hid)
    z = jnp.zeros((half, HEAD_DIM, hid), w1.dtype)

    def halfpack(lo):
        kk = jnp.concatenate([w1k[lo:lo + half], z], axis=-1)
        vv = jnp.concatenate([z, w1v[lo:lo + half]], axis=-1)
        return jnp.concatenate([kk, vv], axis=1).astype(BF16)

    def pepack(lo):
        return jnp.concatenate([pe[0, lo:lo + half], pe[1, lo:lo + half]], axis=-1)

    zz = jnp.zeros((hid, HEAD_DIM), w2.dtype)
    w2p = jnp.concatenate([jnp.concatenate([w2[0], zz], axis=1),
                           jnp.concatenate([zz, w2[1]], axis=1)], axis=0).astype(BF16)
    return pepack(0), pepack(half), halfpack(0), halfpack(half), w2p


def _overlap_matrix_t(seq):
    n_chunk = seq // NSA_CMP_STRIDE
    n_cmp = (seq - NSA_CMP_LEN) // NSA_CMP_STRIDE + 1
    n_sel = seq // NSA_SEL_LEN
    cmp_start = np.arange(n_cmp) * NSA_CMP_STRIDE
    sel_start = np.arange(n_sel) * NSA_SEL_LEN
    ov = np.minimum(cmp_start[:, None] + NSA_CMP_LEN, sel_start[None, :] + NSA_SEL_LEN) \
        - np.maximum(cmp_start[:, None], sel_start[None, :])
    full = np.zeros((n_sel, n_chunk), np.float32)
    full[:, :n_cmp] = (np.clip(ov, 0, None) / NSA_CMP_LEN).T
    return jnp.asarray(full, dtype=BF16)


def kernel(x, c, ada_w, ada_b, norm_g, final_norm_g, ffn_w_in, ffn_w_out, mix_w_in, nsa_cmp_pe, nsa_cmp_w1,
           nsa_cmp_w2, fox_f_bias, br_w_nsa, br_w_dil, br_w_fox, mix_w_out):
    B, S, D = x.shape
    L = ada_w.shape[0]
    T = B * S
    mod_all = _ada_modulation(c, ada_w, ada_b).reshape(L, B, 9, D)
    tabs = _rope_tables(S)
    ovlt = _overlap_matrix_t(S)
    fg = final_norm_g.reshape(1, D)
    ffn_in, ffn_out = ffn_w_in.astype(BF16), ffn_w_out.astype(BF16)
    w_proj, w_gate = _pack_mix_w_in(mix_w_in)
    br_a, br_b, br_c = br_w_nsa.astype(BF16), br_w_dil.astype(BF16), br_w_fox.astype(BF16)
    w_mix_out = mix_w_out.astype(BF16)
    h = x.reshape(T, D)
    for l in range(L):
        mod = mod_all[l]
        h = _ffn(h, mod, norm_g[l, 0].reshape(1, D), ffn_in, ffn_out, fg,
                 layer=l, which=0, mod_base=0, final=False, seq=S)

        g1 = norm_g[l, 1].reshape(1, D)
        (nsa_qt, cmp_kv, sel_kv, sel_vt, win_kv, win_vt, dil_q, dil_k, dil_v, fox_q, fox_k, fox_vt,
         misc) = _mixer_proj(h, mod, g1, w_proj, tabs, layer=l, batch=B, seq=S, nsa_q=NSA_Q)

        def bsd(a):
            return a.reshape(B, S, a.shape[-1])

        misc = bsd(misc)
        cmp_out, cmp_vt = _compress(bsd(cmp_kv), *_pack_compress(nsa_cmp_pe[l], nsa_cmp_w1[l], nsa_cmp_w2[l]))
        y_a = _nsa_attention(nsa_qt, cmp_out, cmp_vt, bsd(sel_kv), sel_vt, bsd(win_kv), win_vt, misc, ovlt)
        dil_parts = _dilated_attention(bsd(dil_q), bsd(dil_k), bsd(dil_v))
        bias_row = jnp.pad(fox_f_bias[l].reshape(1, FOX_HEADS),
                           ((0, 0), (MISC_FOX_LANE, LANES - MISC_FOX_LANE - FOX_HEADS)))
        k_aug, qt_aug = _fox_prep(misc, bias_row, bsd(fox_q), bsd(fox_k))
        y_c = _fox_attention(qt_aug, k_aug, fox_vt)

        h = _merge(h, mod, g1, y_a.reshape(T, NSA_Q_W), y_c.reshape(T, FOX_W), dil_parts,
                   w_gate, br_a, br_b, br_c, w_mix_out, layer=l, seq=S)

        h = _ffn(h, mod, norm_g[l, 2].reshape(1, D), ffn_in, ffn_out, fg,
                 layer=l, which=1, mod_base=6, final=(l == L - 1), seq=S)
    return h.reshape(B, S, D)
```

```python
import functools
import math

import numpy as np
import jax
import jax.numpy as jnp
from jax import lax
from jax.experimental import pallas as pl
from jax.experimental.pallas import tpu as pltpu

F32 = jnp.float32
BF16 = jnp.bfloat16

HEAD_DIM = 64
ROPE_DIM = 16
ROPE_HALF = ROPE_DIM // 2
ROPE_THETA = 500000.0
Q_BLOCK = 128
NEG_INF = -1e30
RMS_EPS = 1e-6
QK_SCALE = HEAD_DIM ** -0.5
LOG2E = math.log2(math.e)

NSA_HEADS = 4
NSA_CMP_LEN = 32
NSA_CMP_STRIDE = 16
NSA_CMP_HIDDEN = 128
NSA_SEL_LEN = 64
NSA_SEL_TOPK = 16
NSA_WINDOW = 512
NSA_FORCE_SCORE = 1e4

DIL_PATTERNS = ((128, 1), (512, 4), (2048, 16))
DIL_HEADS_PER_GROUP = 2
DIL_HEADS = DIL_HEADS_PER_GROUP * len(DIL_PATTERNS)
FOX_HEADS = 6

NSA_Q_W = NSA_HEADS * HEAD_DIM
DIL_W = DIL_HEADS * HEAD_DIM
DIL_OUT_W = DIL_HEADS_PER_GROUP * HEAD_DIM
FOX_W = FOX_HEADS * HEAD_DIM

LANES = 128
SUBLANES = 8
MISC_FOX_LANE = 0
MISC_GATE_LANE = 8
PROJ_TM = 1024
FFN_TM = 1024
FOX_AUG = 3
DIL_BLOCKS_PER_STEP = 16
NSA_Q = 512

PROJ_GROUPS = (
    ("nsa_q", NSA_Q_W, "A", QK_SCALE * LOG2E),
    ("cmp_kv", LANES, "B", 1.0),
    ("sel_kv", LANES, "B", 1.0),
    ("win_kv", LANES, "B", 1.0),
    ("dil_q", DIL_W, "A", QK_SCALE * LOG2E),
    ("dil_k", DIL_W, "A", 1.0),
    ("dil_v", DIL_W, None, 1.0),
    ("fox_q", FOX_W, None, QK_SCALE * LOG2E),
    ("fox_k", FOX_W, None, 1.0),
    ("fox_v", FOX_W, None, 1.0),
    ("misc", LANES, None, 1.0),
)
PROJ_COLS = sum(g[1] for g in PROJ_GROUPS)

VMEM_LIMIT = 56 * 1024 * 1024


def _dot(a, b):
    return jnp.dot(a, b, preferred_element_type=F32)


def _dot_nt(a, b):
    return lax.dot_general(a, b, (((1,), (1,)), ((), ())), preferred_element_type=F32)


def _norm_modulate(x, g, shift, scale):
    ms = jnp.mean(x * x, axis=-1, keepdims=True)
    y = x * lax.rsqrt(ms + RMS_EPS) * g
    return y * (1.0 + scale) + shift


def _masked_softmax(s, mask):
    s = jnp.where(mask, s, NEG_INF)
    m = jnp.max(s, axis=-1, keepdims=True)
    e = jnp.where(mask, jnp.exp(s - m), 0.0)
    den = jnp.sum(e, axis=-1, keepdims=True)
    den = jnp.where(den > 0, den, 1.0)
    return e / den, m + jnp.log(den)


def _params(*sem):
    return pltpu.CompilerParams(dimension_semantics=sem, vmem_limit_bytes=VMEM_LIMIT)


def _ada_kernel(c_ref, w_ref, b_ref, o_ref):
    c = c_ref[...]
    cond = c * jax.nn.sigmoid(c)
    o_ref[...] = jnp.dot(cond, w_ref[...], preferred_element_type=F32,
                         precision=lax.Precision.HIGHEST) + b_ref[...]


def _ada_modulation(c, ada_w, ada_b):
    L, D, N = ada_w.shape
    B = c.shape[0]
    tn = 1152
    assert N % tn == 0
    return pl.pallas_call(
        _ada_kernel,
        grid=(L, N // tn),
        in_specs=[
            pl.BlockSpec((B, D), lambda l, j: (0, 0)),
            pl.BlockSpec((None, D, tn), lambda l, j: (l, 0, j)),
            pl.BlockSpec((None, 1, tn), lambda l, j: (l, 0, j)),
        ],
        out_specs=pl.BlockSpec((None, B, tn), lambda l, j: (l, 0, j)),
        out_shape=jax.ShapeDtypeStruct((L, B, N), F32),
        compiler_params=_params("arbitrary", "arbitrary"),
        name="ada_modulation",
    )(c, ada_w, ada_b.reshape(L, 1, N))


def _ffn_kernel(x_ref, mod_ref, g_ref, win_ref, wout_ref, fg_ref, o_ref, a_ref, *, mod_base, d_ff, chunk, final):
    x = x_ref[...]
    n = _norm_modulate(x, g_ref[...], mod_ref[mod_base:mod_base + 1, :],
                       mod_ref[mod_base + 1:mod_base + 2, :]).astype(BF16)
    for j in range(d_ff // chunk):
        gate = _dot(n, win_ref[:, j * chunk:(j + 1) * chunk])
        up = _dot(n, win_ref[:, d_ff + j * chunk:d_ff + (j + 1) * chunk])
        a_ref[:, j * chunk:(j + 1) * chunk] = (gate * jax.nn.sigmoid(gate) * up).astype(BF16)
    f = _dot(a_ref[...], wout_ref[...])
    out = x + (0.5 * mod_ref[mod_base + 2:mod_base + 3, :]) * f
    if final:
        ms = jnp.mean(out * out, axis=-1, keepdims=True)
        out = out * lax.rsqrt(ms + RMS_EPS) * fg_ref[...]
    o_ref[...] = out


def _ffn(h, mod, g, w_in, w_out, final_g, *, layer, which, mod_base, final, seq, tm=FFN_TM):
    T, D = h.shape
    d_ff = w_out.shape[2]
    chunk = 256
    assert T % tm == 0 and seq % tm == 0 and d_ff % chunk == 0
    tpb = seq // tm
    kern = functools.partial(_ffn_kernel, mod_base=mod_base, d_ff=d_ff, chunk=chunk, final=final)
    return pl.pallas_call(
        kern,
        grid=(T // tm,),
        in_specs=[
            pl.BlockSpec((tm, D), lambda i: (i, 0)),
            pl.BlockSpec((None, 9, D), lambda i: (i // tpb, 0, 0)),
            pl.BlockSpec((1, D), lambda i: (0, 0)),
            pl.BlockSpec((None, None, D, 2 * d_ff), lambda i: (layer, which, 0, 0), pipeline_mode=pl.Buffered(1)),
            pl.BlockSpec((None, None, d_ff, D), lambda i: (layer, which, 0, 0), pipeline_mode=pl.Buffered(1)),
            pl.BlockSpec((1, D), lambda i: (0, 0)),
        ],
        out_specs=pl.BlockSpec((tm, D), lambda i: (i, 0)),
        out_shape=jax.ShapeDtypeStruct((T, D), F32),
        scratch_shapes=[pltpu.VMEM((tm, d_ff), BF16)],
        compiler_params=pltpu.CompilerParams(
            dimension_semantics=("arbitrary",), vmem_limit_bytes=VMEM_LIMIT,
            allow_input_fusion=[False, False, False, True, True, False]),
        name="ffn",
    )(h, mod, g, w_in, w_out, final_g)


def _rope_group(v, c, s1, s2):
    return v * c + pltpu.roll(v, LANES - ROPE_HALF, 1) * s1 + pltpu.roll(v, ROPE_HALF, 1) * s2


def _proj_kernel(x_ref, mod_ref, g_ref, w_ref, tc_ref, ts1_ref, ts2_ref,
                 nsa_qt_ref, cmp_ref, sel_ref, selvt_ref, win_ref, winvt_ref,
                 dq_ref, dk_ref, dv_ref, fq_ref, fk_ref, fvt_ref, misc_ref):
    x = x_ref[...]
    tm = x.shape[0]
    nblk = tm // Q_BLOCK
    n = _norm_modulate(x, g_ref[...], mod_ref[3:4, :], mod_ref[4:5, :]).astype(BF16)

    wide = {}

    def group(name, sub):
        off = 0
        for gname, width, rope, scale in PROJ_GROUPS:
            if gname == name:
                break
            off += width
        lo = off + sub * LANES
        blk = lo // (2 * LANES)
        if blk not in wide:
            wide[blk] = _dot(n, w_ref[:, blk * 2 * LANES:(blk + 1) * 2 * LANES])
        v = wide[blk][:, lo % (2 * LANES):lo % (2 * LANES) + LANES]
        if rope is not None:
            t0 = 0 if rope == "A" else LANES
            v = _rope_group(v, tc_ref[:, t0:t0 + LANES], ts1_ref[:, t0:t0 + LANES], ts2_ref[:, t0:t0 + LANES])
        if scale != 1.0:
            v = v * scale
        return v

    nq = nsa_qt_ref.shape[2] // NSA_HEADS
    zero_half = jnp.zeros((HEAD_DIM, NSA_HEADS * nq), BF16)
    for u in range(tm // nq):
        nsa_qt_ref[u, HEAD_DIM:2 * HEAD_DIM, :] = zero_half
    for sub in range(NSA_Q_W // LANES):
        vt = group("nsa_q", sub).T.astype(BF16)
        for hh in range(2):
            h = 2 * sub + hh
            for u in range(tm // nq):
                nsa_qt_ref[u, 0:HEAD_DIM, h * nq:(h + 1) * nq] = \
                    vt[hh * HEAD_DIM:(hh + 1) * HEAD_DIM, u * nq:(u + 1) * nq]

    cmp_ref[...] = group("cmp_kv", 0)
    for name, kv_ref, vt_ref in (("sel_kv", sel_ref, selvt_ref), ("win_kv", win_ref, winvt_ref)):
        v = group(name, 0)
        kv_ref[...] = v.astype(BF16)
        vt = v.T.astype(BF16)
        for u in range(nblk):
            vt_ref[u] = vt[HEAD_DIM:2 * HEAD_DIM, u * Q_BLOCK:(u + 1) * Q_BLOCK]

    for name, o_ref in (("dil_q", dq_ref), ("dil_k", dk_ref), ("dil_v", dv_ref), ("fox_q", fq_ref),
                        ("fox_k", fk_ref)):
        for sub in range(o_ref.shape[1] // LANES):
            o_ref[:, sub * LANES:(sub + 1) * LANES] = group(name, sub).astype(o_ref.dtype)

    for sub in range(FOX_W // LANES):
        vt = group("fox_v", sub).T.astype(BF16)
        for hh in range(2):
            for u in range(nblk):
                fvt_ref[2 * sub + hh, u] = vt[hh * HEAD_DIM:(hh + 1) * HEAD_DIM, u * Q_BLOCK:(u + 1) * Q_BLOCK]

    misc_ref[...] = group("misc", 0)


def _mixer_proj(h, mod, g, w, tabs, *, layer, batch, seq, nsa_q, tm=PROJ_TM):
    T, D = h.shape
    assert T % tm == 0 and seq % tm == 0 and tm % Q_BLOCK == 0 and tm % nsa_q == 0
    tpb = seq // tm
    nblk = tm // Q_BLOCK
    nqb = seq // Q_BLOCK
    tab_spec = pl.BlockSpec((tm, 2 * LANES), lambda i: (i % tpb, 0))

    def flat(width):
        return pl.BlockSpec((tm, width), lambda i: (i, 0))

    vt_spec = pl.BlockSpec((None, nblk, HEAD_DIM, Q_BLOCK), lambda i: (i // tpb, i % tpb, 0, 0))
    out_specs = [
        pl.BlockSpec((None, tm // nsa_q, 2 * HEAD_DIM, NSA_HEADS * nsa_q), lambda i: (i // tpb, i % tpb, 0, 0)),
        flat(LANES), flat(LANES), vt_spec, flat(LANES), vt_spec,
        flat(DIL_W), flat(DIL_W), flat(DIL_W), flat(FOX_W), flat(FOX_W),
        pl.BlockSpec((None, FOX_HEADS, nblk, HEAD_DIM, Q_BLOCK), lambda i: (i // tpb, 0, i % tpb, 0, 0)),
        flat(LANES),
    ]
    vt_shape = jax.ShapeDtypeStruct((batch, nqb, HEAD_DIM, Q_BLOCK), BF16)
    out_shape = [
        jax.ShapeDtypeStruct((batch, seq // nsa_q, 2 * HEAD_DIM, NSA_HEADS * nsa_q), BF16),
        jax.ShapeDtypeStruct((T, LANES), F32),
        jax.ShapeDtypeStruct((T, LANES), BF16), vt_shape,
        jax.ShapeDtypeStruct((T, LANES), BF16), vt_shape,
        jax.ShapeDtypeStruct((T, DIL_W), F32), jax.ShapeDtypeStruct((T, DIL_W), F32),
        jax.ShapeDtypeStruct((T, DIL_W), F32),
        jax.ShapeDtypeStruct((T, FOX_W), BF16), jax.ShapeDtypeStruct((T, FOX_W), BF16),
        jax.ShapeDtypeStruct((batch, FOX_HEADS, nqb, HEAD_DIM, Q_BLOCK), BF16),
        jax.ShapeDtypeStruct((T, LANES), F32),
    ]
    return pl.pallas_call(
        _proj_kernel,
        grid=(T // tm,),
        in_specs=[
            pl.BlockSpec((tm, D), lambda i: (i, 0)),
            pl.BlockSpec((None, 9, D), lambda i: (i // tpb, 0, 0)),
            pl.BlockSpec((1, D), lambda i: (0, 0)),
            pl.BlockSpec((None, D, PROJ_COLS), lambda i: (layer, 0, 0), pipeline_mode=pl.Buffered(1)),
            tab_spec, tab_spec, tab_spec,
        ],
        out_specs=out_specs,
        out_shape=out_shape,
        compiler_params=_params("arbitrary"),
        name="mixer_proj",
    )(h, mod, g, w, *tabs)


def _compress_kernel(x_ref, pet_ref, peb_ref, w1t_ref, w1b_ref, w2_ref, o_ref, vt_ref):
    nrow = x_ref.shape[0] // NSA_CMP_STRIDE
    a = jnp.zeros((nrow, 2 * NSA_CMP_HIDDEN), F32)
    b = jnp.zeros((nrow, 2 * NSA_CMP_HIDDEN), F32)
    for t in range(NSA_CMP_STRIDE):
        xt = x_ref[pl.ds(t, nrow, stride=NSA_CMP_STRIDE), :]
        a = a + _dot((xt + pet_ref[t:t + 1, :]).astype(BF16), w1t_ref[t])
        b = b + _dot((xt + peb_ref[t:t + 1, :]).astype(BF16), w1b_ref[t])
    hid = a + pltpu.roll(b, nrow - 1, 0)
    hid = hid * jax.nn.sigmoid(hid)
    out = _dot(hid.astype(BF16), w2_ref[...])
    o_ref[...] = out.astype(o_ref.dtype)
    vt_ref[...] = out.T[HEAD_DIM:2 * HEAD_DIM, :].astype(vt_ref.dtype)


def _compress(cmp_kv, pe_top, pe_bot, w1_top, w1_bot, w2):
    B, S, _ = cmp_kv.shape
    half = NSA_CMP_STRIDE
    nchunk = S // half
    hw = 2 * NSA_CMP_HIDDEN
    return pl.pallas_call(
        _compress_kernel,
        grid=(B,),
        in_specs=[
            pl.BlockSpec((None, S, LANES), lambda b: (b, 0, 0)),
            pl.BlockSpec((half, LANES), lambda b: (0, 0)),
            pl.BlockSpec((half, LANES), lambda b: (0, 0)),
            pl.BlockSpec((half, LANES, hw), lambda b: (0, 0, 0)),
            pl.BlockSpec((half, LANES, hw), lambda b: (0, 0, 0)),
            pl.BlockSpec((hw, LANES), lambda b: (0, 0)),
        ],
        out_specs=[
            pl.BlockSpec((None, nchunk, LANES), lambda b: (b, 0, 0)),
            pl.BlockSpec((None, HEAD_DIM, nchunk), lambda b: (b, 0, 0)),
        ],
        out_shape=[
            jax.ShapeDtypeStruct((B, nchunk, LANES), BF16),
            jax.ShapeDtypeStruct((B, HEAD_DIM, nchunk), BF16),
        ],
        compiler_params=_params("arbitrary"),
        name="nsa_compress",
    )(cmp_kv, pe_top, pe_bot, w1_top, w1_bot, w2)


def _nsa_kernel(qt_ref, cmp_ref, cmpvt_ref, sel_ref, selvt_ref, win_ref, winvt_ref, misc_ref, ovlt_ref,
                o_ref, member_ref, cnt_ref, sa_ref, sb_ref, m_ref, l_ref, acc_ref, *, seq, tk):
    H = NSA_HEADS
    Q = qt_ref.shape[1] // H
    VB = Q_BLOCK
    HQ = H * Q
    q0 = pl.program_id(1) * Q
    qt = qt_ref[...]
    ncmp = cmp_ref.shape[0]
    n_sel = seq // NSA_SEL_LEN
    sel_shift = NSA_SEL_LEN.bit_length() - 1

    def heads(a):
        return jnp.concatenate([a] * H, axis=1)

    wlen = NSA_WINDOW + Q
    start = pl.multiple_of(jnp.maximum(q0 - NSA_WINDOW, 0), Q)
    s = _dot(win_ref[pl.ds(start, wlen), :], qt)
    kpos = start + lax.broadcasted_iota(jnp.int32, (wlen, Q), 0)
    tw = q0 + lax.broadcasted_iota(jnp.int32, (wlen, Q), 1)
    s = s + heads(jnp.where((kpos <= tw) & (kpos > tw - NSA_WINDOW), 0.0, NEG_INF))
    e = jnp.exp2(s - jnp.max(s, axis=0, keepdims=True))
    sblk = start // VB
    vt = jnp.concatenate([winvt_ref[sblk + u] for u in range(wlen // VB)], axis=1)
    o_win = _dot(vt, e.astype(BF16)) / jnp.sum(e, axis=0, keepdims=True)

    s = _dot(cmp_ref[...], qt)
    nn = lax.broadcasted_iota(jnp.int32, (ncmp, Q), 0)
    tt = q0 + lax.broadcasted_iota(jnp.int32, (ncmp, Q), 1)
    cmask = heads((nn * NSA_CMP_STRIDE + (NSA_CMP_LEN - 1) <= tt) & (nn < ncmp - 1))
    s = jnp.where(cmask, s, NEG_INF)
    m = jnp.max(s, axis=0, keepdims=True)
    e = jnp.where(cmask, jnp.exp2(s - m), 0.0)
    den = jnp.sum(e, axis=0, keepdims=True)
    p_cmp = e / jnp.where(den > 0, den, 1.0)
    o_cmp = _dot(cmpvt_ref[...], p_cmp.astype(BF16))

    psum = p_cmp[:, 0:Q] + p_cmp[:, Q:2 * Q] + p_cmp[:, 2 * Q:3 * Q] + p_cmp[:, 3 * Q:4 * Q]
    hi = psum.astype(BF16)
    lo = (psum - hi.astype(F32)).astype(BF16)
    ovlt = ovlt_ref[...]
    imp = _dot(ovlt, hi) + _dot(ovlt, lo)
    jj = lax.broadcasted_iota(jnp.int32, (n_sel, Q), 0)
    tq = q0 + lax.broadcasted_iota(jnp.int32, (n_sel, Q), 1)
    valid = jj * NSA_SEL_LEN <= tq
    forced = (jj == (tq >> sel_shift)) | (jj == 0)
    imp = jnp.where(forced, NSA_FORCE_SCORE, jnp.where(valid, imp, -1.0))
    bpt = tk // NSA_SEL_LEN
    vpt = tk // VB
    n_full = q0 // tk

    def sel_scores(k, dst_ref):
        dst_ref[...] = _dot(sel_ref[pl.ds(pl.multiple_of(k * tk, tk), tk), :], qt)

    sel_scores(0, sa_ref)
    m_ref[...] = jnp.full((1, HQ), NEG_INF, F32)
    l_ref[...] = jnp.zeros(l_ref.shape, F32)
    acc_ref[...] = jnp.zeros(acc_ref.shape, F32)

    ngrp = n_sel // SUBLANES
    grp = [imp[r * SUBLANES:(r + 1) * SUBLANES, :] for r in range(ngrp)]
    jrow = lax.broadcasted_iota(jnp.int32, (SUBLANES, Q), 0)
    cnt_ref[...] = jnp.zeros((n_sel, Q), F32)
    last_started = (q0 + Q - 1) >> sel_shift
    for ib in range(ngrp):
        @pl.when(ib * SUBLANES <= last_started)
        def _(ib=ib):
            cnt = [cnt_ref[r * SUBLANES:(r + 1) * SUBLANES, :] for r in range(ngrp)]
            for i in range(ib * SUBLANES, (ib + 1) * SUBLANES):
                row = jnp.broadcast_to(imp[i:i + 1, :], (SUBLANES, Q))
                for r in range(ngrp):
                    if r > ib:
                        hit = jnp.where(row >= grp[r], 1.0, 0.0)
                    elif r < ib:
                        hit = jnp.where(row > grp[r], 1.0, 0.0)
                    else:
                        hit = jnp.where(jrow + r * SUBLANES > i, jnp.where(row >= grp[r], 1.0, 0.0),
                                        jnp.where(row > grp[r], 1.0, 0.0))
                    cnt[r] = cnt[r] + hit
            for r in range(ngrp):
                cnt_ref[r * SUBLANES:(r + 1) * SUBLANES, :] = cnt[r]
    top_k = min(NSA_SEL_TOPK, n_sel)
    member_ref[...] = jnp.where(cnt_ref[...] < top_k, 0.0, NEG_INF)

    def sel_update(k, src_ref, causal):
        bias = jnp.concatenate(
            [jnp.broadcast_to(member_ref[pl.ds(k * bpt + jb, 1), :], (NSA_SEL_LEN, Q)) for jb in range(bpt)],
            axis=0)
        if causal:
            kpos = k * tk + lax.broadcasted_iota(jnp.int32, (tk, Q), 0)
            tcol = q0 + lax.broadcasted_iota(jnp.int32, (tk, Q), 1)
            bias = jnp.where(kpos <= tcol, bias, NEG_INF)
        s = src_ref[...] + heads(bias)
        m = m_ref[...]
        m_new = jnp.maximum(m, jnp.max(s, axis=0, keepdims=True))
        e = jnp.exp2(s - m_new)
        alpha = jnp.exp2(m - m_new)
        l_ref[...] = alpha * l_ref[...] + jnp.sum(e, axis=0, keepdims=True)
        vt = jnp.concatenate([selvt_ref[k * vpt + u] for u in range(vpt)], axis=1)
        acc_ref[...] = alpha * acc_ref[...] + _dot(vt, e.astype(BF16))
        m_ref[...] = m_new

    def sel_pair(j, _):
        sel_scores(2 * j + 1, sb_ref)
        sel_update(2 * j, sa_ref, False)
        sel_scores(2 * j + 2, sa_ref)
        sel_update(2 * j + 1, sb_ref, False)
        return 0

    lax.fori_loop(0, n_full // 2, sel_pair, 0)

    @pl.when(n_full % 2 == 1)
    def _():
        sel_scores(n_full, sb_ref)
        sel_update(n_full - 1, sa_ref, False)
        sel_update(n_full, sb_ref, True)

    @pl.when(n_full % 2 == 0)
    def _():
        sel_update(n_full, sa_ref, True)

    o_sel = acc_ref[...] / l_ref[...]

    g = jax.nn.sigmoid(misc_ref[...].T)
    outs = []
    for h in range(H):
        r0 = MISC_GATE_LANE + 3 * h
        cols = slice(h * Q, (h + 1) * Q)
        outs.append(g[r0:r0 + 1, :] * o_cmp[:, cols] + g[r0 + 1:r0 + 2, :] * o_sel[:, cols]
                    + g[r0 + 2:r0 + 3, :] * o_win[:, cols])
    o_ref[...] = jnp.concatenate(outs, axis=0).T.astype(o_ref.dtype)


def _nsa_attention(qt, cmp_out, cmp_vt, sel_kv, sel_vt, win_kv, win_vt, misc, ovlt, *, tk=512):
    B, S, _ = sel_kv.shape
    ncmp = cmp_out.shape[1]
    n_sel = S // NSA_SEL_LEN
    nvb = S // Q_BLOCK
    nq = qt.shape[3] // NSA_HEADS
    assert S % tk == 0 and tk % nq == 0 and NSA_WINDOW % nq == 0 and nq % Q_BLOCK == 0
    assert S >= NSA_WINDOW + nq and n_sel % SUBLANES == 0
    kern = functools.partial(_nsa_kernel, seq=S, tk=tk)
    return pl.pallas_call(
        kern,
        grid=(B, S // nq),
        in_specs=[
            pl.BlockSpec((None, None, 2 * HEAD_DIM, NSA_HEADS * nq), lambda b, i: (b, i, 0, 0)),
            pl.BlockSpec((None, ncmp, LANES), lambda b, i: (b, 0, 0)),
            pl.BlockSpec((None, HEAD_DIM, ncmp), lambda b, i: (b, 0, 0)),
            pl.BlockSpec((None, S, LANES), lambda b, i: (b, 0, 0)),
            pl.BlockSpec((None, nvb, HEAD_DIM, Q_BLOCK), lambda b, i: (b, 0, 0, 0)),
            pl.BlockSpec((None, S, LANES), lambda b, i: (b, 0, 0)),
            pl.BlockSpec((None, nvb, HEAD_DIM, Q_BLOCK), lambda b, i: (b, 0, 0, 0)),
            pl.BlockSpec((None, nq, LANES), lambda b, i: (b, i, 0)),
            pl.BlockSpec((n_sel, ncmp), lambda b, i: (0, 0)),
        ],
        out_specs=pl.BlockSpec((None, nq, NSA_Q_W), lambda b, i: (b, i, 0)),
        out_shape=jax.ShapeDtypeStruct((B, S, NSA_Q_W), BF16),
        scratch_shapes=[
            pltpu.VMEM((n_sel, nq), F32),
            pltpu.VMEM((n_sel, nq), F32),
            pltpu.VMEM((tk, NSA_HEADS * nq), F32),
            pltpu.VMEM((tk, NSA_HEADS * nq), F32),
            pltpu.VMEM((1, NSA_HEADS * nq), F32),
            pltpu.VMEM((1, NSA_HEADS * nq), F32),
            pltpu.VMEM((HEAD_DIM, NSA_HEADS * nq), F32),
        ],
        compiler_params=_params("arbitrary", "arbitrary"),
        name="nsa_attention",
    )(qt, cmp_out, cmp_vt, sel_kv, sel_vt, win_kv, win_vt, misc, ovlt)


def _dil_kernel(q_ref, k_ref, v_ref, o_ref, lse_ref, vt_ref, *, band, dil, cps):
    Q = Q_BLOCK
    r0 = pl.program_id(1) * cps
    nblk = q_ref.shape[0] // (dil * Q)
    row = lax.broadcasted_iota(jnp.int32, (LANES, Q), 0)

    def rows(c, first, count):
        if dil == 1:
            return pl.ds(first, count)
        return pl.ds(first * dil + r0 + c, count, stride=dil)

    per_step = min(DIL_BLOCKS_PER_STEP // cps, nblk)

    def transpose_v(j, _):
        for c in range(cps):
            for u in range(per_step):
                jb = per_step * j + u
                vt_ref[c * nblk + jb] = v_ref[rows(c, jb * Q, Q), :].T.astype(BF16)
        return 0

    lax.fori_loop(0, nblk // per_step, transpose_v, 0)

    def scores(c, jq):
        kb = jnp.maximum(jq - 1, 0)
        qt = q_ref[rows(c, jq * Q, Q), :].T
        rhs = jnp.concatenate([jnp.where(row < HEAD_DIM, qt, 0.0), jnp.where(row >= HEAD_DIM, qt, 0.0)],
                              axis=1).astype(BF16)
        return _dot(k_ref[rows(c, kb * Q, 2 * Q), :].astype(BF16), rhs)

    def attend(c, jq, s):
        kb = jnp.maximum(jq - 1, 0)
        dist = (jq - kb) * Q + lax.broadcasted_iota(jnp.int32, (2 * Q, Q), 1) \
            - lax.broadcasted_iota(jnp.int32, (2 * Q, Q), 0)
        bias = jnp.where((dist >= 0) & (dist <= band), 0.0, NEG_INF)
        s = s + jnp.concatenate([bias, bias], axis=1)
        m = jnp.max(s, axis=0, keepdims=True)
        e = jnp.exp2(s - m)
        l = jnp.sum(e, axis=0, keepdims=True)
        eb = e.astype(BF16)
        vt = jnp.concatenate([vt_ref[c * nblk + kb], vt_ref[c * nblk + kb + 1]], axis=1)
        return m, l, _dot(vt[0:HEAD_DIM], eb[:, 0:Q]), _dot(vt[HEAD_DIM:2 * HEAD_DIM], eb[:, Q:2 * Q])

    def emit(c, jq, m, l, pv0, pv1):
        inv = 1.0 / l
        lse = m * (1.0 / LOG2E) + jnp.log(l)
        lt = jnp.concatenate([jnp.broadcast_to(lse[:, 0:Q], (HEAD_DIM, Q)),
                              jnp.broadcast_to(lse[:, Q:2 * Q], (HEAD_DIM, Q))], axis=0)
        ot = jnp.concatenate([pv0 * inv[:, 0:Q], pv1 * inv[:, Q:2 * Q]], axis=0)
        o_ref[rows(c, jq * Q, Q), :] = ot.T
        lse_ref[rows(c, jq * Q, Q), :] = lt.T

    def step(j, _):
        blocks = [(c, per_step * j + u) for c in range(cps) for u in range(per_step)]
        ss = [scores(c, jq) for c, jq in blocks]
        parts = [attend(c, jq, s) for (c, jq), s in zip(blocks, ss)]
        for (c, jq), part in zip(blocks, parts):
            emit(c, jq, *part)
        return 0

    lax.fori_loop(0, nblk // per_step, step, 0)


def _dilated_group(q, k, v, group, window, dil):
    B, S, W = q.shape
    nblk = S // dil // Q_BLOCK
    cps = min(dil, max(1, DIL_BLOCKS_PER_STEP // nblk))
    per_step = min(DIL_BLOCKS_PER_STEP // cps, nblk)
    assert dil & (dil - 1) == 0 and window // dil == Q_BLOCK and nblk >= 2
    assert dil % cps == 0 and nblk % per_step == 0
    in_spec = pl.BlockSpec((None, S, LANES), lambda b, r: (b, 0, group))
    out_spec = pl.BlockSpec((None, S, LANES), lambda b, r: (b, 0, 0))
    out_sds = jax.ShapeDtypeStruct((B, S, LANES), F32)
    o, lse = pl.pallas_call(
        functools.partial(_dil_kernel, band=window // dil, dil=dil, cps=cps),
        grid=(B, dil // cps),
        in_specs=[in_spec, in_spec, in_spec],
        out_specs=[out_spec, out_spec],
        out_shape=[out_sds, out_sds],
        scratch_shapes=[pltpu.VMEM((cps * nblk, LANES, Q_BLOCK), BF16)],
        compiler_params=_params("arbitrary", "arbitrary"),
        name=f"dilated_attention_d{dil}",
    )(q, k, v)
    return o.reshape(B * S, LANES), lse.reshape(B * S, LANES)


def _dilated_attention(q, k, v):
    outs = [_dilated_group(q, k, v, g, window, dil) for g, (window, dil) in enumerate(DIL_PATTERNS)]
    return [o for o, _ in outs] + [l for _, l in outs]


def _fox_prep_kernel(misc_ref, bias_ref, q_ref, k_ref, pq_ref, pk_ref, kaug_ref, qtaug_ref, carry_ref):
    tb = misc_ref.shape[0]

    @pl.when(pl.program_id(1) == 0)
    def _():
        carry_ref[...] = jnp.zeros_like(carry_ref)

    r = lax.broadcasted_iota(jnp.int32, (LANES, LANES), 0)
    c = lax.broadcasted_iota(jnp.int32, (LANES, LANES), 1)
    tri = jnp.where(r >= c, 1.0, 0.0).astype(F32)
    carry = carry_ref[...]
    cums = []
    for blk in range(tb // LANES):
        x = misc_ref[blk * LANES:(blk + 1) * LANES, :] + bias_ref[...]
        log_f = -(jnp.maximum(-x, 0.0) + jnp.log1p(jnp.exp(-jnp.abs(x))))
        cs = jnp.dot(tri, log_f, preferred_element_type=F32, precision=lax.Precision.HIGHEST) + carry
        cums.append(cs)
        carry = cs[LANES - 1:LANES, :]
    carry_ref[...] = carry
    lane = lax.broadcasted_iota(jnp.int32, (tb, LANES), 1)
    cum = jnp.where(lane < FOX_HEADS, jnp.concatenate(cums, axis=0) * LOG2E, 0.0)
    hi = cum.astype(BF16).astype(F32)
    r1 = cum - hi
    mid = r1.astype(BF16).astype(F32)
    lo = r1 - mid
    pieces = (hi + pltpu.roll(mid, FOX_HEADS, 1) + pltpu.roll(lo, 2 * FOX_HEADS, 1)
              + jnp.where(lane == FOX_AUG * FOX_HEADS, 1.0, 0.0)).astype(BF16)
    for h in range(FOX_HEADS):
        g0 = (h // 2) * LANES
        q_aug = _dot(jnp.concatenate([q_ref[:, g0:g0 + LANES], pieces], axis=1), pq_ref[h])
        k_aug = _dot(jnp.concatenate([k_ref[:, g0:g0 + LANES], pieces], axis=1), pk_ref[h])
        kaug_ref[h] = k_aug.astype(BF16)
        qtaug_ref[h] = q_aug.T.astype(BF16)


def _fox_placement():
    a0, nh = HEAD_DIM, FOX_HEADS
    pq = np.zeros((nh, 2 * LANES, LANES), np.float32)
    pk = np.zeros((nh, 2 * LANES, LANES), np.float32)
    one = LANES + FOX_AUG * nh
    for h in range(nh):
        for d in range(HEAD_DIM):
            pq[h, (h % 2) * HEAD_DIM + d, d] = 1.0
            pk[h, (h % 2) * HEAD_DIM + d, d] = 1.0
        for j in range(FOX_AUG):
            pq[h, LANES + j * nh + h, a0 + j] = 1.0
            pq[h, one, a0 + FOX_AUG + j] = 1.0
            pk[h, one, a0 + j] = 1.0
            pk[h, LANES + j * nh + h, a0 + FOX_AUG + j] = -1.0
    return jnp.asarray(pq, BF16), jnp.asarray(pk, BF16)


def _fox_prep(misc, bias_row, q, k, *, tb=PROJ_TM):
    B, S, _ = misc.shape
    assert S % tb == 0
    place_spec = pl.BlockSpec((FOX_HEADS, 2 * LANES, LANES), lambda b, i: (0, 0, 0))
    return pl.pallas_call(
        _fox_prep_kernel,
        grid=(B, S // tb),
        in_specs=[
            pl.BlockSpec((None, tb, LANES), lambda b, i: (b, i, 0)),
            pl.BlockSpec((1, LANES), lambda b, i: (0, 0)),
            pl.BlockSpec((None, tb, FOX_W), lambda b, i: (b, i, 0)),
            pl.BlockSpec((None, tb, FOX_W), lambda b, i: (b, i, 0)),
            place_spec, place_spec,
        ],
        out_specs=[
            pl.BlockSpec((None, FOX_HEADS, tb, LANES), lambda b, i: (b, 0, i, 0)),
            pl.BlockSpec((None, FOX_HEADS, LANES, tb), lambda b, i: (b, 0, 0, i)),
        ],
        out_shape=[
            jax.ShapeDtypeStruct((B, FOX_HEADS, S, LANES), BF16),
            jax.ShapeDtypeStruct((B, FOX_HEADS, LANES, S), BF16),
        ],
        scratch_shapes=[pltpu.VMEM((1, LANES), F32)],
        compiler_params=_params("arbitrary", "arbitrary"),
        name="fox_prep",
    )(misc, bias_row, q, k, *_fox_placement())


def _fox_kernel(qt_ref, k_ref, vt_ref, o_ref, sa_ref, sb_ref, m_ref, l_ref, acc_ref, *, tq, tk):
    nh = qt_ref.shape[0]
    q0 = pl.program_id(2) * tq
    n_full = q0 // tk
    vpt = tk // Q_BLOCK

    def scores(k, dst_ref):
        base = pl.multiple_of(k * tk, tk)
        for h in range(nh):
            dst_ref[h] = _dot(k_ref[h, pl.ds(base, tk), :], qt_ref[h])

    def update(k, src_ref, causal):
        if causal:
            kpos = k * tk + lax.broadcasted_iota(jnp.int32, (tk, tq), 0)
            tcol = q0 + lax.broadcasted_iota(jnp.int32, (tk, tq), 1)
            causal_bias = jnp.where(kpos <= tcol, 0.0, NEG_INF)
        for h in range(nh):
            s = src_ref[h]
            if causal:
                s = s + causal_bias
            m = m_ref[h]
            m_new = jnp.maximum(m, jnp.max(s, axis=0, keepdims=True))
            e = jnp.exp2(s - m_new)
            alpha = jnp.exp2(m - m_new)
            l_ref[h] = alpha * l_ref[h] + jnp.sum(e, axis=0, keepdims=True)
            vt = jnp.concatenate([vt_ref[h, k * vpt + u] for u in range(vpt)], axis=1)
            acc_ref[h] = alpha * acc_ref[h] + _dot(vt, e.astype(BF16))
            m_ref[h] = m_new

    scores(0, sa_ref)
    m_ref[...] = jnp.full(m_ref.shape, NEG_INF, F32)
    l_ref[...] = jnp.zeros(l_ref.shape, F32)
    acc_ref[...] = jnp.zeros(acc_ref.shape, F32)

    def pair(j, _):
        scores(2 * j + 1, sb_ref)
        update(2 * j, sa_ref, False)
        scores(2 * j + 2, sa_ref)
        update(2 * j + 1, sb_ref, False)
        return 0

    lax.fori_loop(0, n_full // 2, pair, 0)

    @pl.when(n_full % 2 == 1)
    def _():
        scores(n_full, sb_ref)
        update(n_full - 1, sa_ref, False)
        update(n_full, sb_ref, True)

    @pl.when(n_full % 2 == 0)
    def _():
        update(n_full, sa_ref, True)

    outs = [acc_ref[h] / l_ref[h] for h in range(nh)]
    o_ref[...] = jnp.concatenate(outs, axis=0).T.astype(o_ref.dtype)


def _fox_attention(qt_aug, k_aug, vt, *, tq=512, tk=512, heads_per_step=FOX_HEADS):
    B, H, S, _ = k_aug.shape
    hps = heads_per_step
    assert S % tk == 0 and tk % tq == 0 and tk % Q_BLOCK == 0 and H % hps == 0 and (hps * HEAD_DIM) % LANES == 0
    nqb = S // Q_BLOCK
    kern = functools.partial(_fox_kernel, tq=tq, tk=tk)
    return pl.pallas_call(
        kern,
        grid=(B, H // hps, S // tq),
        in_specs=[
            pl.BlockSpec((None, hps, LANES, tq), lambda b, p, i: (b, p, 0, i)),
            pl.BlockSpec((None, hps, S, LANES), lambda b, p, i: (b, p, 0, 0)),
            pl.BlockSpec((None, hps, nqb, HEAD_DIM, Q_BLOCK), lambda b, p, i: (b, p, 0, 0, 0)),
        ],
        out_specs=pl.BlockSpec((None, tq, hps * HEAD_DIM), lambda b, p, i: (b, i, p)),
        out_shape=jax.ShapeDtypeStruct((B, S, H * HEAD_DIM), BF16),
        scratch_shapes=[
            pltpu.VMEM((hps, tk, tq), F32),
            pltpu.VMEM((hps, tk, tq), F32),
            pltpu.VMEM((hps, 1, tq), F32),
            pltpu.VMEM((hps, 1, tq), F32),
            pltpu.VMEM((hps, HEAD_DIM, tq), F32),
        ],
        compiler_params=_params("arbitrary", "arbitrary", "arbitrary"),
        name="fox_attention",
    )(qt_aug, k_aug, vt)


def _merge_kernel(x_ref, mod_ref, g_ref, ya_ref, yc_ref, d0_ref, d1_ref, d2_ref, l0_ref, l1_ref, l2_ref,
                  wg_ref, bra_ref, brb_ref, brc_ref, wo_ref, o_ref):
    x = x_ref[...]
    D = x.shape[1]
    n = _norm_modulate(x, g_ref[...], mod_ref[3:4, :], mod_ref[4:5, :]).astype(BF16)
    lse = [l0_ref[...], l1_ref[...], l2_ref[...]]
    mx = jnp.maximum(jnp.maximum(lse[0], lse[1]), lse[2])
    w = [jnp.exp(l - mx) for l in lse]
    wsum = w[0] + w[1] + w[2]
    yb = (w[0] / wsum) * d0_ref[...] + (w[1] / wsum) * d1_ref[...] + (w[2] / wsum) * d2_ref[...]
    merged = jax.nn.sigmoid(_dot(n, wg_ref[:, 0:D])) * _dot(ya_ref[...], bra_ref[...])
    merged = merged + jax.nn.sigmoid(_dot(n, wg_ref[:, D:2 * D])) * _dot(yb.astype(BF16), brb_ref[...])
    merged = merged + jax.nn.sigmoid(_dot(n, wg_ref[:, 2 * D:3 * D])) * _dot(yc_ref[...], brc_ref[...])
    o_ref[...] = x + mod_ref[5:6, :] * _dot(merged.astype(BF16), wo_ref[...])


def _merge(h, mod, g, ya, yc, dil_parts, w_gate, br_a, br_b, br_c, w_out, *, layer, seq, tm=PROJ_TM):
    T, D = h.shape
    assert T % tm == 0 and seq % tm == 0 and len(dil_parts) == 2 * len(DIL_PATTERNS)
    tpb = seq // tm

    def resident(shape):
        return pl.BlockSpec((None,) + shape, lambda i: (layer, 0, 0), pipeline_mode=pl.Buffered(1))

    dil_spec = pl.BlockSpec((tm, DIL_OUT_W), lambda i: (i, 0))
    return pl.pallas_call(
        _merge_kernel,
        grid=(T // tm,),
        in_specs=[
            pl.BlockSpec((tm, D), lambda i: (i, 0)),
            pl.BlockSpec((None, 9, D), lambda i: (i // tpb, 0, 0)),
            pl.BlockSpec((1, D), lambda i: (0, 0)),
            pl.BlockSpec((tm, NSA_Q_W), lambda i: (i, 0)),
            pl.BlockSpec((tm, FOX_W), lambda i: (i, 0)),
            dil_spec, dil_spec, dil_spec, dil_spec, dil_spec, dil_spec,
            resident((D, 3 * D)),
            resident((NSA_Q_W, D)),
            resident((DIL_OUT_W, D)),
            resident((FOX_W, D)),
            resident((D, D)),
        ],
        out_specs=pl.BlockSpec((tm, D), lambda i: (i, 0)),
        out_shape=jax.ShapeDtypeStruct((T, D), F32),
        compiler_params=_params("arbitrary"),
        name="merge_out",
    )(h, mod, g, ya, yc, *dil_parts, w_gate, br_a, br_b, br_c, w_out)


def _rope_tables(seq):
    inv_freq = ROPE_THETA ** (-jnp.arange(0, ROPE_DIM, 2, dtype=F32) / ROPE_DIM)
    ang = jnp.arange(seq, dtype=F32)[:, None] * inv_freq[None, :]
    cos, sin = jnp.cos(ang), jnp.sin(ang)
    d = np.arange(LANES) % HEAD_DIM
    idx = d % ROPE_HALF
    first = jnp.asarray(d < ROPE_HALF)
    second = jnp.asarray((d >= ROPE_HALF) & (d < ROPE_DIM))
    c_a = jnp.where(first | second, cos[:, idx], 1.0)
    s1_a = jnp.where(first, -sin[:, idx], 0.0)
    s2_a = jnp.where(second, sin[:, idx], 0.0)
    head0 = jnp.asarray(np.arange(LANES) < HEAD_DIM)
    c_b = jnp.where(head0, c_a, 1.0)
    s1_b = jnp.where(head0, s1_a, 0.0)
    s2_b = jnp.where(head0, s2_a, 0.0)
    return (jnp.concatenate([c_a, c_b], axis=1), jnp.concatenate([s1_a, s1_b], axis=1),
            jnp.concatenate([s2_a, s2_b], axis=1))


def _pack_mix_kernel(w_ref, proj_ref, gate_ref):
    rows = w_ref.shape[0]
    head = NSA_Q_W + 6 * HEAD_DIM
    body = 3 * DIL_W + 3 * FOX_W
    g0 = head
    b0 = g0 + 3 * NSA_HEADS
    f0 = b0 + body
    m0 = f0 + FOX_HEADS
    proj_ref[:, 0:head] = w_ref[:, 0:head].astype(BF16)
    proj_ref[:, head:head + body] = w_ref[:, b0:b0 + body].astype(BF16)
    misc = jnp.concatenate([
        w_ref[:, f0:f0 + FOX_HEADS], jnp.zeros((rows, MISC_GATE_LANE - MISC_FOX_LANE - FOX_HEADS), F32),
        w_ref[:, g0:g0 + 3 * NSA_HEADS], jnp.zeros((rows, LANES - MISC_GATE_LANE - 3 * NSA_HEADS), F32)], axis=1)
    proj_ref[:, head + body:head + body + LANES] = misc.astype(BF16)
    gate_ref[...] = w_ref[:, m0:m0 + gate_ref.shape[1]].astype(BF16)


def _pack_mix_w_in(w, *, tr=256):
    L, D, C = w.shape
    gate_cols = C - (PROJ_COLS - LANES) - FOX_HEADS - 3 * NSA_HEADS
    assert D % tr == 0 and gate_cols % LANES == 0
    return pl.pallas_call(
        _pack_mix_kernel,
        grid=(L, D // tr),
        in_specs=[pl.BlockSpec((None, tr, C), lambda l, i: (l, i, 0))],
        out_specs=[pl.BlockSpec((None, tr, PROJ_COLS), lambda l, i: (l, i, 0)),
                   pl.BlockSpec((None, tr, gate_cols), lambda l, i: (l, i, 0))],
        out_shape=[jax.ShapeDtypeStruct((L, D, PROJ_COLS), BF16), jax.ShapeDtypeStruct((L, D, gate_cols), BF16)],
        compiler_params=_params("arbitrary", "arbitrary"),
        name="pack_mix_w_in",
    )(w)


def _pack_compress(pe, w1, w2):
    half = NSA_CMP_STRIDE
    hid = NSA_CMP_HIDDEN
    w1k = w1[0].reshape(NSA_CMP_LEN, HEAD_DIM, hid)
    w1v = w1[1].reshape(NSA_CMP_LEN, HEAD_DIM, hid)
    z = jnp.zeros((half, HEAD_DIM, hid), w1.dtype)

    def halfpack(lo):
        kk = jnp.concatenate([w1k[lo:lo + half], z], axis=-1)
        vv = jnp.concatenate([z, w1v[lo:lo + half]], axis=-1)
        return jnp.concatenate([kk, vv], axis=1).astype(BF16)

    def pepack(lo):
        return jnp.concatenate([pe[0, lo:lo + half], pe[1, lo:lo + half]], axis=-1)

    zz = jnp.zeros((hid, HEAD_DIM), w2.dtype)
    w2p = jnp.concatenate([jnp.concatenate([w2[0], zz], axis=1),
                           jnp.concatenate([zz, w2[1]], axis=1)], axis=0).astype(BF16)
    return pepack(0), pepack(half), halfpack(0), halfpack(half), w2p


def _overlap_matrix_t(seq):
    n_chunk = seq // NSA_CMP_STRIDE
    n_cmp = (seq - NSA_CMP_LEN) // NSA_CMP_STRIDE + 1
    n_sel = seq // NSA_SEL_LEN
    cmp_start = np.arange(n_cmp) * NSA_CMP_STRIDE
    sel_start = np.arange(n_sel) * NSA_SEL_LEN
    ov = np.minimum(cmp_start[:, None] + NSA_CMP_LEN, sel_start[None, :] + NSA_SEL_LEN) \
        - np.maximum(cmp_start[:, None], sel_start[None, :])
    full = np.zeros((n_sel, n_chunk), np.float32)
    full[:, :n_cmp] = (np.clip(ov, 0, None) / NSA_CMP_LEN).T
    return jnp.asarray(full, dtype=BF16)


def kernel(x, c, ada_w, ada_b, norm_g, final_norm_g, ffn_w_in, ffn_w_out, mix_w_in, nsa_cmp_pe, nsa_cmp_w1,
           nsa_cmp_w2, fox_f_bias, br_w_nsa, br_w_dil, br_w_fox, mix_w_out):
    B, S, D = x.shape
    L = ada_w.shape[0]
    T = B * S
    mod_all = _ada_modulation(c, ada_w, ada_b).reshape(L, B, 9, D)
    tabs = _rope_tables(S)
    ovlt = _overlap_matrix_t(S)
    fg = final_norm_g.reshape(1, D)
    ffn_in, ffn_out = ffn_w_in.astype(BF16), ffn_w_out.astype(BF16)
    w_proj, w_gate = _pack_mix_w_in(mix_w_in)
    br_a, br_b, br_c = br_w_nsa.astype(BF16), br_w_dil.astype(BF16), br_w_fox.astype(BF16)
    w_mix_out = mix_w_out.astype(BF16)
    h = x.reshape(T, D)
    for l in range(L):
        mod = mod_all[l]
        h = _ffn(h, mod, norm_g[l, 0].reshape(1, D), ffn_in, ffn_out, fg,
                 layer=l, which=0, mod_base=0, final=False, seq=S)

        g1 = norm_g[l, 1].reshape(1, D)
        (nsa_qt, cmp_kv, sel_kv, sel_vt, win_kv, win_vt, dil_q, dil_k, dil_v, fox_q, fox_k, fox_vt,
         misc) = _mixer_proj(h, mod, g1, w_proj, tabs, layer=l, batch=B, seq=S, nsa_q=NSA_Q)

        def bsd(a):
            return a.reshape(B, S, a.shape[-1])

        misc = bsd(misc)
        cmp_out, cmp_vt = _compress(bsd(cmp_kv), *_pack_compress(nsa_cmp_pe[l], nsa_cmp_w1[l], nsa_cmp_w2[l]))
        y_a = _nsa_attention(nsa_qt, cmp_out, cmp_vt, bsd(sel_kv), sel_vt, bsd(win_kv), win_vt, misc, ovlt)
        dil_parts = _dilated_attention(bsd(dil_q), bsd(dil_k), bsd(dil_v))
        bias_row = jnp.pad(fox_f_bias[l].reshape(1, FOX_HEADS),
                           ((0, 0), (MISC_FOX_LANE, LANES - MISC_FOX_LANE - FOX_HEADS)))
        k_aug, qt_aug = _fox_prep(misc, bias_row, bsd(fox_q), bsd(fox_k))
        y_c = _fox_attention(qt_aug, k_aug, fox_vt)

        h = _merge(h, mod, g1, y_a.reshape(T, NSA_Q_W), y_c.reshape(T, FOX_W), dil_parts,
                   w_gate, br_a, br_b, br_c, w_mix_out, layer=l, seq=S)

        h = _ffn(h, mod, norm_g[l, 2].reshape(1, D), ffn_in, ffn_out, fg,
                 layer=l, which=1, mod_base=6, final=(l == L - 1), seq=S)
    return h.reshape(B, S, D)
```
